```python
import math
import jax, jax.numpy as jnp
from jax import lax
import numpy as np

D_MODEL = 1024
BATCH = 2
SEQ = 8192
DEPTH = 2
DEC_BATCH = 1
DEC_SEQ = 16384
PAST_LEN = 128

ATTN_WIDTH = 512
POOL_WIDTH = D_MODEL - ATTN_WIDTH
N_DIFF_HEADS = 4
DIFF_HEAD_DIM = 64
V_HEAD_DIM = 2 * DIFF_HEAD_DIM
POOL_WINDOWS = (2, 4, 8, 16)
POOL_GROUP = POOL_WIDTH // len(POOL_WINDOWS)
D_FF = 2816
NUM_BUCKETS = 32
MAX_DISTANCE = 128
Q_BLOCK = 128
RMS_EPS = 1e-6
IN_WIDTH = 3 * ATTN_WIDTH + POOL_WIDTH

kernel_name = "hymba_diffattn_pool_convffn_encoder"


def rms_norm(x, g):
    xf = x.astype(jnp.float32)
    y = xf * lax.rsqrt(jnp.mean(xf * xf, axis=-1, keepdims=True) + RMS_EPS)
    return (y * g.astype(jnp.float32)).astype(x.dtype)


def rel_bucket(rel):
    nb = NUM_BUCKETS // 2
    max_exact = nb // 2
    ret = jnp.where(rel > 0, nb, 0)
    n = jnp.abs(rel)
    nf = jnp.maximum(n, 1).astype(jnp.float32)
    large = max_exact + (jnp.log(nf / max_exact) / math.log(MAX_DISTANCE / max_exact)
                         * (nb - max_exact)).astype(jnp.int32)
    large = jnp.minimum(large, nb - 1)
    return ret + jnp.where(n < max_exact, n, large)


def diff_attention(q, k, v, rel_bias, lam):
    B, S = q.shape[0], q.shape[1]
    nblk = S // Q_BLOCK
    scale = DIFF_HEAD_DIM ** -0.5
    qb = q.reshape(B, nblk, Q_BLOCK, N_DIFF_HEADS, 2, DIFF_HEAD_DIM).transpose(1, 0, 2, 3, 4, 5)
    kpos = jnp.arange(S, dtype=jnp.int32)

    def one_block(args):
        qblk, i = args
        qpos = i * Q_BLOCK + jnp.arange(Q_BLOCK, dtype=jnp.int32)
        bias = rel_bias[rel_bucket(kpos[None, :] - qpos[:, None])]
        bias = jnp.transpose(bias, (2, 0, 1)).astype(jnp.float32)
        s = jnp.einsum('bqhmd,bkhmd->bhmqk', qblk, k,
                       preferred_element_type=jnp.float32) * scale
        s = s + bias[None, :, None]
        p = jax.nn.softmax(s, axis=-1)
        w = p[:, :, 0] - lam * p[:, :, 1]
        return jnp.einsum('bhqk,bkhd->bqhd', w.astype(v.dtype), v)

    out = lax.map(one_block, (qb, jnp.arange(nblk, dtype=jnp.int32)))
    return out.transpose(1, 0, 2, 3, 4).reshape(B, S, N_DIFF_HEADS, V_HEAD_DIM)


def pool_mixer(p, w_pool, pool_scale):
    B, S, _ = p.shape
    pf = p.astype(jnp.float32)
    csum = jnp.concatenate([jnp.zeros((B, 1, POOL_WIDTH), jnp.float32),
                            jnp.cumsum(pf, axis=1)], axis=1)
    t = np.arange(S)
    outs = []
    for g, win in enumerate(POOL_WINDOWS):
        lo = np.clip(t - win // 2, 0, S - 1)
        hi = np.clip(t + win - win // 2 - 1, 0, S - 1)
        cs = slice(g * POOL_GROUP, (g + 1) * POOL_GROUP)
        cg = csum[..., cs]
        total = jnp.take(cg, hi + 1, axis=1) - jnp.take(cg, lo, axis=1)
        cnt = (hi - lo + 1).astype(np.float32)[None, :, None]
        outs.append(total / cnt - pf[..., cs])
    y = jnp.stack(outs, axis=2)
    y = jnp.einsum('bsgc,gcd->bsgd', y, w_pool.astype(jnp.float32)).reshape(B, S, POOL_WIDTH)
    return (y * pool_scale.astype(jnp.float32)).astype(p.dtype)


def encoder_layer(x, i, rel_bias, ln_mix_pre, ln_mix_post, w_in, lam_q, lam_k, head_norm,
                  w_pool, pool_scale, w_out, ln_ffn_pre, ln_ffn_post, w_up, conv_w, conv_b, w_down):
    B, S, _ = x.shape
    h = rms_norm(x, ln_mix_pre[i])
    z = h @ w_in[i]
    q = z[..., :ATTN_WIDTH].reshape(B, S, N_DIFF_HEADS, 2, DIFF_HEAD_DIM)
    k = z[..., ATTN_WIDTH:2 * ATTN_WIDTH].reshape(B, S, N_DIFF_HEADS, 2, DIFF_HEAD_DIM)
    v = z[..., 2 * ATTN_WIDTH:3 * ATTN_WIDTH].reshape(B, S, N_DIFF_HEADS, V_HEAD_DIM)
    p = z[..., 3 * ATTN_WIDTH:]
    lam_init = 0.8 - 0.6 * math.exp(-0.3 * i)
    lq = lam_q[i].astype(jnp.float32)
    lk = lam_k[i].astype(jnp.float32)
    lam = jnp.exp(jnp.sum(lq[0] * lk[0])) - jnp.exp(jnp.sum(lq[1] * lk[1])) + lam_init
    a = diff_attention(q, k, v, rel_bias, lam)
    a = (rms_norm(a, head_norm[i]) * (1.0 - lam_init)).reshape(B, S, ATTN_WIDTH)
    m = pool_mixer(p, w_pool[i], pool_scale[i])
    o = jnp.concatenate([a, m], axis=-1) @ w_out[i]
    x = x + rms_norm(o, ln_mix_post[i])
    u = rms_norm(x, ln_ffn_pre[i]) @ w_up[i]
    up = jnp.pad(u, ((0, 0), (1, 1), (0, 0)))
    cw = conv_w[i]
    u = up[:, :-2] * cw[0] + up[:, 1:-1] * cw[1] + up[:, 2:] * cw[2] + conv_b[i]
    gate, val = jnp.split(u, 2, axis=-1)
    f = (jax.nn.gelu(gate, approximate=True) * val) @ w_down[i]
    return x + rms_norm(f, ln_ffn_post[i])


def setup_inputs(seed: int = 0) -> dict:
    key = jax.random.key(seed)
    ks = jax.random.split(key, 20)
    f32 = jnp.float32
    n = lambda k, s, sc: jax.random.normal(k, s, f32) * sc
    return {
        "x_prompt": n(ks[0], (BATCH, SEQ, D_MODEL), 1.0),
        "x_sample": n(ks[1], (DEC_BATCH, DEC_SEQ, D_MODEL), 1.0),
        "rel_bias": n(ks[2], (NUM_BUCKETS, N_DIFF_HEADS), 0.5),
        "ln_mix_pre": 1.0 + n(ks[3], (DEPTH, D_MODEL), 0.05),
        "ln_mix_post": 1.0 + n(ks[4], (DEPTH, D_MODEL), 0.05),
        "w_in": n(ks[5], (DEPTH, D_MODEL, IN_WIDTH), D_MODEL ** -0.5),
        "lam_q": n(ks[6], (DEPTH, 2, DIFF_HEAD_DIM), 0.1),
        "lam_k": n(ks[7], (DEPTH, 2, DIFF_HEAD_DIM), 0.1),
        "head_norm": 1.0 + n(ks[8], (DEPTH, V_HEAD_DIM), 0.05),
        "w_pool": n(ks[9], (DEPTH, len(POOL_WINDOWS), POOL_GROUP, POOL_GROUP), POOL_GROUP ** -0.5),
        "pool_scale": 1.0 + n(ks[10], (DEPTH, POOL_WIDTH), 0.1),
        "w_out": n(ks[11], (DEPTH, D_MODEL, D_MODEL), D_MODEL ** -0.5),
        "ln_ffn_pre": 1.0 + n(ks[12], (DEPTH, D_MODEL), 0.05),
        "ln_ffn_post": 1.0 + n(ks[13], (DEPTH, D_MODEL), 0.05),
        "w_up": n(ks[14], (DEPTH, D_MODEL, 2 * D_FF), D_MODEL ** -0.5),
        "conv_w": n(ks[15], (DEPTH, 3, 2 * D_FF), 3 ** -0.5),
        "conv_b": n(ks[16], (DEPTH, 2 * D_FF), 0.01),
        "w_down": n(ks[17], (DEPTH, D_FF, D_MODEL), D_FF ** -0.5),
    }


def reference(x_prompt, x_sample, rel_bias, ln_mix_pre, ln_mix_post, w_in, lam_q, lam_k,
              head_norm, w_pool, pool_scale, w_out, ln_ffn_pre, ln_ffn_post, w_up,
              conv_w, conv_b, w_down):
    def trunk(x):
        for i in range(DEPTH):
            x = encoder_layer(x, i, rel_bias, ln_mix_pre, ln_mix_post, w_in, lam_q, lam_k,
                              head_norm, w_pool, pool_scale, w_out, ln_ffn_pre, ln_ffn_post,
                              w_up, conv_w, conv_b, w_down)
        return x
    y_prompt = trunk(x_prompt)
    y_sample = trunk(x_sample)
    return (y_prompt, y_sample)
```

```python
import functools
import math

import jax
import jax.numpy as jnp
from jax import lax
from jax.experimental import pallas as pl
from jax.experimental.pallas import tpu as pltpu

D_MODEL = 1024
ATTN_WIDTH = 512
POOL_WIDTH = D_MODEL - ATTN_WIDTH
N_DIFF_HEADS = 4
DIFF_HEAD_DIM = 64
V_HEAD_DIM = 2 * DIFF_HEAD_DIM
POOL_WINDOWS = (2, 4, 8, 16)
POOL_GROUP = POOL_WIDTH // len(POOL_WINDOWS)
D_FF = 2816
NUM_BUCKETS = 32
MAX_DISTANCE = 128
RMS_EPS = 1e-6
QK_SCALE = DIFF_HEAD_DIM ** -0.5

LANES = 128
SUBLANES = 8
ATTN_TILE = 512
ROW_TILE = 512
FF_CHUNK = D_FF // 2
HALO = SUBLANES
MASK_VALUE = -1e30
VMEM_LIMIT_BYTES = 48 * 1024 * 1024

F32 = jnp.float32
BF16 = jnp.bfloat16


def _rms(x, g):
    var = jnp.mean(x * x, axis=-1, keepdims=True)
    return x * lax.rsqrt(var + RMS_EPS) * g


def _params(*semantics):
    return pltpu.CompilerParams(dimension_semantics=semantics, vmem_limit_bytes=VMEM_LIMIT_BYTES)


def _inproj_kernel(x_ref, g_ref, w_ref, q_ref, k_ref, v_ref, p_ref):
    h = _rms(x_ref[...], g_ref[...]).astype(BF16)
    a = ATTN_WIDTH
    q = jnp.dot(h, w_ref[:, 0:a], preferred_element_type=F32)
    q_ref[...] = (q * QK_SCALE).astype(BF16)
    k_ref[...] = jnp.dot(h, w_ref[:, a:2 * a], preferred_element_type=F32).astype(BF16)
    v_ref[...] = jnp.dot(h, w_ref[:, 2 * a:3 * a], preferred_element_type=F32).astype(BF16)
    p_ref[...] = jnp.dot(h, w_ref[:, 3 * a:], preferred_element_type=F32)


def _inproj(x, g, w_in, tm):
    t = x.shape[0]
    row = lambda i: (i, 0)
    const = lambda i: (0, 0)
    return pl.pallas_call(
        _inproj_kernel,
        grid=(t // tm,),
        in_specs=[
            pl.BlockSpec((tm, D_MODEL), row),
            pl.BlockSpec((1, D_MODEL), const),
            pl.BlockSpec(w_in.shape, const),
        ],
        out_specs=[
            pl.BlockSpec((tm, ATTN_WIDTH), row),
            pl.BlockSpec((tm, ATTN_WIDTH), row),
            pl.BlockSpec((tm, ATTN_WIDTH), row),
            pl.BlockSpec((tm, POOL_WIDTH), row),
        ],
        out_shape=[
            jax.ShapeDtypeStruct((t, ATTN_WIDTH), BF16),
            jax.ShapeDtypeStruct((t, ATTN_WIDTH), BF16),
            jax.ShapeDtypeStruct((t, ATTN_WIDTH), BF16),
            jax.ShapeDtypeStruct((t, POOL_WIDTH), F32),
        ],
        compiler_params=_params("parallel"),
        name="inproj",
    )(x, g, w_in)


def _attn_kernel(far_ref, q_ref, k_ref, v_ref, bias_ref, lq_ref, lk_ref, hn_ref, o_ref,
                 qs_scr, m_scr, l_scr, acc_scr, *, tile, nk, lam_init):
    h = pl.program_id(1)
    i = pl.program_id(2)
    t = tile

    q = q_ref[0]
    lane = lax.broadcasted_iota(jnp.int32, (t, LANES), 1)
    zero = jnp.zeros_like(q)
    qs_scr[0:t, :] = jnp.where(lane < DIFF_HEAD_DIM, q, zero)
    qs_scr[t:2 * t, :] = jnp.where(lane >= DIFF_HEAD_DIM, q, zero)
    m_scr[...] = jnp.full(m_scr.shape, MASK_VALUE, F32)
    l_scr[...] = jnp.zeros(l_scr.shape, F32)
    acc_scr[...] = jnp.zeros(acc_scr.shape, F32)

    def step(j, bias, c):
        start = pl.multiple_of(j * t, t)
        kj = k_ref[0, pl.ds(start, t), :]
        vj = v_ref[0, pl.ds(start, t), :]
        s = lax.dot_general(qs_scr[...], kj, (((1,), (1,)), ((), ())),
                            preferred_element_type=F32)
        if bias is not None:
            s = s + jnp.concatenate([bias, bias], axis=0)
        m_prev = m_scr[...]
        m_cur = jnp.max(s, axis=1, keepdims=True)
        if c is not None:
            m_cur = m_cur + c
        m_next = jnp.maximum(m_prev, m_cur)
        shift = m_next if c is None else m_next - c
        p = jnp.exp(s - jnp.concatenate([shift] * (t // LANES), axis=1))
        alpha = jnp.exp(m_prev - m_next)
        l_scr[...] = alpha * l_scr[...] + jnp.sum(p, axis=1, keepdims=True)
        acc_scr[...] = alpha * acc_scr[...] + jnp.dot(p.astype(BF16), vj, preferred_element_type=F32)
        m_scr[...] = m_next

    c_left = far_ref[h, 0]
    c_right = far_ref[h, 1]

    def left_body(j, carry):
        step(j, None, c_left)
        return carry

    def right_body(j, carry):
        step(j, None, c_right)
        return carry

    lax.fori_loop(0, jnp.maximum(i - 1, 0), left_body, 0)
    for d in (-1, 0, 1):
        j = i + d

        @pl.when(jnp.logical_and(j >= 0, j < nk))
        def _():
            step(j, bias_ref[0, d + 1], None)

    lax.fori_loop(i + 2, nk, right_body, 0)

    o = acc_scr[...] / l_scr[...]
    prod = lq_ref[...] * lk_ref[...]
    lam = (jnp.exp(jnp.sum(prod[0:1], axis=1, keepdims=True))
           - jnp.exp(jnp.sum(prod[1:2], axis=1, keepdims=True)) + lam_init)
    a = o[0:t] - lam * o[t:2 * t]
    o_ref[0] = (_rms(a, hn_ref[...]) * (1.0 - lam_init)).astype(o_ref.dtype)


def _attention(q, k, v, bias_near, far, lam_q, lam_k, head_norm, lam_init, tile):
    b, s, _ = q.shape
    nk = s // tile
    kern = functools.partial(_attn_kernel, tile=tile, nk=nk, lam_init=lam_init)
    return pl.pallas_call(
        kern,
        grid=(b, N_DIFF_HEADS, nk),
        in_specs=[
            pl.BlockSpec(memory_space=pltpu.SMEM),
            pl.BlockSpec((1, tile, V_HEAD_DIM), lambda bi, h, i: (bi, i, h)),
            pl.BlockSpec((1, s, V_HEAD_DIM), lambda bi, h, i: (bi, 0, h)),
            pl.BlockSpec((1, s, V_HEAD_DIM), lambda bi, h, i: (bi, 0, h)),
            pl.BlockSpec((1, 3, tile, tile), lambda bi, h, i: (h, 0, 0, 0)),
            pl.BlockSpec((2, DIFF_HEAD_DIM), lambda bi, h, i: (0, 0)),
            pl.BlockSpec((2, DIFF_HEAD_DIM), lambda bi, h, i: (0, 0)),
            pl.BlockSpec((1, V_HEAD_DIM), lambda bi, h, i: (0, 0)),
        ],
        out_specs=pl.BlockSpec((1, tile, V_HEAD_DIM), lambda bi, h, i: (bi, i, h)),
        out_shape=jax.ShapeDtypeStruct((b, s, ATTN_WIDTH), BF16),
        scratch_shapes=[
            pltpu.VMEM((2 * tile, V_HEAD_DIM), BF16),
            pltpu.VMEM((2 * tile, LANES), F32),
            pltpu.VMEM((2 * tile, LANES), F32),
            pltpu.VMEM((2 * tile, V_HEAD_DIM), F32),
        ],
        compiler_params=_params("parallel", "parallel", "arbitrary"),
        name="diff_attention",
    )(far, q, k, v, bias_near, lam_q, lam_k, head_norm)


def _mixout_kernel(a_ref, p_ref, pprev_ref, pnext_ref, x_ref, wout_ref, wpool_ref, ps_ref, g_ref,
                   o_ref, pbuf, *, tm, tiles_per_seq, seq_len):
    ti = lax.rem(pl.program_id(0), tiles_per_seq)
    keep_prev = (ti > 0).astype(F32)
    keep_next = (ti < tiles_per_seq - 1).astype(F32)
    pbuf[0:HALO, :] = pprev_ref[...] * keep_prev
    pbuf[HALO:HALO + tm, :] = p_ref[...]
    pbuf[HALO + tm:, :] = pnext_ref[...] * keep_next

    pos = ti * tm + lax.broadcasted_iota(jnp.int32, (tm, 1), 0)
    mixed = []
    for g, win in enumerate(POOL_WINDOWS):
        lo_off = -(win // 2)
        hi_off = win - win // 2 - 1
        cols = slice(g * POOL_GROUP, (g + 1) * POOL_GROUP)
        total = pbuf[HALO + lo_off:HALO + lo_off + tm, cols]
        for off in range(lo_off + 1, hi_off + 1):
            total = total + pbuf[HALO + off:HALO + off + tm, cols]
        lo = jnp.maximum(pos + lo_off, 0)
        hi = jnp.minimum(pos + hi_off, seq_len - 1)
        cnt = (hi - lo + 1).astype(F32)
        y = total / cnt - pbuf[HALO:HALO + tm, cols]
        mixed.append(jnp.dot(y.astype(BF16), wpool_ref[g], preferred_element_type=F32))
    m = jnp.concatenate(mixed, axis=1) * ps_ref[...]

    o = jnp.dot(a_ref[...], wout_ref[0:ATTN_WIDTH, :], preferred_element_type=F32)
    o = o + jnp.dot(m.astype(BF16), wout_ref[ATTN_WIDTH:, :], preferred_element_type=F32)
    o_ref[...] = x_ref[...] + _rms(o, g_ref[...])


def _halo_specs(tm, width, n_rows):
    per_tile = tm // HALO
    last = n_rows // HALO - 1
    prev = pl.BlockSpec((HALO, width), lambda i, *_: (jnp.maximum(i * per_tile - 1, 0), 0))
    nxt = pl.BlockSpec((HALO, width), lambda i, *_: (jnp.minimum((i + 1) * per_tile, last), 0))
    return prev, nxt


def _mixout(a, p, x, w_out, w_pool, pool_scale, g, seq_len, tm):
    t = x.shape[0]
    row = lambda i: (i, 0)
    const = lambda i: (0, 0)
    prev, nxt = _halo_specs(tm, POOL_WIDTH, t)
    kern = functools.partial(_mixout_kernel, tm=tm, tiles_per_seq=seq_len // tm, seq_len=seq_len)
    return pl.pallas_call(
        kern,
        grid=(t // tm,),
        in_specs=[
            pl.BlockSpec((tm, ATTN_WIDTH), row),
            pl.BlockSpec((tm, POOL_WIDTH), row),
            prev,
            nxt,
            pl.BlockSpec((tm, D_MODEL), row),
            pl.BlockSpec(w_out.shape, const),
            pl.BlockSpec(w_pool.shape, lambda i: (0, 0, 0)),
            pl.BlockSpec((1, POOL_WIDTH), const),
            pl.BlockSpec((1, D_MODEL), const),
        ],
        out_specs=pl.BlockSpec((tm, D_MODEL), row),
        out_shape=jax.ShapeDtypeStruct((t, D_MODEL), F32),
        scratch_shapes=[pltpu.VMEM((tm + 2 * HALO, POOL_WIDTH), F32)],
        compiler_params=_params("parallel"),
        name="pool_outproj",
    )(a, p, p, p, x, w_out, w_pool, pool_scale, g)


def _gelu_tanh(x):
    return 0.5 * x * (1.0 + jnp.tanh(math.sqrt(2.0 / math.pi) * (x + 0.044715 * (x * x * x))))


def _ffn_kernel(x_ref, xprev_ref, xnext_ref, gpre_ref, wg_ref, wv_ref, cwg_ref, cwv_ref, cbg_ref, cbv_ref,
                wd_ref, gpost_ref, o_ref, h_scr, u_scr, acc_scr, *, tm, tiles_per_seq):
    c = pl.program_id(1)

    @pl.when(c == 0)
    def _():
        ti = lax.rem(pl.program_id(0), tiles_per_seq)
        keep_prev = (ti > 0).astype(F32)
        keep_next = (ti < tiles_per_seq - 1).astype(F32)
        g = gpre_ref[...]
        h_scr[0:HALO, :] = _rms(xprev_ref[...], g) * keep_prev
        h_scr[HALO:HALO + tm, :] = _rms(x_ref[...], g)
        h_scr[HALO + tm:, :] = _rms(xnext_ref[...], g) * keep_next
        acc_scr[...] = jnp.zeros(acc_scr.shape, F32)

    h = h_scr[...].astype(BF16)

    def conv(w_ref, cw_ref, cb_ref):
        u_scr[...] = jnp.dot(h, w_ref[...], preferred_element_type=F32)
        cw = cw_ref[...]
        return (u_scr[HALO - 1:HALO - 1 + tm, :] * cw[0:1]
                + u_scr[HALO:HALO + tm, :] * cw[1:2]
                + u_scr[HALO + 1:HALO + 1 + tm, :] * cw[2:3]
                + cb_ref[...])

    gate = conv(wg_ref, cwg_ref, cbg_ref)
    val = conv(wv_ref, cwv_ref, cbv_ref)
    f = (_gelu_tanh(gate) * val).astype(BF16)
    acc_scr[...] += jnp.dot(f, wd_ref[...], preferred_element_type=F32)

    @pl.when(c == pl.num_programs(1) - 1)
    def _():
        o_ref[...] = x_ref[...] + _rms(acc_scr[...], gpost_ref[...])


def _ffn(x, g_pre, w_up, conv_w, conv_b, w_down, g_post, seq_len, tm):
    t = x.shape[0]
    n_chunks = D_FF // FF_CHUNK
    row = lambda i, c: (i, 0)
    const = lambda i, c: (0, 0)
    gate_cols = lambda i, c: (0, c)
    val_cols = lambda i, c: (0, n_chunks + c)
    prev, nxt = _halo_specs(tm, D_MODEL, t)
    kern = functools.partial(_ffn_kernel, tm=tm, tiles_per_seq=seq_len // tm)
    return pl.pallas_call(
        kern,
        grid=(t // tm, n_chunks),
        in_specs=[
            pl.BlockSpec((tm, D_MODEL), row),
            prev,
            nxt,
            pl.BlockSpec((1, D_MODEL), const),
            pl.BlockSpec((D_MODEL, FF_CHUNK), gate_cols),
            pl.BlockSpec((D_MODEL, FF_CHUNK), val_cols),
            pl.BlockSpec((3, FF_CHUNK), gate_cols),
            pl.BlockSpec((3, FF_CHUNK), val_cols),
            pl.BlockSpec((1, FF_CHUNK), gate_cols),
            pl.BlockSpec((1, FF_CHUNK), val_cols),
            pl.BlockSpec((FF_CHUNK, D_MODEL), lambda i, c: (c, 0)),
            pl.BlockSpec((1, D_MODEL), const),
        ],
        out_specs=pl.BlockSpec((tm, D_MODEL), row),
        out_shape=jax.ShapeDtypeStruct((t, D_MODEL), F32),
        scratch_shapes=[
            pltpu.VMEM((tm + 2 * HALO, D_MODEL), F32),
            pltpu.VMEM((tm + 2 * HALO, FF_CHUNK), F32),
            pltpu.VMEM((tm, D_MODEL), F32),
        ],
        compiler_params=_params("parallel", "arbitrary"),
        name="conv_mlp",
    )(x, x, x, g_pre, w_up, w_up, conv_w, conv_w, conv_b, conv_b, w_down, g_post)


def _rel_bucket(rel):
    nb = NUM_BUCKETS // 2
    max_exact = nb // 2
    ret = jnp.where(rel > 0, nb, 0)
    n = jnp.abs(rel)
    nf = jnp.maximum(n, 1).astype(F32)
    large = max_exact + (jnp.log(nf / max_exact) / math.log(MAX_DISTANCE / max_exact)
                         * (nb - max_exact)).astype(jnp.int32)
    large = jnp.minimum(large, nb - 1)
    return ret + jnp.where(n < max_exact, n, large)


def _bias_tables(rel_bias, tile):
    assert tile >= MAX_DISTANCE, "far tiles must lie beyond the last distance bucket"
    pos = jnp.arange(tile, dtype=jnp.int32)
    offs = jnp.array([-tile, 0, tile], dtype=jnp.int32)
    rel = offs[:, None, None] + pos[None, None, :] - pos[None, :, None]
    near = jnp.transpose(rel_bias[_rel_bucket(rel)], (3, 0, 1, 2)).astype(F32)
    far = jnp.stack([rel_bias[NUM_BUCKETS // 2 - 1], rel_bias[NUM_BUCKETS - 1]], axis=1).astype(F32)
    return near, far


def kernel(x_prompt, x_sample, rel_bias, ln_mix_pre, ln_mix_post, w_in, lam_q, lam_k, head_norm, w_pool,
           pool_scale, w_out, ln_ffn_pre, ln_ffn_post, w_up, conv_w, conv_b, w_down):
    depth = w_in.shape[0]
    w_in_b, w_out_b, w_pool_b = w_in.astype(BF16), w_out.astype(BF16), w_pool.astype(BF16)
    w_up_b, w_down_b = w_up.astype(BF16), w_down.astype(BF16)
    tables = {}

    def trunk(x):
        b, s, d = x.shape
        tile = min(ATTN_TILE, s)
        tm = min(ROW_TILE, s)
        assert s % tile == 0 and s % tm == 0 and tm % HALO == 0
        if tile not in tables:
            tables[tile] = _bias_tables(rel_bias, tile)
        bias_near, far = tables[tile]
        xf = x.reshape(b * s, d)
        for i in range(depth):
            lam_init = 0.8 - 0.6 * math.exp(-0.3 * i)
            q, k, v, p = _inproj(xf, ln_mix_pre[i][None], w_in_b[i], tm)
            shape3 = (b, s, ATTN_WIDTH)
            a = _attention(q.reshape(shape3), k.reshape(shape3), v.reshape(shape3), bias_near, far,
                           lam_q[i], lam_k[i], head_norm[i][None], lam_init, tile)
            xf = _mixout(a.reshape(b * s, ATTN_WIDTH), p, xf, w_out_b[i], w_pool_b[i], pool_scale[i][None],
                         ln_mix_post[i][None], s, tm)
            xf = _ffn(xf, ln_ffn_pre[i][None], w_up_b[i], conv_w[i], conv_b[i][None], w_down_b[i],
                      ln_ffn_post[i][None], s, tm)
        return xf.reshape(b, s, d)

    return (trunk(x_prompt), trunk(x_sample))
```

```python
import functools
import math

import jax
import jax.numpy as jnp
from jax import lax
from jax.experimental import pallas as pl
from jax.experimental.pallas import tpu as pltpu

D_MODEL = 1024
ATTN_WIDTH = 512
POOL_WIDTH = D_MODEL - ATTN_WIDTH
N_DIFF_HEADS = 4
DIFF_HEAD_DIM = 64
V_HEAD_DIM = 2 * DIFF_HEAD_DIM
POOL_WINDOWS = (2, 4, 8, 16)
POOL_GROUP = POOL_WIDTH // len(POOL_WINDOWS)
D_FF = 2816
NUM_BUCKETS = 32
MAX_DISTANCE = 128
RMS_EPS = 1e-6
QK_SCALE = DIFF_HEAD_DIM ** -0.5
LOG2E = math.log2(math.e)

LANES = 128
SUBLANES = 8
MXU_WIDTH = 256
ATTN_TILE = 512
ROW_TILE = 512
FF_CHUNK = D_FF // 2
HALO = SUBLANES
ONES_ROWS = 16
VT_ROWS = V_HEAD_DIM + ONES_ROWS
NEAR_REACH = 2
MASK_VALUE = -1e30
VMEM_LIMIT_BYTES = 48 * 1024 * 1024

F32 = jnp.float32
BF16 = jnp.bfloat16
NT_DIMS = (((1,), (1,)), ((), ()))


def _rms(x, g):
    var = jnp.mean(x * x, axis=-1, keepdims=True)
    return x * lax.rsqrt(var + RMS_EPS) * g


def _params(*semantics):
    return pltpu.CompilerParams(dimension_semantics=semantics, vmem_limit_bytes=VMEM_LIMIT_BYTES)


def _inproj_kernel(x_ref, g_ref, w_ref, wvt_ref, q_ref, k_ref, vt_ref, p_ref):
    h = _rms(x_ref[...], g_ref[...]).astype(BF16)
    a = ATTN_WIDTH
    q = jnp.dot(h, w_ref[:, 0:a], preferred_element_type=F32)
    q_ref[...] = (q * (QK_SCALE * LOG2E)).astype(BF16)
    k_ref[...] = jnp.dot(h, w_ref[:, a:2 * a], preferred_element_type=F32).astype(BF16)
    p_ref[...] = jnp.dot(h, w_ref[:, 3 * a:], preferred_element_type=F32)
    vt = lax.dot_general(wvt_ref[...], h, NT_DIMS, preferred_element_type=F32).astype(BF16)
    ones = jnp.ones((ONES_ROWS, vt.shape[1]), BF16)
    for hd in range(N_DIFF_HEADS):
        r0 = hd * VT_ROWS
        vt_ref[0, r0:r0 + V_HEAD_DIM, :] = vt[hd * V_HEAD_DIM:(hd + 1) * V_HEAD_DIM]
        vt_ref[0, r0 + V_HEAD_DIM:r0 + VT_ROWS, :] = ones


def _inproj(x, g, w_in, w_vt, tm):
    t = x.shape[0]
    row = lambda i: (i, 0)
    const = lambda i: (0, 0)
    return pl.pallas_call(
        _inproj_kernel,
        grid=(t // tm,),
        in_specs=[
            pl.BlockSpec((tm, D_MODEL), row),
            pl.BlockSpec((1, D_MODEL), const),
            pl.BlockSpec(w_in.shape, const),
            pl.BlockSpec(w_vt.shape, const),
        ],
        out_specs=[
            pl.BlockSpec((tm, ATTN_WIDTH), row),
            pl.BlockSpec((tm, ATTN_WIDTH), row),
            pl.BlockSpec((1, N_DIFF_HEADS * VT_ROWS, tm), lambda i: (i, 0, 0)),
            pl.BlockSpec((tm, POOL_WIDTH), row),
        ],
        out_shape=[
            jax.ShapeDtypeStruct((t, ATTN_WIDTH), BF16),
            jax.ShapeDtypeStruct((t, ATTN_WIDTH), BF16),
            jax.ShapeDtypeStruct((t // tm, N_DIFF_HEADS * VT_ROWS, tm), BF16),
            jax.ShapeDtypeStruct((t, POOL_WIDTH), F32),
        ],
        compiler_params=_params("parallel"),
        name="inproj",
    )(x, g, w_in, w_vt)


def _attn_kernel(far_ref, q_ref, k_ref, vt_ref, bias_ref, lq_ref, lk_ref, hn_ref, o_ref,
                 qs_scr, sa_scr, sb_scr, m_scr, acc_scr, *, tile, nk, lam_init):
    h = pl.program_id(1)
    i = pl.program_id(2)
    t = tile
    cq = min(MXU_WIDTH, t)
    n_chunks = 2 * t // cq

    q = q_ref[...]
    lane = lax.broadcasted_iota(jnp.int32, (t, LANES), 1)
    zero = jnp.zeros_like(q)
    qs_scr[0:t, :] = jnp.where(lane < DIFF_HEAD_DIM, q, zero)
    qs_scr[t:2 * t, :] = jnp.where(lane >= DIFF_HEAD_DIM, q, zero)
    m_scr[...] = jnp.full(m_scr.shape, MASK_VALUE, F32)
    acc_scr[...] = jnp.zeros(acc_scr.shape, F32)

    def scores(j):
        start = pl.multiple_of(j * t, t)
        return lax.dot_general(k_ref[pl.ds(start, t), :], qs_scr[...], NT_DIMS,
                               preferred_element_type=F32)

    def softmax_pv(j, s_ref, near, c):
        vtj = vt_ref[j]
        for ci in range(n_chunks):
            cols = slice(ci * cq, (ci + 1) * cq)
            s = s_ref[:, cols]
            if near:
                q0 = (ci * cq) % t
                s = s + bias_ref[0, j - i + NEAR_REACH, :, q0:q0 + cq]
            m_prev = m_scr[:, cols]
            m_cur = jnp.max(s, axis=0, keepdims=True)
            if c is not None:
                m_cur = m_cur + c
            m_next = jnp.maximum(m_prev, m_cur)
            p = jnp.exp2(s - (m_next if c is None else m_next - c))
            alpha = jnp.exp2(m_prev - m_next)
            acc_scr[:, cols] = alpha * acc_scr[:, cols] + jnp.dot(vtj, p.astype(BF16),
                                                                  preferred_element_type=F32)
            m_scr[:, cols] = m_next

    def pair(m, near, c):
        j0 = 2 * m
        sb_scr[...] = scores(j0 + 1)
        softmax_pv(j0, sa_scr, near, c)
        sa_scr[...] = scores(jnp.minimum(j0 + 2, nk - 2))
        softmax_pv(j0 + 1, sb_scr, near, c)

    c_left = far_ref[h, 0]
    c_right = far_ref[h, 1]

    def left_body(m, carry):
        pair(m, False, c_left)
        return carry

    def right_body(m, carry):
        pair(m, False, c_right)
        return carry

    first_near = lax.div(i + 1, 2) - 1
    sa_scr[...] = scores(0)
    lax.fori_loop(0, jnp.maximum(first_near, 0), left_body, 0)
    for m in (first_near, first_near + 1):

        @pl.when(jnp.logical_and(m >= 0, m < nk // 2))
        def _():
            pair(m, True, None)

    lax.fori_loop(first_near + 2, nk // 2, right_body, 0)

    o = acc_scr[0:V_HEAD_DIM, :] / acc_scr[V_HEAD_DIM:V_HEAD_DIM + 1, :]
    prod = lq_ref[...] * lk_ref[...]
    lam = (jnp.exp(jnp.sum(prod[0:1], axis=1, keepdims=True))
           - jnp.exp(jnp.sum(prod[1:2], axis=1, keepdims=True)) + lam_init)
    a = (o[:, 0:t] - lam * o[:, t:2 * t]).T
    o_ref[...] = (_rms(a, hn_ref[...]) * (1.0 - lam_init)).astype(o_ref.dtype)


def _attention(q, k, vt, bias_near_t, far, lam_q, lam_k, head_norm, lam_init, tile, seq_len):
    t_all = q.shape[0]
    b = t_all // seq_len
    nk = seq_len // tile
    assert nk % 2 == 0, "key tiles are processed in pairs"
    kern = functools.partial(_attn_kernel, tile=tile, nk=nk, lam_init=lam_init)
    return pl.pallas_call(
        kern,
        grid=(b, N_DIFF_HEADS, nk),
        in_specs=[
            pl.BlockSpec(memory_space=pltpu.SMEM),
            pl.BlockSpec((tile, V_HEAD_DIM), lambda bi, h, i: (bi * nk + i, h)),
            pl.BlockSpec((seq_len, V_HEAD_DIM), lambda bi, h, i: (bi, h)),
            pl.BlockSpec((nk, VT_ROWS, tile), lambda bi, h, i: (bi, h, 0)),
            pl.BlockSpec((1, 2 * NEAR_REACH + 1, tile, tile), lambda bi, h, i: (h, 0, 0, 0)),
            pl.BlockSpec((2, DIFF_HEAD_DIM), lambda bi, h, i: (0, 0)),
            pl.BlockSpec((2, DIFF_HEAD_DIM), lambda bi, h, i: (0, 0)),
            pl.BlockSpec((1, V_HEAD_DIM), lambda bi, h, i: (0, 0)),
        ],
        out_specs=pl.BlockSpec((tile, V_HEAD_DIM), lambda bi, h, i: (bi * nk + i, h)),
        out_shape=jax.ShapeDtypeStruct((t_all, ATTN_WIDTH), BF16),
        scratch_shapes=[
            pltpu.VMEM((2 * tile, V_HEAD_DIM), BF16),
            pltpu.VMEM((tile, 2 * tile), F32),
            pltpu.VMEM((tile, 2 * tile), F32),
            pltpu.VMEM((1, 2 * tile), F32),
            pltpu.VMEM((VT_ROWS, 2 * tile), F32),
        ],
        compiler_params=_params("parallel", "parallel", "arbitrary"),
        name="diff_attention",
    )(far, q, k, vt, bias_near_t, lam_q, lam_k, head_norm)


def _mixout_kernel(a_ref, p_ref, pprev_ref, pnext_ref, x_ref, wout_ref, wpool_ref, ps_ref, g_ref,
                   o_ref, pbuf, *, tm, tiles_per_seq, seq_len):
    ti = lax.rem(pl.program_id(0), tiles_per_seq)
    keep_prev = (ti > 0).astype(F32)
    keep_next = (ti < tiles_per_seq - 1).astype(F32)
    pbuf[0:HALO, :] = pprev_ref[...] * keep_prev
    pbuf[HALO:HALO + tm, :] = p_ref[...]
    pbuf[HALO + tm:, :] = pnext_ref[...] * keep_next

    pos = ti * tm + lax.broadcasted_iota(jnp.int32, (tm, 1), 0)
    mixed = []
    for g, win in enumerate(POOL_WINDOWS):
        lo_off = -(win // 2)
        hi_off = win - win // 2 - 1
        cols = slice(g * POOL_GROUP, (g + 1) * POOL_GROUP)
        total = pbuf[HALO + lo_off:HALO + lo_off + tm, cols]
        for off in range(lo_off + 1, hi_off + 1):
            total = total + pbuf[HALO + off:HALO + off + tm, cols]
        lo = jnp.maximum(pos + lo_off, 0)
        hi = jnp.minimum(pos + hi_off, seq_len - 1)
        cnt = (hi - lo + 1).astype(F32)
        y = total / cnt - pbuf[HALO:HALO + tm, cols]
        mixed.append(jnp.dot(y.astype(BF16), wpool_ref[g], preferred_element_type=F32))
    m = jnp.concatenate(mixed, axis=1) * ps_ref[...]

    o = jnp.dot(a_ref[...], wout_ref[0:ATTN_WIDTH, :], preferred_element_type=F32)
    o = o + jnp.dot(m.astype(BF16), wout_ref[ATTN_WIDTH:, :], preferred_element_type=F32)
    o_ref[...] = x_ref[...] + _rms(o, g_ref[...])


def _halo_specs(tm, width, n_rows):
    per_tile = tm // HALO
    last = n_rows // HALO - 1
    prev = pl.BlockSpec((HALO, width), lambda i, *_: (jnp.maximum(i * per_tile - 1, 0), 0))
    nxt = pl.BlockSpec((HALO, width), lambda i, *_: (jnp.minimum((i + 1) * per_tile, last), 0))
    return prev, nxt


def _mixout(a, p, x, w_out, w_pool, pool_scale, g, seq_len, tm):
    t = x.shape[0]
    row = lambda i: (i, 0)
    const = lambda i: (0, 0)
    prev, nxt = _halo_specs(tm, POOL_WIDTH, t)
    kern = functools.partial(_mixout_kernel, tm=tm, tiles_per_seq=seq_len // tm, seq_len=seq_len)
    return pl.pallas_call(
        kern,
        grid=(t // tm,),
        in_specs=[
            pl.BlockSpec((tm, ATTN_WIDTH), row),
            pl.BlockSpec((tm, POOL_WIDTH), row),
            prev,
            nxt,
            pl.BlockSpec((tm, D_MODEL), row),
            pl.BlockSpec(w_out.shape, const),
            pl.BlockSpec(w_pool.shape, lambda i: (0, 0, 0)),
            pl.BlockSpec((1, POOL_WIDTH), const),
            pl.BlockSpec((1, D_MODEL), const),
        ],
        out_specs=pl.BlockSpec((tm, D_MODEL), row),
        out_shape=jax.ShapeDtypeStruct((t, D_MODEL), F32),
        scratch_shapes=[pltpu.VMEM((tm + 2 * HALO, POOL_WIDTH), F32)],
        compiler_params=_params("parallel"),
        name="pool_outproj",
    )(a, p, p, p, x, w_out, w_pool, pool_scale, g)


def _gelu_tanh(x):
    return 0.5 * x * (1.0 + jnp.tanh(math.sqrt(2.0 / math.pi) * (x + 0.044715 * (x * x * x))))


def _ffn_kernel(x_ref, xprev_ref, xnext_ref, gpre_ref, wg_ref, wv_ref, cwg_ref, cwv_ref, cbg_ref, cbv_ref,
                wd_ref, gpost_ref, o_ref, h_scr, u_scr, acc_scr, *, tm, tiles_per_seq):
    c = pl.program_id(1)

    @pl.when(c == 0)
    def _():
        ti = lax.rem(pl.program_id(0), tiles_per_seq)
        keep_prev = (ti > 0).astype(F32)
        keep_next = (ti < tiles_per_seq - 1).astype(F32)
        g = gpre_ref[...]
        h_scr[0:HALO, :] = _rms(xprev_ref[...], g) * keep_prev
        h_scr[HALO:HALO + tm, :] = _rms(x_ref[...], g)
        h_scr[HALO + tm:, :] = _rms(xnext_ref[...], g) * keep_next
        acc_scr[...] = jnp.zeros(acc_scr.shape, F32)

    h = h_scr[...].astype(BF16)

    def conv(w_ref, cw_ref, cb_ref):
        u_scr[...] = jnp.dot(h, w_ref[...], preferred_element_type=F32)
        cw = cw_ref[...]
        return (u_scr[HALO - 1:HALO - 1 + tm, :] * cw[0:1]
                + u_scr[HALO:HALO + tm, :] * cw[1:2]
                + u_scr[HALO + 1:HALO + 1 + tm, :] * cw[2:3]
                + cb_ref[...])

    gate = conv(wg_ref, cwg_ref, cbg_ref)
    val = conv(wv_ref, cwv_ref, cbv_ref)
    f = (_gelu_tanh(gate) * val).astype(BF16)
    acc_scr[...] += jnp.dot(f, wd_ref[...], preferred_element_type=F32)

    @pl.when(c == pl.num_programs(1) - 1)
    def _():
        o_ref[...] = x_ref[...] + _rms(acc_scr[...], gpost_ref[...])


def _ffn(x, g_pre, w_up, conv_w, conv_b, w_down, g_post, seq_len, tm):
    t = x.shape[0]
    n_chunks = D_FF // FF_CHUNK
    row = lambda i, c: (i, 0)
    const = lambda i, c: (0, 0)
    gate_cols = lambda i, c: (0, c)
    val_cols = lambda i, c: (0, n_chunks + c)
    prev, nxt = _halo_specs(tm, D_MODEL, t)
    kern = functools.partial(_ffn_kernel, tm=tm, tiles_per_seq=seq_len // tm)
    return pl.pallas_call(
        kern,
        grid=(t // tm, n_chunks),
        in_specs=[
            pl.BlockSpec((tm, D_MODEL), row),
            prev,
            nxt,
            pl.BlockSpec((1, D_MODEL), const),
            pl.BlockSpec((D_MODEL, FF_CHUNK), gate_cols),
            pl.BlockSpec((D_MODEL, FF_CHUNK), val_cols),
            pl.BlockSpec((3, FF_CHUNK), gate_cols),
            pl.BlockSpec((3, FF_CHUNK), val_cols),
            pl.BlockSpec((1, FF_CHUNK), gate_cols),
            pl.BlockSpec((1, FF_CHUNK), val_cols),
            pl.BlockSpec((FF_CHUNK, D_MODEL), lambda i, c: (c, 0)),
            pl.BlockSpec((1, D_MODEL), const),
        ],
        out_specs=pl.BlockSpec((tm, D_MODEL), row),
        out_shape=jax.ShapeDtypeStruct((t, D_MODEL), F32),
        scratch_shapes=[
            pltpu.VMEM((tm + 2 * HALO, D_MODEL), F32),
            pltpu.VMEM((tm + 2 * HALO, FF_CHUNK), F32),
            pltpu.VMEM((tm, D_MODEL), F32),
        ],
        compiler_params=_params("parallel", "arbitrary"),
        name="conv_mlp",
    )(x, x, x, g_pre, w_up, w_up, conv_w, conv_w, conv_b, conv_b, w_down, g_post)


def _rel_bucket(rel):
    nb = NUM_BUCKETS // 2
    max_exact = nb // 2
    ret = jnp.where(rel > 0, nb, 0)
    n = jnp.abs(rel)
    nf = jnp.maximum(n, 1).astype(F32)
    large = max_exact + (jnp.log(nf / max_exact) / math.log(MAX_DISTANCE / max_exact)
                         * (nb - max_exact)).astype(jnp.int32)
    large = jnp.minimum(large, nb - 1)
    return ret + jnp.where(n < max_exact, n, large)


def _bias_tables(rel_bias, tile):
    assert tile >= MAX_DISTANCE, "far tiles must lie beyond the last distance bucket"
    reach = (NEAR_REACH + 1) * tile
    span = 2 * reach
    rel = jnp.arange(-(reach - 1), reach, dtype=jnp.int32)
    rel_bias = rel_bias.astype(F32) * LOG2E
    by_rel = rel_bias[_rel_bucket(rel)].T
    periodic = jnp.tile(jnp.pad(by_rel, ((0, 0), (0, 1))), (1, tile))[:, :tile * (span - 1)]
    toeplitz = periodic.reshape(N_DIFF_HEADS, tile, span - 1)
    near = jnp.stack([toeplitz[:, :, reach + d * tile - 1:reach + (d + 1) * tile - 1]
                      for d in range(-NEAR_REACH, NEAR_REACH + 1)], axis=1)
    far = jnp.stack([rel_bias[NUM_BUCKETS // 2 - 1], rel_bias[NUM_BUCKETS - 1]], axis=1).astype(F32)
    return jnp.swapaxes(near, 2, 3), far


def kernel(x_prompt, x_sample, rel_bias, ln_mix_pre, ln_mix_post, w_in, lam_q, lam_k, head_norm, w_pool,
           pool_scale, w_out, ln_ffn_pre, ln_ffn_post, w_up, conv_w, conv_b, w_down):
    depth = w_in.shape[0]
    w_in_b, w_out_b, w_pool_b = w_in.astype(BF16), w_out.astype(BF16), w_pool.astype(BF16)
    w_up_b, w_down_b = w_up.astype(BF16), w_down.astype(BF16)
    w_vt_b = jnp.swapaxes(w_in_b[:, :, 2 * ATTN_WIDTH:3 * ATTN_WIDTH], 1, 2)
    tables = {}

    def trunk(x):
        b, s, d = x.shape
        tile = min(ATTN_TILE, s)
        tm = min(ROW_TILE, s)
        assert s % tile == 0 and s % tm == 0 and tm % HALO == 0
        if tile not in tables:
            tables[tile] = _bias_tables(rel_bias, tile)
        bias_near_t, far = tables[tile]
        xf = x.reshape(b * s, d)
        for i in range(depth):
            lam_init = 0.8 - 0.6 * math.exp(-0.3 * i)
            q, k, vt, p = _inproj(xf, ln_mix_pre[i][None], w_in_b[i], w_vt_b[i], tile)
            a = _attention(q, k, vt, bias_near_t, far, lam_q[i], lam_k[i], head_norm[i][None], lam_init,
                           tile, s)
            xf = _mixout(a, p, xf, w_out_b[i], w_pool_b[i], pool_scale[i][None], ln_mix_post[i][None], s, tm)
            xf = _ffn(xf, ln_ffn_pre[i][None], w_up_b[i], conv_w[i], conv_b[i][None], w_down_b[i],
                      ln_ffn_post[i][None], s, tm)
        return xf.reshape(b, s, d)

    return (trunk(x_prompt), trunk(x_sample))
```

```python
import functools
import math

import jax
import jax.numpy as jnp
from jax import lax
from jax.experimental import pallas as pl
from jax.experimental.pallas import tpu as pltpu

D_MODEL = 1024
ATTN_WIDTH = 512
POOL_WIDTH = D_MODEL - ATTN_WIDTH
N_DIFF_HEADS = 4
DIFF_HEAD_DIM = 64
V_HEAD_DIM = 2 * DIFF_HEAD_DIM
POOL_WINDOWS = (2, 4, 8, 16)
POOL_GROUP = POOL_WIDTH // len(POOL_WINDOWS)
D_FF = 2816
NUM_BUCKETS = 32
MAX_DISTANCE = 128
RMS_EPS = 1e-6
QK_SCALE = DIFF_HEAD_DIM ** -0.5
LOG2E = math.log2(math.e)

LANES = 128
SUBLANES = 8
MXU_WIDTH = 256
ATTN_TILE = 512
ROW_TILE = 512
FF_CHUNK = D_FF // 2
HALO = SUBLANES
ONES_ROWS = 16
VT_ROWS = V_HEAD_DIM + ONES_ROWS
NEAR_REACH = 2
MASK_VALUE = -1e30
VMEM_LIMIT_BYTES = 48 * 1024 * 1024

F32 = jnp.float32
BF16 = jnp.bfloat16
NT_DIMS = (((1,), (1,)), ((), ()))


def _rms(x, g):
    var = jnp.mean(x * x, axis=-1, keepdims=True)
    return x * lax.rsqrt(var + RMS_EPS) * g


def _params(*semantics):
    return pltpu.CompilerParams(dimension_semantics=semantics, vmem_limit_bytes=VMEM_LIMIT_BYTES)


def _inproj_kernel(x_ref, g_ref, w_ref, wvt_ref, q_ref, k_ref, vt_ref, p_ref):
    h = _rms(x_ref[...], g_ref[...]).astype(BF16)
    a = ATTN_WIDTH
    q = jnp.dot(h, w_ref[:, 0:a], preferred_element_type=F32)
    q_ref[...] = (q * (QK_SCALE * LOG2E)).astype(BF16)
    k_ref[...] = jnp.dot(h, w_ref[:, a:2 * a], preferred_element_type=F32).astype(BF16)
    p_ref[...] = jnp.dot(h, w_ref[:, 3 * a:], preferred_element_type=F32)
    vt = lax.dot_general(wvt_ref[...], h, NT_DIMS, preferred_element_type=F32).astype(BF16)
    ones = jnp.ones((ONES_ROWS, vt.shape[1]), BF16)
    for hd in range(N_DIFF_HEADS):
        r0 = hd * VT_ROWS
        vt_ref[0, r0:r0 + V_HEAD_DIM, :] = vt[hd * V_HEAD_DIM:(hd + 1) * V_HEAD_DIM]
        vt_ref[0, r0 + V_HEAD_DIM:r0 + VT_ROWS, :] = ones


def _inproj(x, g, w_in, w_vt, tm):
    t = x.shape[0]
    row = lambda i: (i, 0)
    const = lambda i: (0, 0)
    return pl.pallas_call(
        _inproj_kernel,
        grid=(t // tm,),
        in_specs=[
            pl.BlockSpec((tm, D_MODEL), row),
            pl.BlockSpec((1, D_MODEL), const),
            pl.BlockSpec(w_in.shape, const),
            pl.BlockSpec(w_vt.shape, const),
        ],
        out_specs=[
            pl.BlockSpec((tm, ATTN_WIDTH), row),
            pl.BlockSpec((tm, ATTN_WIDTH), row),
            pl.BlockSpec((1, N_DIFF_HEADS * VT_ROWS, tm), lambda i: (i, 0, 0)),
            pl.BlockSpec((tm, POOL_WIDTH), row),
        ],
        out_shape=[
            jax.ShapeDtypeStruct((t, ATTN_WIDTH), BF16),
            jax.ShapeDtypeStruct((t, ATTN_WIDTH), BF16),
            jax.ShapeDtypeStruct((t // tm, N_DIFF_HEADS * VT_ROWS, tm), BF16),
            jax.ShapeDtypeStruct((t, POOL_WIDTH), F32),
        ],
        compiler_params=_params("parallel"),
        name="inproj",
    )(x, g, w_in, w_vt)


def _attn_kernel(far_ref, q_ref, k_ref, vt_ref, bias_ref, lq_ref, lk_ref, hn_ref, o_ref,
                 qs_scr, sa_scr, sb_scr, pa_scr, pb_scr, m_scr, alpha_scr, acc_scr, *, tile, nk, lam_init):
    h = pl.program_id(1)
    i = pl.program_id(2)
    t = tile
    cq = min(MXU_WIDTH, t)
    n_chunks = 2 * t // cq

    q = q_ref[...]
    lane = lax.broadcasted_iota(jnp.int32, (t, LANES), 1)
    zero = jnp.zeros_like(q)
    qs_scr[0:t, :] = jnp.where(lane < DIFF_HEAD_DIM, q, zero)
    qs_scr[t:2 * t, :] = jnp.where(lane >= DIFF_HEAD_DIM, q, zero)
    m_scr[...] = jnp.full(m_scr.shape, MASK_VALUE, F32)
    acc_scr[...] = jnp.zeros(acc_scr.shape, F32)

    def scores(j, cols):
        start = pl.multiple_of(j * t, t)
        return lax.dot_general(k_ref[pl.ds(start, t), :], qs_scr[cols, :], NT_DIMS,
                               preferred_element_type=F32)

    def tile_step(j, s_scr, p_scr, near, c):
        j_next = jnp.minimum(j + 2, nk - 1)
        for ci in range(n_chunks):
            cols = slice(ci * cq, (ci + 1) * cq)
            s = s_scr[:, cols]
            if near:
                q0 = (ci * cq) % t
                s = s + bias_ref[0, j - i + NEAR_REACH, :, q0:q0 + cq]
            m_prev = m_scr[:, cols]
            m_cur = jnp.max(s, axis=0, keepdims=True)
            if c is not None:
                m_cur = m_cur + c
            m_next = jnp.maximum(m_prev, m_cur)
            p_scr[:, cols] = jnp.exp2(s - (m_next if c is None else m_next - c)).astype(BF16)
            alpha_scr[:, cols] = jnp.exp2(m_prev - m_next)
            m_scr[:, cols] = m_next
            s_scr[:, cols] = scores(j_next, cols)
        pv = jnp.dot(vt_ref[j], p_scr[...], preferred_element_type=F32)
        acc_scr[...] = alpha_scr[...] * acc_scr[...] + pv

    def pair(m, near, c):
        j0 = 2 * m
        tile_step(j0, sa_scr, pa_scr, near, c)
        tile_step(j0 + 1, sb_scr, pb_scr, near, c)

    c_left = far_ref[h, 0]
    c_right = far_ref[h, 1]

    def left_body(m, carry):
        pair(m, False, c_left)
        return carry

    def right_body(m, carry):
        pair(m, False, c_right)
        return carry

    first_near = lax.div(i + 1, 2) - 1
    for ci in range(n_chunks):
        cols = slice(ci * cq, (ci + 1) * cq)
        sa_scr[:, cols] = scores(0, cols)
        sb_scr[:, cols] = scores(1, cols)
    lax.fori_loop(0, jnp.maximum(first_near, 0), left_body, 0)
    for m in (first_near, first_near + 1):

        @pl.when(jnp.logical_and(m >= 0, m < nk // 2))
        def _():
            pair(m, True, None)

    lax.fori_loop(first_near + 2, nk // 2, right_body, 0)

    o = acc_scr[0:V_HEAD_DIM, :] / acc_scr[V_HEAD_DIM:V_HEAD_DIM + 1, :]
    prod = lq_ref[...] * lk_ref[...]
    lam = (jnp.exp(jnp.sum(prod[0:1], axis=1, keepdims=True))
           - jnp.exp(jnp.sum(prod[1:2], axis=1, keepdims=True)) + lam_init)
    a = (o[:, 0:t] - lam * o[:, t:2 * t]).T
    o_ref[...] = (_rms(a, hn_ref[...]) * (1.0 - lam_init)).astype(o_ref.dtype)


def _attention(q, k, vt, bias_near_t, far, lam_q, lam_k, head_norm, lam_init, tile, seq_len):
    t_all = q.shape[0]
    b = t_all // seq_len
    nk = seq_len // tile
    assert nk % 2 == 0, "key tiles are processed in pairs"
    kern = functools.partial(_attn_kernel, tile=tile, nk=nk, lam_init=lam_init)
    return pl.pallas_call(
        kern,
        grid=(b, N_DIFF_HEADS, nk),
        in_specs=[
            pl.BlockSpec(memory_space=pltpu.SMEM),
            pl.BlockSpec((tile, V_HEAD_DIM), lambda bi, h, i: (bi * nk + i, h)),
            pl.BlockSpec((seq_len, V_HEAD_DIM), lambda bi, h, i: (bi, h)),
            pl.BlockSpec((nk, VT_ROWS, tile), lambda bi, h, i: (bi, h, 0)),
            pl.BlockSpec((1, 2 * NEAR_REACH + 1, tile, tile), lambda bi, h, i: (h, 0, 0, 0)),
            pl.BlockSpec((2, DIFF_HEAD_DIM), lambda bi, h, i: (0, 0)),
            pl.BlockSpec((2, DIFF_HEAD_DIM), lambda bi, h, i: (0, 0)),
            pl.BlockSpec((1, V_HEAD_DIM), lambda bi, h, i: (0, 0)),
        ],
        out_specs=pl.BlockSpec((tile, V_HEAD_DIM), lambda bi, h, i: (bi * nk + i, h)),
        out_shape=jax.ShapeDtypeStruct((t_all, ATTN_WIDTH), BF16),
        scratch_shapes=[
            pltpu.VMEM((2 * tile, V_HEAD_DIM), BF16),
            pltpu.VMEM((tile, 2 * tile), F32),
            pltpu.VMEM((tile, 2 * tile), F32),
            pltpu.VMEM((tile, 2 * tile), BF16),
            pltpu.VMEM((tile, 2 * tile), BF16),
            pltpu.VMEM((1, 2 * tile), F32),
            pltpu.VMEM((1, 2 * tile), F32),
            pltpu.VMEM((VT_ROWS, 2 * tile), F32),
        ],
        compiler_params=_params("parallel", "parallel", "arbitrary"),
        name="diff_attention",
    )(far, q, k, vt, bias_near_t, lam_q, lam_k, head_norm)


def _mixout_kernel(a_ref, p_ref, pprev_ref, pnext_ref, x_ref, wout_ref, wpool_ref, ps_ref, g_ref,
                   o_ref, pbuf, *, tm, tiles_per_seq, seq_len):
    ti = lax.rem(pl.program_id(0), tiles_per_seq)
    keep_prev = (ti > 0).astype(F32)
    keep_next = (ti < tiles_per_seq - 1).astype(F32)
    pbuf[0:HALO, :] = pprev_ref[...] * keep_prev
    pbuf[HALO:HALO + tm, :] = p_ref[...]
    pbuf[HALO + tm:, :] = pnext_ref[...] * keep_next

    pos = ti * tm + lax.broadcasted_iota(jnp.int32, (tm, 1), 0)
    mixed = []
    for g, win in enumerate(POOL_WINDOWS):
        lo_off = -(win // 2)
        hi_off = win - win // 2 - 1
        cols = slice(g * POOL_GROUP, (g + 1) * POOL_GROUP)
        total = pbuf[HALO + lo_off:HALO + lo_off + tm, cols]
        for off in range(lo_off + 1, hi_off + 1):
            total = total + pbuf[HALO + off:HALO + off + tm, cols]
        lo = jnp.maximum(pos + lo_off, 0)
        hi = jnp.minimum(pos + hi_off, seq_len - 1)
        cnt = (hi - lo + 1).astype(F32)
        y = total / cnt - pbuf[HALO:HALO + tm, cols]
        mixed.append(jnp.dot(y.astype(BF16), wpool_ref[g], preferred_element_type=F32))
    m = jnp.concatenate(mixed, axis=1) * ps_ref[...]

    o = jnp.dot(a_ref[...], wout_ref[0:ATTN_WIDTH, :], preferred_element_type=F32)
    o = o + jnp.dot(m.astype(BF16), wout_ref[ATTN_WIDTH:, :], preferred_element_type=F32)
    o_ref[...] = x_ref[...] + _rms(o, g_ref[...])


def _halo_specs(tm, width, n_rows):
    per_tile = tm // HALO
    last = n_rows // HALO - 1
    prev = pl.BlockSpec((HALO, width), lambda i, *_: (jnp.maximum(i * per_tile - 1, 0), 0))
    nxt = pl.BlockSpec((HALO, width), lambda i, *_: (jnp.minimum((i + 1) * per_tile, last), 0))
    return prev, nxt


def _mixout(a, p, x, w_out, w_pool, pool_scale, g, seq_len, tm):
    t = x.shape[0]
    row = lambda i: (i, 0)
    const = lambda i: (0, 0)
    prev, nxt = _halo_specs(tm, POOL_WIDTH, t)
    kern = functools.partial(_mixout_kernel, tm=tm, tiles_per_seq=seq_len // tm, seq_len=seq_len)
    return pl.pallas_call(
        kern,
        grid=(t // tm,),
        in_specs=[
            pl.BlockSpec((tm, ATTN_WIDTH), row),
            pl.BlockSpec((tm, POOL_WIDTH), row),
            prev,
            nxt,
            pl.BlockSpec((tm, D_MODEL), row),
            pl.BlockSpec(w_out.shape, const),
            pl.BlockSpec(w_pool.shape, lambda i: (0, 0, 0)),
            pl.BlockSpec((1, POOL_WIDTH), const),
            pl.BlockSpec((1, D_MODEL), const),
        ],
        out_specs=pl.BlockSpec((tm, D_MODEL), row),
        out_shape=jax.ShapeDtypeStruct((t, D_MODEL), F32),
        scratch_shapes=[pltpu.VMEM((tm + 2 * HALO, POOL_WIDTH), F32)],
        compiler_params=_params("parallel"),
        name="pool_outproj",
    )(a, p, p, p, x, w_out, w_pool, pool_scale, g)


def _gelu_tanh(x):
    return 0.5 * x * (1.0 + jnp.tanh(math.sqrt(2.0 / math.pi) * (x + 0.044715 * (x * x * x))))


def _ffn_kernel(x_ref, xprev_ref, xnext_ref, gpre_ref, wg_ref, wv_ref, cwg_ref, cwv_ref, cbg_ref, cbv_ref,
                wd_ref, gpost_ref, o_ref, h_scr, u_scr, acc_scr, *, tm, tiles_per_seq):
    c = pl.program_id(1)

    @pl.when(c == 0)
    def _():
        ti = lax.rem(pl.program_id(0), tiles_per_seq)
        keep_prev = (ti > 0).astype(F32)
        keep_next = (ti < tiles_per_seq - 1).astype(F32)
        g = gpre_ref[...]
        h_scr[0:HALO, :] = _rms(xprev_ref[...], g) * keep_prev
        h_scr[HALO:HALO + tm, :] = _rms(x_ref[...], g)
        h_scr[HALO + tm:, :] = _rms(xnext_ref[...], g) * keep_next
        acc_scr[...] = jnp.zeros(acc_scr.shape, F32)

    h = h_scr[...].astype(BF16)

    def conv(w_ref, cw_ref, cb_ref):
        u_scr[...] = jnp.dot(h, w_ref[...], preferred_element_type=F32)
        cw = cw_ref[...]
        return (u_scr[HALO - 1:HALO - 1 + tm, :] * cw[0:1]
                + u_scr[HALO:HALO + tm, :] * cw[1:2]
                + u_scr[HALO + 1:HALO + 1 + tm, :] * cw[2:3]
                + cb_ref[...])

    gate = conv(wg_ref, cwg_ref, cbg_ref)
    val = conv(wv_ref, cwv_ref, cbv_ref)
    f = (_gelu_tanh(gate) * val).astype(BF16)
    acc_scr[...] += jnp.dot(f, wd_ref[...], preferred_element_type=F32)

    @pl.when(c == pl.num_programs(1) - 1)
    def _():
        o_ref[...] = x_ref[...] + _rms(acc_scr[...], gpost_ref[...])


def _ffn(x, g_pre, w_up, conv_w, conv_b, w_down, g_post, seq_len, tm):
    t = x.shape[0]
    n_chunks = D_FF // FF_CHUNK
    row = lambda i, c: (i, 0)
    const = lambda i, c: (0, 0)
    gate_cols = lambda i, c: (0, c)
    val_cols = lambda i, c: (0, n_chunks + c)
    prev, nxt = _halo_specs(tm, D_MODEL, t)
    kern = functools.partial(_ffn_kernel, tm=tm, tiles_per_seq=seq_len // tm)
    return pl.pallas_call(
        kern,
        grid=(t // tm, n_chunks),
        in_specs=[
            pl.BlockSpec((tm, D_MODEL), row),
            prev,
            nxt,
            pl.BlockSpec((1, D_MODEL), const),
            pl.BlockSpec((D_MODEL, FF_CHUNK), gate_cols),
            pl.BlockSpec((D_MODEL, FF_CHUNK), val_cols),
            pl.BlockSpec((3, FF_CHUNK), gate_cols),
            pl.BlockSpec((3, FF_CHUNK), val_cols),
            pl.BlockSpec((1, FF_CHUNK), gate_cols),
            pl.BlockSpec((1, FF_CHUNK), val_cols),
            pl.BlockSpec((FF_CHUNK, D_MODEL), lambda i, c: (c, 0)),
            pl.BlockSpec((1, D_MODEL), const),
        ],
        out_specs=pl.BlockSpec((tm, D_MODEL), row),
        out_shape=jax.ShapeDtypeStruct((t, D_MODEL), F32),
        scratch_shapes=[
            pltpu.VMEM((tm + 2 * HALO, D_MODEL), F32),
            pltpu.VMEM((tm + 2 * HALO, FF_CHUNK), F32),
            pltpu.VMEM((tm, D_MODEL), F32),
        ],
        compiler_params=_params("parallel", "arbitrary"),
        name="conv_mlp",
    )(x, x, x, g_pre, w_up, w_up, conv_w, conv_w, conv_b, conv_b, w_down, g_post)


def _rel_bucket(rel):
    nb = NUM_BUCKETS // 2
    max_exact = nb // 2
    ret = jnp.where(rel > 0, nb, 0)
    n = jnp.abs(rel)
    nf = jnp.maximum(n, 1).astype(F32)
    large = max_exact + (jnp.log(nf / max_exact) / math.log(MAX_DISTANCE / max_exact)
                         * (nb - max_exact)).astype(jnp.int32)
    large = jnp.minimum(large, nb - 1)
    return ret + jnp.where(n < max_exact, n, large)


def _bias_tables(rel_bias, tile):
    assert tile >= MAX_DISTANCE, "far tiles must lie beyond the last distance bucket"
    reach = (NEAR_REACH + 1) * tile
    span = 2 * reach
    rel = jnp.arange(-(reach - 1), reach, dtype=jnp.int32)
    rel_bias = rel_bias.astype(F32) * LOG2E
    by_rel = rel_bias[_rel_bucket(rel)].T
    periodic = jnp.tile(jnp.pad(by_rel, ((0, 0), (0, 1))), (1, tile))[:, :tile * (span - 1)]
    toeplitz = periodic.reshape(N_DIFF_HEADS, tile, span - 1)
    near = jnp.stack([toeplitz[:, :, reach + d * tile - 1:reach + (d + 1) * tile - 1]
                      for d in range(-NEAR_REACH, NEAR_REACH + 1)], axis=1)
    far = jnp.stack([rel_bias[NUM_BUCKETS // 2 - 1], rel_bias[NUM_BUCKETS - 1]], axis=1).astype(F32)
    return jnp.swapaxes(near, 2, 3), far


def kernel(x_prompt, x_sample, rel_bias, ln_mix_pre, ln_mix_post, w_in, lam_q, lam_k, head_norm, w_pool,
           pool_scale, w_out, ln_ffn_pre, ln_ffn_post, w_up, conv_w, conv_b, w_down):
    depth = w_in.shape[0]
    w_in_b, w_out_b, w_pool_b = w_in.astype(BF16), w_out.astype(BF16), w_pool.astype(BF16)
    w_up_b, w_down_b = w_up.astype(BF16), w_down.astype(BF16)
    w_vt_b = jnp.swapaxes(w_in_b[:, :, 2 * ATTN_WIDTH:3 * ATTN_WIDTH], 1, 2)
    tables = {}

    def trunk(x):
        b, s, d = x.shape
        tile = min(ATTN_TILE, s)
        tm = min(ROW_TILE, s)
        assert s % tile == 0 and s % tm == 0 and tm % HALO == 0
        if tile not in tables:
            tables[tile] = _bias_tables(rel_bias, tile)
        bias_near_t, far = tables[tile]
        xf = x.reshape(b * s, d)
        for i in range(depth):
            lam_init = 0.8 - 0.6 * math.exp(-0.3 * i)
            q, k, vt, p = _inproj(xf, ln_mix_pre[i][None], w_in_b[i], w_vt_b[i], tile)
            a = _attention(q, k, vt, bias_near_t, far, lam_q[i], lam_k[i], head_norm[i][None], lam_init,
                           tile, s)
            xf = _mixout(a, p, xf, w_out_b[i], w_pool_b[i], pool_scale[i][None], ln_mix_post[i][None], s, tm)
            xf = _ffn(xf, ln_ffn_pre[i][None], w_up_b[i], conv_w[i], conv_b[i][None], w_down_b[i],
                      ln_ffn_post[i][None], s, tm)
        return xf.reshape(b, s, d)

    return (trunk(x_prompt), trunk(x_sample))
```

```python
import functools
import math

import jax
import jax.numpy as jnp
from jax import lax
from jax.experimental import pallas as pl
from jax.experimental.pallas import tpu as pltpu

D_MODEL = 1024
ATTN_WIDTH = 512
POOL_WIDTH = D_MODEL - ATTN_WIDTH
N_DIFF_HEADS = 4
DIFF_HEAD_DIM = 64
V_HEAD_DIM = 2 * DIFF_HEAD_DIM
POOL_WINDOWS = (2, 4, 8, 16)
POOL_GROUP = POOL_WIDTH // len(POOL_WINDOWS)
D_FF = 2816
NUM_BUCKETS = 32
MAX_DISTANCE = 128
RMS_EPS = 1e-6
QK_SCALE = DIFF_HEAD_DIM ** -0.5
LOG2E = math.log2(math.e)

LANES = 128
SUBLANES = 8
MXU_WIDTH = 256
ATTN_TILE = 512
ROW_TILE = 512
FF_CHUNK = D_FF // 2
HALO = SUBLANES
ONES_ROWS = 16
VT_ROWS = V_HEAD_DIM + ONES_ROWS
NEAR_REACH = 2
MASK_VALUE = -1e30
VMEM_LIMIT_BYTES = 48 * 1024 * 1024

F32 = jnp.float32
BF16 = jnp.bfloat16
NT_DIMS = (((1,), (1,)), ((), ()))


def _rms(x, g):
    var = jnp.mean(x * x, axis=-1, keepdims=True)
    return x * lax.rsqrt(var + RMS_EPS) * g


def _params(*semantics):
    return pltpu.CompilerParams(dimension_semantics=semantics, vmem_limit_bytes=VMEM_LIMIT_BYTES)


def _inproj_kernel(x_ref, g_ref, w_ref, wvt_ref, q_ref, k_ref, vt_ref, p_ref):
    h = _rms(x_ref[...], g_ref[...]).astype(BF16)
    a = ATTN_WIDTH
    q = jnp.dot(h, w_ref[:, 0:a], preferred_element_type=F32)
    q_ref[...] = (q * (QK_SCALE * LOG2E)).astype(BF16)
    k_ref[...] = jnp.dot(h, w_ref[:, a:2 * a], preferred_element_type=F32).astype(BF16)
    p_ref[...] = jnp.dot(h, w_ref[:, 3 * a:], preferred_element_type=F32)
    vt = lax.dot_general(wvt_ref[...], h, NT_DIMS, preferred_element_type=F32).astype(BF16)
    ones = jnp.ones((ONES_ROWS, vt.shape[1]), BF16)
    for hd in range(N_DIFF_HEADS):
        r0 = hd * VT_ROWS
        vt_ref[0, r0:r0 + V_HEAD_DIM, :] = vt[hd * V_HEAD_DIM:(hd + 1) * V_HEAD_DIM]
        vt_ref[0, r0 + V_HEAD_DIM:r0 + VT_ROWS, :] = ones


def _inproj(x, g, w_in, w_vt, tm):
    t = x.shape[0]
    row = lambda i: (i, 0)
    const = lambda i: (0, 0)
    return pl.pallas_call(
        _inproj_kernel,
        grid=(t // tm,),
        in_specs=[
            pl.BlockSpec((tm, D_MODEL), row),
            pl.BlockSpec((1, D_MODEL), const),
            pl.BlockSpec(w_in.shape, const),
            pl.BlockSpec(w_vt.shape, const),
        ],
        out_specs=[
            pl.BlockSpec((tm, ATTN_WIDTH), row),
            pl.BlockSpec((tm, ATTN_WIDTH), row),
            pl.BlockSpec((1, N_DIFF_HEADS * VT_ROWS, tm), lambda i: (i, 0, 0)),
            pl.BlockSpec((tm, POOL_WIDTH), row),
        ],
        out_shape=[
            jax.ShapeDtypeStruct((t, ATTN_WIDTH), BF16),
            jax.ShapeDtypeStruct((t, ATTN_WIDTH), BF16),
            jax.ShapeDtypeStruct((t // tm, N_DIFF_HEADS * VT_ROWS, tm), BF16),
            jax.ShapeDtypeStruct((t, POOL_WIDTH), F32),
        ],
        compiler_params=_params("parallel"),
        name="inproj",
    )(x, g, w_in, w_vt)


def _attn_kernel(far_ref, q_ref, k_ref, vt_ref, bias_ref, lq_ref, lk_ref, hn_ref, o_ref,
                 qs_scr, sa_scr, sb_scr, pa_scr, pb_scr, m_scr, alpha_scr, acc_scr, *, tile, nk, lam_init):
    h = pl.program_id(1)
    i = pl.program_id(2)
    t = tile
    cq = min(MXU_WIDTH, t)
    n_chunks = 2 * t // cq
    spc = cq // LANES
    n_slabs = 2 * t // LANES

    q = q_ref[...]
    lane = lax.broadcasted_iota(jnp.int32, (t, LANES), 1)
    zero = jnp.zeros_like(q)
    qs_scr[0:t, :] = jnp.where(lane < DIFF_HEAD_DIM, q, zero)
    qs_scr[t:2 * t, :] = jnp.where(lane >= DIFF_HEAD_DIM, q, zero)
    m_scr[...] = jnp.full(m_scr.shape, MASK_VALUE, F32)
    acc_scr[...] = jnp.zeros(acc_scr.shape, F32)

    def scores(j, cols):
        start = pl.multiple_of(j * t, t)
        return lax.dot_general(k_ref[pl.ds(start, t), :], qs_scr[cols, :], NT_DIMS,
                               preferred_element_type=F32)

    def load_chunk(ref, ci):
        return jnp.concatenate([ref[ci * spc + u] for u in range(spc)], axis=1)

    def store_chunk(ref, ci, val):
        for u in range(spc):
            ref[ci * spc + u] = val[:, u * LANES:(u + 1) * LANES]

    def tile_step(j, s_scr, p_scr, near, c):
        j_next = jnp.minimum(j + 2, nk - 1)
        for ci in range(n_chunks):
            cols = slice(ci * cq, (ci + 1) * cq)
            s = load_chunk(s_scr, ci)
            if near:
                q0 = (ci * cq) % t
                s = s + bias_ref[0, j - i + NEAR_REACH, :, q0:q0 + cq]
            m_prev = m_scr[:, cols]
            m_cur = jnp.max(s, axis=0, keepdims=True)
            if c is not None:
                m_cur = m_cur + c
            m_next = jnp.maximum(m_prev, m_cur)
            store_chunk(p_scr, ci, jnp.exp2(s - (m_next if c is None else m_next - c)).astype(BF16))
            alpha_scr[:, cols] = jnp.exp2(m_prev - m_next)
            m_scr[:, cols] = m_next
            store_chunk(s_scr, ci, scores(j_next, cols))
        p = jnp.concatenate([p_scr[u] for u in range(n_slabs)], axis=1)
        pv = jnp.dot(vt_ref[j], p, preferred_element_type=F32)
        for u in range(n_slabs):
            lanes = slice(u * LANES, (u + 1) * LANES)
            acc_scr[u] = alpha_scr[:, lanes] * acc_scr[u] + pv[:, lanes]

    def pair(m, near, c):
        j0 = 2 * m
        tile_step(j0, sa_scr, pa_scr, near, c)
        tile_step(j0 + 1, sb_scr, pb_scr, near, c)

    c_left = far_ref[h, 0]
    c_right = far_ref[h, 1]

    def left_body(m, carry):
        pair(m, False, c_left)
        return carry

    def right_body(m, carry):
        pair(m, False, c_right)
        return carry

    first_near = lax.div(i + 1, 2) - 1
    for ci in range(n_chunks):
        cols = slice(ci * cq, (ci + 1) * cq)
        store_chunk(sa_scr, ci, scores(0, cols))
        store_chunk(sb_scr, ci, scores(1, cols))
    lax.fori_loop(0, jnp.maximum(first_near, 0), left_body, 0)
    for m in (first_near, first_near + 1):

        @pl.when(jnp.logical_and(m >= 0, m < nk // 2))
        def _():
            pair(m, True, None)

    lax.fori_loop(first_near + 2, nk // 2, right_body, 0)

    acc = jnp.concatenate([acc_scr[u] for u in range(n_slabs)], axis=1)
    o = acc[0:V_HEAD_DIM, :] / acc[V_HEAD_DIM:V_HEAD_DIM + 1, :]
    prod = lq_ref[...] * lk_ref[...]
    lam = (jnp.exp(jnp.sum(prod[0:1], axis=1, keepdims=True))
           - jnp.exp(jnp.sum(prod[1:2], axis=1, keepdims=True)) + lam_init)
    a = (o[:, 0:t] - lam * o[:, t:2 * t]).T
    o_ref[...] = (_rms(a, hn_ref[...]) * (1.0 - lam_init)).astype(o_ref.dtype)


def _attention(q, k, vt, bias_near_t, far, lam_q, lam_k, head_norm, lam_init, tile, seq_len):
    t_all = q.shape[0]
    b = t_all // seq_len
    nk = seq_len // tile
    assert nk % 2 == 0, "key tiles are processed in pairs"
    slabs = 2 * tile // LANES
    kern = functools.partial(_attn_kernel, tile=tile, nk=nk, lam_init=lam_init)
    return pl.pallas_call(
        kern,
        grid=(b, N_DIFF_HEADS, nk),
        in_specs=[
            pl.BlockSpec(memory_space=pltpu.SMEM),
            pl.BlockSpec((tile, V_HEAD_DIM), lambda bi, h, i: (bi * nk + i, h)),
            pl.BlockSpec((seq_len, V_HEAD_DIM), lambda bi, h, i: (bi, h)),
            pl.BlockSpec((nk, VT_ROWS, tile), lambda bi, h, i: (bi, h, 0)),
            pl.BlockSpec((1, 2 * NEAR_REACH + 1, tile, tile), lambda bi, h, i: (h, 0, 0, 0)),
            pl.BlockSpec((2, DIFF_HEAD_DIM), lambda bi, h, i: (0, 0)),
            pl.BlockSpec((2, DIFF_HEAD_DIM), lambda bi, h, i: (0, 0)),
            pl.BlockSpec((1, V_HEAD_DIM), lambda bi, h, i: (0, 0)),
        ],
        out_specs=pl.BlockSpec((tile, V_HEAD_DIM), lambda bi, h, i: (bi * nk + i, h)),
        out_shape=jax.ShapeDtypeStruct((t_all, ATTN_WIDTH), BF16),
        scratch_shapes=[
            pltpu.VMEM((2 * tile, V_HEAD_DIM), BF16),
            pltpu.VMEM((slabs, tile, LANES), F32),
            pltpu.VMEM((slabs, tile, LANES), F32),
            pltpu.VMEM((slabs, tile, LANES), BF16),
            pltpu.VMEM((slabs, tile, LANES), BF16),
            pltpu.VMEM((1, 2 * tile), F32),
            pltpu.VMEM((1, 2 * tile), F32),
            pltpu.VMEM((slabs, VT_ROWS, LANES), F32),
        ],
        compiler_params=_params("parallel", "parallel", "arbitrary"),
        name="diff_attention",
    )(far, q, k, vt, bias_near_t, lam_q, lam_k, head_norm)


def _mixout_kernel(a_ref, p_ref, pprev_ref, pnext_ref, x_ref, wout_ref, wpool_ref, ps_ref, g_ref,
                   o_ref, pbuf, *, tm, tiles_per_seq, seq_len):
    ti = lax.rem(pl.program_id(0), tiles_per_seq)
    keep_prev = (ti > 0).astype(F32)
    keep_next = (ti < tiles_per_seq - 1).astype(F32)
    pbuf[0:HALO, :] = pprev_ref[...] * keep_prev
    pbuf[HALO:HALO + tm, :] = p_ref[...]
    pbuf[HALO + tm:, :] = pnext_ref[...] * keep_next

    pos = ti * tm + lax.broadcasted_iota(jnp.int32, (tm, 1), 0)
    mixed = []
    for g, win in enumerate(POOL_WINDOWS):
        lo_off = -(win // 2)
        hi_off = win - win // 2 - 1
        cols = slice(g * POOL_GROUP, (g + 1) * POOL_GROUP)
        total = pbuf[HALO + lo_off:HALO + lo_off + tm, cols]
        for off in range(lo_off + 1, hi_off + 1):
            total = total + pbuf[HALO + off:HALO + off + tm, cols]
        lo = jnp.maximum(pos + lo_off, 0)
        hi = jnp.minimum(pos + hi_off, seq_len - 1)
        cnt = (hi - lo + 1).astype(F32)
        y = total / cnt - pbuf[HALO:HALO + tm, cols]
        mixed.append(jnp.dot(y.astype(BF16), wpool_ref[g], preferred_element_type=F32))
    m = jnp.concatenate(mixed, axis=1) * ps_ref[...]

    o = jnp.dot(a_ref[...], wout_ref[0:ATTN_WIDTH, :], preferred_element_type=F32)
    o = o + jnp.dot(m.astype(BF16), wout_ref[ATTN_WIDTH:, :], preferred_element_type=F32)
    o_ref[...] = x_ref[...] + _rms(o, g_ref[...])


def _halo_specs(tm, width, n_rows):
    per_tile = tm // HALO
    last = n_rows // HALO - 1
    prev = pl.BlockSpec((HALO, width), lambda i, *_: (jnp.maximum(i * per_tile - 1, 0), 0))
    nxt = pl.BlockSpec((HALO, width), lambda i, *_: (jnp.minimum((i + 1) * per_tile, last), 0))
    return prev, nxt


def _mixout(a, p, x, w_out, w_pool, pool_scale, g, seq_len, tm):
    t = x.shape[0]
    row = lambda i: (i, 0)
    const = lambda i: (0, 0)
    prev, nxt = _halo_specs(tm, POOL_WIDTH, t)
    kern = functools.partial(_mixout_kernel, tm=tm, tiles_per_seq=seq_len // tm, seq_len=seq_len)
    return pl.pallas_call(
        kern,
        grid=(t // tm,),
        in_specs=[
            pl.BlockSpec((tm, ATTN_WIDTH), row),
            pl.BlockSpec((tm, POOL_WIDTH), row),
            prev,
            nxt,
            pl.BlockSpec((tm, D_MODEL), row),
            pl.BlockSpec(w_out.shape, const),
            pl.BlockSpec(w_pool.shape, lambda i: (0, 0, 0)),
            pl.BlockSpec((1, POOL_WIDTH), const),
            pl.BlockSpec((1, D_MODEL), const),
        ],
        out_specs=pl.BlockSpec((tm, D_MODEL), row),
        out_shape=jax.ShapeDtypeStruct((t, D_MODEL), F32),
        scratch_shapes=[pltpu.VMEM((tm + 2 * HALO, POOL_WIDTH), F32)],
        compiler_params=_params("parallel"),
        name="pool_outproj",
    )(a, p, p, p, x, w_out, w_pool, pool_scale, g)


def _gelu_tanh(x):
    return 0.5 * x * (1.0 + jnp.tanh(math.sqrt(2.0 / math.pi) * (x + 0.044715 * (x * x * x))))


def _ffn_kernel(x_ref, xprev_ref, xnext_ref, gpre_ref, wg_ref, wv_ref, cwg_ref, cwv_ref, cbg_ref, cbv_ref,
                wd_ref, gpost_ref, o_ref, h_scr, u_scr, acc_scr, *, tm, tiles_per_seq):
    c = pl.program_id(1)

    @pl.when(c == 0)
    def _():
        ti = lax.rem(pl.program_id(0), tiles_per_seq)
        keep_prev = (ti > 0).astype(F32)
        keep_next = (ti < tiles_per_seq - 1).astype(F32)
        g = gpre_ref[...]
        h_scr[0:HALO, :] = _rms(xprev_ref[...], g) * keep_prev
        h_scr[HALO:HALO + tm, :] = _rms(x_ref[...], g)
        h_scr[HALO + tm:, :] = _rms(xnext_ref[...], g) * keep_next
        acc_scr[...] = jnp.zeros(acc_scr.shape, F32)

    h = h_scr[...].astype(BF16)

    def conv(w_ref, cw_ref, cb_ref):
        u_scr[...] = jnp.dot(h, w_ref[...], preferred_element_type=F32)
        cw = cw_ref[...]
        return (u_scr[HALO - 1:HALO - 1 + tm, :] * cw[0:1]
                + u_scr[HALO:HALO + tm, :] * cw[1:2]
                + u_scr[HALO + 1:HALO + 1 + tm, :] * cw[2:3]
                + cb_ref[...])

    gate = conv(wg_ref, cwg_ref, cbg_ref)
    val = conv(wv_ref, cwv_ref, cbv_ref)
    f = (_gelu_tanh(gate) * val).astype(BF16)
    acc_scr[...] += jnp.dot(f, wd_ref[...], preferred_element_type=F32)

    @pl.when(c == pl.num_programs(1) - 1)
    def _():
        o_ref[...] = x_ref[...] + _rms(acc_scr[...], gpost_ref[...])


def _ffn(x, g_pre, w_up, conv_w, conv_b, w_down, g_post, seq_len, tm):
    t = x.shape[0]
    n_chunks = D_FF // FF_CHUNK
    row = lambda i, c: (i, 0)
    const = lambda i, c: (0, 0)
    gate_cols = lambda i, c: (0, c)
    val_cols = lambda i, c: (0, n_chunks + c)
    prev, nxt = _halo_specs(tm, D_MODEL, t)
    kern = functools.partial(_ffn_kernel, tm=tm, tiles_per_seq=seq_len // tm)
    return pl.pallas_call(
        kern,
        grid=(t // tm, n_chunks),
        in_specs=[
            pl.BlockSpec((tm, D_MODEL), row),
            prev,
            nxt,
            pl.BlockSpec((1, D_MODEL), const),
            pl.BlockSpec((D_MODEL, FF_CHUNK), gate_cols),
            pl.BlockSpec((D_MODEL, FF_CHUNK), val_cols),
            pl.BlockSpec((3, FF_CHUNK), gate_cols),
            pl.BlockSpec((3, FF_CHUNK), val_cols),
            pl.BlockSpec((1, FF_CHUNK), gate_cols),
            pl.BlockSpec((1, FF_CHUNK), val_cols),
            pl.BlockSpec((FF_CHUNK, D_MODEL), lambda i, c: (c, 0)),
            pl.BlockSpec((1, D_MODEL), const),
        ],
        out_specs=pl.BlockSpec((tm, D_MODEL), row),
        out_shape=jax.ShapeDtypeStruct((t, D_MODEL), F32),
        scratch_shapes=[
            pltpu.VMEM((tm + 2 * HALO, D_MODEL), F32),
            pltpu.VMEM((tm + 2 * HALO, FF_CHUNK), F32),
            pltpu.VMEM((tm, D_MODEL), F32),
        ],
        compiler_params=_params("parallel", "arbitrary"),
        name="conv_mlp",
    )(x, x, x, g_pre, w_up, w_up, conv_w, conv_w, conv_b, conv_b, w_down, g_post)


def _rel_bucket(rel):
    nb = NUM_BUCKETS // 2
    max_exact = nb // 2
    ret = jnp.where(rel > 0, nb, 0)
    n = jnp.abs(rel)
    nf = jnp.maximum(n, 1).astype(F32)
    large = max_exact + (jnp.log(nf / max_exact) / math.log(MAX_DISTANCE / max_exact)
                         * (nb - max_exact)).astype(jnp.int32)
    large = jnp.minimum(large, nb - 1)
    return ret + jnp.where(n < max_exact, n, large)


def _bias_tables(rel_bias, tile):
    assert tile >= MAX_DISTANCE, "far tiles must lie beyond the last distance bucket"
    reach = (NEAR_REACH + 1) * tile
    span = 2 * reach
    rel = jnp.arange(-(reach - 1), reach, dtype=jnp.int32)
    rel_bias = rel_bias.astype(F32) * LOG2E
    by_rel = rel_bias[_rel_bucket(rel)].T
    periodic = jnp.tile(jnp.pad(by_rel, ((0, 0), (0, 1))), (1, tile))[:, :tile * (span - 1)]
    toeplitz = periodic.reshape(N_DIFF_HEADS, tile, span - 1)
    near = jnp.stack([toeplitz[:, :, reach + d * tile - 1:reach + (d + 1) * tile - 1]
                      for d in range(-NEAR_REACH, NEAR_REACH + 1)], axis=1)
    far = jnp.stack([rel_bias[NUM_BUCKETS // 2 - 1], rel_bias[NUM_BUCKETS - 1]], axis=1).astype(F32)
    return jnp.swapaxes(near, 2, 3), far


def kernel(x_prompt, x_sample, rel_bias, ln_mix_pre, ln_mix_post, w_in, lam_q, lam_k, head_norm, w_pool,
           pool_scale, w_out, ln_ffn_pre, ln_ffn_post, w_up, conv_w, conv_b, w_down):
    depth = w_in.shape[0]
    w_in_b, w_out_b, w_pool_b = w_in.astype(BF16), w_out.astype(BF16), w_pool.astype(BF16)
    w_up_b, w_down_b = w_up.astype(BF16), w_down.astype(BF16)
    w_vt_b = jnp.swapaxes(w_in_b[:, :, 2 * ATTN_WIDTH:3 * ATTN_WIDTH], 1, 2)
    tables = {}

    def trunk(x):
        b, s, d = x.shape
        tile = min(ATTN_TILE, s)
        tm = min(ROW_TILE, s)
        assert s % tile == 0 and s % tm == 0 and tm % HALO == 0
        if tile not in tables:
            tables[tile] = _bias_tables(rel_bias, tile)
        bias_near_t, far = tables[tile]
        xf = x.reshape(b * s, d)
        for i in range(depth):
            lam_init = 0.8 - 0.6 * math.exp(-0.3 * i)
            q, k, vt, p = _inproj(xf, ln_mix_pre[i][None], w_in_b[i], w_vt_b[i], tile)
            a = _attention(q, k, vt, bias_near_t, far, lam_q[i], lam_k[i], head_norm[i][None], lam_init,
                           tile, s)
            xf = _mixout(a, p, xf, w_out_b[i], w_pool_b[i], pool_scale[i][None], ln_mix_post[i][None], s, tm)
            xf = _ffn(xf, ln_ffn_pre[i][None], w_up_b[i], conv_w[i], conv_b[i][None], w_down_b[i],
                      ln_ffn_post[i][None], s, tm)
        return xf.reshape(b, s, d)

    return (trunk(x_prompt), trunk(x_sample))
```

```python
import functools
import math

import jax
import jax.numpy as jnp
from jax import lax
from jax.experimental import pallas as pl
from jax.experimental.pallas import tpu as pltpu

D_MODEL = 1024
ATTN_WIDTH = 512
POOL_WIDTH = D_MODEL - ATTN_WIDTH
N_DIFF_HEADS = 4
DIFF_HEAD_DIM = 64
V_HEAD_DIM = 2 * DIFF_HEAD_DIM
POOL_WINDOWS = (2, 4, 8, 16)
POOL_GROUP = POOL_WIDTH // len(POOL_WINDOWS)
D_FF = 2816
NUM_BUCKETS = 32
MAX_DISTANCE = 128
RMS_EPS = 1e-6
QK_SCALE = DIFF_HEAD_DIM ** -0.5
LOG2E = math.log2(math.e)

LANES = 128
SUBLANES = 8
ATTN_TILE = 512
ROW_TILE = 512
FF_CHUNK = D_FF // 2
HALO = SUBLANES
VA_COLS = 2 * V_HEAD_DIM
ROW_CHUNK = 512
NEAR_REACH = 2
MASK_VALUE = -1e30
VMEM_LIMIT_BYTES = 56 * 1024 * 1024

F32 = jnp.float32
BF16 = jnp.bfloat16
NT_DIMS = (((1,), (1,)), ((), ()))


def _rms(x, g):
    var = jnp.mean(x * x, axis=-1, keepdims=True)
    return x * lax.rsqrt(var + RMS_EPS) * g


def _params(*semantics):
    return pltpu.CompilerParams(dimension_semantics=semantics, vmem_limit_bytes=VMEM_LIMIT_BYTES)


def _inproj_kernel(x_ref, g_ref, w_ref, wkt_ref, q_ref, kt_ref, va_ref, p_ref):
    h = _rms(x_ref[...], g_ref[...]).astype(BF16)
    a = ATTN_WIDTH
    q = jnp.dot(h, w_ref[:, 0:a], preferred_element_type=F32)
    q_ref[...] = (q * (QK_SCALE * LOG2E)).astype(BF16)
    kt_ref[0] = lax.dot_general(wkt_ref[...], h, NT_DIMS, preferred_element_type=F32).astype(BF16)
    p_ref[...] = jnp.dot(h, w_ref[:, 3 * a:], preferred_element_type=F32)
    v = jnp.dot(h, w_ref[:, 2 * a:3 * a], preferred_element_type=F32).astype(BF16)
    ones = jnp.ones((v.shape[0], V_HEAD_DIM), BF16)
    for hd in range(N_DIFF_HEADS):
        c0 = hd * VA_COLS
        va_ref[:, c0:c0 + V_HEAD_DIM] = v[:, hd * V_HEAD_DIM:(hd + 1) * V_HEAD_DIM]
        va_ref[:, c0 + V_HEAD_DIM:c0 + VA_COLS] = ones


def _inproj(x, g, w_in, w_kt, tm):
    t = x.shape[0]
    row = lambda i: (i, 0)
    const = lambda i: (0, 0)
    return pl.pallas_call(
        _inproj_kernel,
        grid=(t // tm,),
        in_specs=[
            pl.BlockSpec((tm, D_MODEL), row),
            pl.BlockSpec((1, D_MODEL), const),
            pl.BlockSpec(w_in.shape, const),
            pl.BlockSpec(w_kt.shape, const),
        ],
        out_specs=[
            pl.BlockSpec((tm, ATTN_WIDTH), row),
            pl.BlockSpec((1, ATTN_WIDTH, tm), lambda i: (i, 0, 0)),
            pl.BlockSpec((tm, N_DIFF_HEADS * VA_COLS), row),
            pl.BlockSpec((tm, POOL_WIDTH), row),
        ],
        out_shape=[
            jax.ShapeDtypeStruct((t, ATTN_WIDTH), BF16),
            jax.ShapeDtypeStruct((t // tm, ATTN_WIDTH, tm), BF16),
            jax.ShapeDtypeStruct((t, N_DIFF_HEADS * VA_COLS), BF16),
            jax.ShapeDtypeStruct((t, POOL_WIDTH), F32),
        ],
        compiler_params=_params("parallel"),
        name="inproj",
    )(x, g, w_in, w_kt)


def _attn_kernel(far_ref, q_ref, kt_ref, va_ref, bias_ref, lq_ref, lk_ref, hn_ref, o_ref,
                 qs_scr, sa_scr, sb_scr, m_scr, acc_scr, *, tile, nk, lam_init):
    h = pl.program_id(1)
    i = pl.program_id(2)
    t = tile
    rc = min(ROW_CHUNK, t)
    n_chunks = 2 * t // rc

    q = q_ref[...]
    lane = lax.broadcasted_iota(jnp.int32, (t, LANES), 1)
    zero = jnp.zeros_like(q)
    qs_scr[0:t, :] = jnp.where(lane < DIFF_HEAD_DIM, q, zero)
    qs_scr[t:2 * t, :] = jnp.where(lane >= DIFF_HEAD_DIM, q, zero)
    m_scr[...] = jnp.full(m_scr.shape, MASK_VALUE, F32)
    acc_scr[...] = jnp.zeros(acc_scr.shape, F32)

    def scores(j, rows):
        return jnp.dot(qs_scr[rows, :], kt_ref[j], preferred_element_type=F32)

    def tile_step(j, s_scr, near, c):
        j_next = jnp.minimum(j + 2, nk - 1)
        vj = va_ref[pl.ds(pl.multiple_of(j * t, t), t), :]
        for ci in range(n_chunks):
            rows = slice(ci * rc, (ci + 1) * rc)
            s = s_scr[rows, :]
            if near:
                q0 = (ci * rc) % t
                s = s + bias_ref[0, j - i + NEAR_REACH, q0:q0 + rc, :]
            m_prev = m_scr[rows, :]
            m_cur = jnp.max(s, axis=1, keepdims=True)
            if c is not None:
                m_cur = m_cur + c
            m_next = jnp.maximum(m_prev, m_cur)
            shift = m_next if c is None else m_next - c
            p = jnp.exp2(s - jnp.concatenate([shift] * (t // LANES), axis=1)).astype(BF16)
            alpha = jnp.exp2(m_prev - m_next)
            pv = jnp.dot(p, vj, preferred_element_type=F32)
            acc_scr[rows, :] = jnp.concatenate([alpha] * (VA_COLS // LANES), axis=1) * acc_scr[rows, :] + pv
            m_scr[rows, :] = m_next
            s_scr[rows, :] = scores(j_next, rows)

    def pair(m, near, c):
        j0 = 2 * m
        tile_step(j0, sa_scr, near, c)
        tile_step(j0 + 1, sb_scr, near, c)

    c_left = far_ref[h, 0]
    c_right = far_ref[h, 1]

    def left_body(m, carry):
        pair(m, False, c_left)
        return carry

    def right_body(m, carry):
        pair(m, False, c_right)
        return carry

    first_near = lax.div(i + 1, 2) - 1
    for ci in range(n_chunks):
        rows = slice(ci * rc, (ci + 1) * rc)
        sa_scr[rows, :] = scores(0, rows)
        sb_scr[rows, :] = scores(1, rows)
    lax.fori_loop(0, jnp.maximum(first_near, 0), left_body, 0)
    for m in (first_near, first_near + 1):

        @pl.when(jnp.logical_and(m >= 0, m < nk // 2))
        def _():
            pair(m, True, None)

    lax.fori_loop(first_near + 2, nk // 2, right_body, 0)

    acc = acc_scr[...]
    o = acc[:, 0:V_HEAD_DIM] / acc[:, V_HEAD_DIM:VA_COLS]
    prod = lq_ref[...] * lk_ref[...]
    lam = (jnp.exp(jnp.sum(prod[0:1], axis=1, keepdims=True))
           - jnp.exp(jnp.sum(prod[1:2], axis=1, keepdims=True)) + lam_init)
    a = o[0:t] - lam * o[t:2 * t]
    o_ref[...] = (_rms(a, hn_ref[...]) * (1.0 - lam_init)).astype(o_ref.dtype)


def _attention(q, kt, va, bias_near, far, lam_q, lam_k, head_norm, lam_init, tile, seq_len):
    t_all = q.shape[0]
    b = t_all // seq_len
    nk = seq_len // tile
    assert nk % 2 == 0, "key tiles are processed in pairs"
    kern = functools.partial(_attn_kernel, tile=tile, nk=nk, lam_init=lam_init)
    return pl.pallas_call(
        kern,
        grid=(b, N_DIFF_HEADS, nk),
        in_specs=[
            pl.BlockSpec(memory_space=pltpu.SMEM),
            pl.BlockSpec((tile, V_HEAD_DIM), lambda bi, h, i: (bi * nk + i, h)),
            pl.BlockSpec((nk, V_HEAD_DIM, tile), lambda bi, h, i: (bi, h, 0)),
            pl.BlockSpec((seq_len, VA_COLS), lambda bi, h, i: (bi, h)),
            pl.BlockSpec((1, 2 * NEAR_REACH + 1, tile, tile), lambda bi, h, i: (h, 0, 0, 0)),
            pl.BlockSpec((2, DIFF_HEAD_DIM), lambda bi, h, i: (0, 0)),
            pl.BlockSpec((2, DIFF_HEAD_DIM), lambda bi, h, i: (0, 0)),
            pl.BlockSpec((1, V_HEAD_DIM), lambda bi, h, i: (0, 0)),
        ],
        out_specs=pl.BlockSpec((tile, V_HEAD_DIM), lambda bi, h, i: (bi * nk + i, h)),
        out_shape=jax.ShapeDtypeStruct((t_all, ATTN_WIDTH), BF16),
        scratch_shapes=[
            pltpu.VMEM((2 * tile, V_HEAD_DIM), BF16),
            pltpu.VMEM((2 * tile, tile), F32),
            pltpu.VMEM((2 * tile, tile), F32),
            pltpu.VMEM((2 * tile, LANES), F32),
            pltpu.VMEM((2 * tile, VA_COLS), F32),
        ],
        compiler_params=_params("parallel", "parallel", "arbitrary"),
        name="diff_attention",
    )(far, q, kt, va, bias_near, lam_q, lam_k, head_norm)


def _mixout_kernel(a_ref, p_ref, pprev_ref, pnext_ref, x_ref, wout_ref, wpool_ref, ps_ref, g_ref,
                   o_ref, pbuf, *, tm, tiles_per_seq, seq_len):
    ti = lax.rem(pl.program_id(0), tiles_per_seq)
    keep_prev = (ti > 0).astype(F32)
    keep_next = (ti < tiles_per_seq - 1).astype(F32)
    pbuf[0:HALO, :] = pprev_ref[...] * keep_prev
    pbuf[HALO:HALO + tm, :] = p_ref[...]
    pbuf[HALO + tm:, :] = pnext_ref[...] * keep_next

    pos = ti * tm + lax.broadcasted_iota(jnp.int32, (tm, 1), 0)
    mixed = []
    for g, win in enumerate(POOL_WINDOWS):
        lo_off = -(win // 2)
        hi_off = win - win // 2 - 1
        cols = slice(g * POOL_GROUP, (g + 1) * POOL_GROUP)
        total = pbuf[HALO + lo_off:HALO + lo_off + tm, cols]
        for off in range(lo_off + 1, hi_off + 1):
            total = total + pbuf[HALO + off:HALO + off + tm, cols]
        lo = jnp.maximum(pos + lo_off, 0)
        hi = jnp.minimum(pos + hi_off, seq_len - 1)
        cnt = (hi - lo + 1).astype(F32)
        y = total / cnt - pbuf[HALO:HALO + tm, cols]
        mixed.append(jnp.dot(y.astype(BF16), wpool_ref[g], preferred_element_type=F32))
    m = jnp.concatenate(mixed, axis=1) * ps_ref[...]

    o = jnp.dot(a_ref[...], wout_ref[0:ATTN_WIDTH, :], preferred_element_type=F32)
    o = o + jnp.dot(m.astype(BF16), wout_ref[ATTN_WIDTH:, :], preferred_element_type=F32)
    o_ref[...] = x_ref[...] + _rms(o, g_ref[...])


def _halo_specs(tm, width, n_rows):
    per_tile = tm // HALO
    last = n_rows // HALO - 1
    prev = pl.BlockSpec((HALO, width), lambda i, *_: (jnp.maximum(i * per_tile - 1, 0), 0))
    nxt = pl.BlockSpec((HALO, width), lambda i, *_: (jnp.minimum((i + 1) * per_tile, last), 0))
    return prev, nxt


def _mixout(a, p, x, w_out, w_pool, pool_scale, g, seq_len, tm):
    t = x.shape[0]
    row = lambda i: (i, 0)
    const = lambda i: (0, 0)
    prev, nxt = _halo_specs(tm, POOL_WIDTH, t)
    kern = functools.partial(_mixout_kernel, tm=tm, tiles_per_seq=seq_len // tm, seq_len=seq_len)
    return pl.pallas_call(
        kern,
        grid=(t // tm,),
        in_specs=[
            pl.BlockSpec((tm, ATTN_WIDTH), row),
            pl.BlockSpec((tm, POOL_WIDTH), row),
            prev,
            nxt,
            pl.BlockSpec((tm, D_MODEL), row),
            pl.BlockSpec(w_out.shape, const),
            pl.BlockSpec(w_pool.shape, lambda i: (0, 0, 0)),
            pl.BlockSpec((1, POOL_WIDTH), const),
            pl.BlockSpec((1, D_MODEL), const),
        ],
        out_specs=pl.BlockSpec((tm, D_MODEL), row),
        out_shape=jax.ShapeDtypeStruct((t, D_MODEL), F32),
        scratch_shapes=[pltpu.VMEM((tm + 2 * HALO, POOL_WIDTH), F32)],
        compiler_params=_params("parallel"),
        name="pool_outproj",
    )(a, p, p, p, x, w_out, w_pool, pool_scale, g)


def _gelu_tanh(x):
    return 0.5 * x * (1.0 + jnp.tanh(math.sqrt(2.0 / math.pi) * (x + 0.044715 * (x * x * x))))


def _ffn_kernel(x_ref, xprev_ref, xnext_ref, gpre_ref, wg_ref, wv_ref, cwg_ref, cwv_ref, cbg_ref, cbv_ref,
                wd_ref, gpost_ref, o_ref, h_scr, u_scr, acc_scr, *, tm, tiles_per_seq):
    c = pl.program_id(1)

    @pl.when(c == 0)
    def _():
        ti = lax.rem(pl.program_id(0), tiles_per_seq)
        keep_prev = (ti > 0).astype(F32)
        keep_next = (ti < tiles_per_seq - 1).astype(F32)
        g = gpre_ref[...]
        h_scr[0:HALO, :] = _rms(xprev_ref[...], g) * keep_prev
        h_scr[HALO:HALO + tm, :] = _rms(x_ref[...], g)
        h_scr[HALO + tm:, :] = _rms(xnext_ref[...], g) * keep_next
        acc_scr[...] = jnp.zeros(acc_scr.shape, F32)

    h = h_scr[...].astype(BF16)

    def conv(w_ref, cw_ref, cb_ref):
        u_scr[...] = jnp.dot(h, w_ref[...], preferred_element_type=F32)
        cw = cw_ref[...]
        return (u_scr[HALO - 1:HALO - 1 + tm, :] * cw[0:1]
                + u_scr[HALO:HALO + tm, :] * cw[1:2]
                + u_scr[HALO + 1:HALO + 1 + tm, :] * cw[2:3]
                + cb_ref[...])

    gate = conv(wg_ref, cwg_ref, cbg_ref)
    val = conv(wv_ref, cwv_ref, cbv_ref)
    f = (_gelu_tanh(gate) * val).astype(BF16)
    acc_scr[...] += jnp.dot(f, wd_ref[...], preferred_element_type=F32)

    @pl.when(c == pl.num_programs(1) - 1)
    def _():
        o_ref[...] = x_ref[...] + _rms(acc_scr[...], gpost_ref[...])


def _ffn(x, g_pre, w_up, conv_w, conv_b, w_down, g_post, seq_len, tm):
    t = x.shape[0]
    n_chunks = D_FF // FF_CHUNK
    row = lambda i, c: (i, 0)
    const = lambda i, c: (0, 0)
    gate_cols = lambda i, c: (0, c)
    val_cols = lambda i, c: (0, n_chunks + c)
    prev, nxt = _halo_specs(tm, D_MODEL, t)
    kern = functools.partial(_ffn_kernel, tm=tm, tiles_per_seq=seq_len // tm)
    return pl.pallas_call(
        kern,
        grid=(t // tm, n_chunks),
        in_specs=[
            pl.BlockSpec((tm, D_MODEL), row),
            prev,
            nxt,
            pl.BlockSpec((1, D_MODEL), const),
            pl.BlockSpec((D_MODEL, FF_CHUNK), gate_cols),
            pl.BlockSpec((D_MODEL, FF_CHUNK), val_cols),
            pl.BlockSpec((3, FF_CHUNK), gate_cols),
            pl.BlockSpec((3, FF_CHUNK), val_cols),
            pl.BlockSpec((1, FF_CHUNK), gate_cols),
            pl.BlockSpec((1, FF_CHUNK), val_cols),
            pl.BlockSpec((FF_CHUNK, D_MODEL), lambda i, c: (c, 0)),
            pl.BlockSpec((1, D_MODEL), const),
        ],
        out_specs=pl.BlockSpec((tm, D_MODEL), row),
        out_shape=jax.ShapeDtypeStruct((t, D_MODEL), F32),
        scratch_shapes=[
            pltpu.VMEM((tm + 2 * HALO, D_MODEL), F32),
            pltpu.VMEM((tm + 2 * HALO, FF_CHUNK), F32),
            pltpu.VMEM((tm, D_MODEL), F32),
        ],
        compiler_params=_params("parallel", "arbitrary"),
        name="conv_mlp",
    )(x, x, x, g_pre, w_up, w_up, conv_w, conv_w, conv_b, conv_b, w_down, g_post)


def _rel_bucket(rel):
    nb = NUM_BUCKETS // 2
    max_exact = nb // 2
    ret = jnp.where(rel > 0, nb, 0)
    n = jnp.abs(rel)
    nf = jnp.maximum(n, 1).astype(F32)
    large = max_exact + (jnp.log(nf / max_exact) / math.log(MAX_DISTANCE / max_exact)
                         * (nb - max_exact)).astype(jnp.int32)
    large = jnp.minimum(large, nb - 1)
    return ret + jnp.where(n < max_exact, n, large)


def _bias_tables(rel_bias, tile):
    assert tile >= MAX_DISTANCE, "far tiles must lie beyond the last distance bucket"
    reach = (NEAR_REACH + 1) * tile
    span = 2 * reach
    rel = jnp.arange(-(reach - 1), reach, dtype=jnp.int32)
    rel_bias = rel_bias.astype(F32) * LOG2E
    by_rel = rel_bias[_rel_bucket(rel)].T
    periodic = jnp.tile(jnp.pad(by_rel, ((0, 0), (0, 1))), (1, tile))[:, :tile * (span - 1)]
    toeplitz = periodic.reshape(N_DIFF_HEADS, tile, span - 1)
    near = jnp.stack([toeplitz[:, :, reach + d * tile - 1:reach + (d + 1) * tile - 1]
                      for d in range(-NEAR_REACH, NEAR_REACH + 1)], axis=1)
    far = jnp.stack([rel_bias[NUM_BUCKETS // 2 - 1], rel_bias[NUM_BUCKETS - 1]], axis=1).astype(F32)
    return near, far


def kernel(x_prompt, x_sample, rel_bias, ln_mix_pre, ln_mix_post, w_in, lam_q, lam_k, head_norm, w_pool,
           pool_scale, w_out, ln_ffn_pre, ln_ffn_post, w_up, conv_w, conv_b, w_down):
    depth = w_in.shape[0]
    w_in_b, w_out_b, w_pool_b = w_in.astype(BF16), w_out.astype(BF16), w_pool.astype(BF16)
    w_up_b, w_down_b = w_up.astype(BF16), w_down.astype(BF16)
    w_kt_b = jnp.swapaxes(w_in_b[:, :, ATTN_WIDTH:2 * ATTN_WIDTH], 1, 2)
    tables = {}

    def trunk(x):
        b, s, d = x.shape
        tile = min(ATTN_TILE, s)
        tm = min(ROW_TILE, s)
        assert s % tile == 0 and s % tm == 0 and tm % HALO == 0
        if tile not in tables:
            tables[tile] = _bias_tables(rel_bias, tile)
        bias_near, far = tables[tile]
        xf = x.reshape(b * s, d)
        for i in range(depth):
            lam_init = 0.8 - 0.6 * math.exp(-0.3 * i)
            q, kt, va, p = _inproj(xf, ln_mix_pre[i][None], w_in_b[i], w_kt_b[i], tile)
            a = _attention(q, kt, va, bias_near, far, lam_q[i], lam_k[i], head_norm[i][None], lam_init,
                           tile, s)
            xf = _mixout(a, p, xf, w_out_b[i], w_pool_b[i], pool_scale[i][None], ln_mix_post[i][None], s, tm)
            xf = _ffn(xf, ln_ffn_pre[i][None], w_up_b[i], conv_w[i], conv_b[i][None], w_down_b[i],
                      ln_ffn_post[i][None], s, tm)
        return xf.reshape(b, s, d)

    return (trunk(x_prompt), trunk(x_sample))
```

```python
import functools
import math

import jax
import jax.numpy as jnp
from jax import lax
from jax.experimental import pallas as pl
from jax.experimental.pallas import tpu as pltpu

D_MODEL = 1024
ATTN_WIDTH = 512
POOL_WIDTH = D_MODEL - ATTN_WIDTH
N_DIFF_HEADS = 4
DIFF_HEAD_DIM = 64
V_HEAD_DIM = 2 * DIFF_HEAD_DIM
POOL_WINDOWS = (2, 4, 8, 16)
POOL_GROUP = POOL_WIDTH // len(POOL_WINDOWS)
D_FF = 2816
NUM_BUCKETS = 32
MAX_DISTANCE = 128
RMS_EPS = 1e-6
QK_SCALE = DIFF_HEAD_DIM ** -0.5
LOG2E = math.log2(math.e)

LANES = 128
SUBLANES = 8
MXU_WIDTH = 256
ATTN_TILE = 512
ROW_TILE = 512
FF_CHUNK = D_FF // 2
HALO = SUBLANES
ONES_ROWS = 16
VT_ROWS = V_HEAD_DIM + ONES_ROWS
NEAR_REACH = 2
MASK_VALUE = -1e30
VMEM_LIMIT_BYTES = 48 * 1024 * 1024

F32 = jnp.float32
BF16 = jnp.bfloat16
NT_DIMS = (((1,), (1,)), ((), ()))


def _rms(x, g):
    var = jnp.mean(x * x, axis=-1, keepdims=True)
    return x * lax.rsqrt(var + RMS_EPS) * g


def _params(*semantics):
    return pltpu.CompilerParams(dimension_semantics=semantics, vmem_limit_bytes=VMEM_LIMIT_BYTES)


def _inproj_kernel(x_ref, g_ref, w_ref, wvt_ref, q_ref, k_ref, vt_ref, p_ref):
    h = _rms(x_ref[...], g_ref[...]).astype(BF16)
    a = ATTN_WIDTH
    q = jnp.dot(h, w_ref[:, 0:a], preferred_element_type=F32)
    q_ref[...] = (q * (QK_SCALE * LOG2E)).astype(BF16)
    k_ref[...] = jnp.dot(h, w_ref[:, a:2 * a], preferred_element_type=F32).astype(BF16)
    p_ref[...] = jnp.dot(h, w_ref[:, 3 * a:], preferred_element_type=F32)
    vt = lax.dot_general(wvt_ref[...], h, NT_DIMS, preferred_element_type=F32).astype(BF16)
    ones = jnp.ones((ONES_ROWS, vt.shape[1]), BF16)
    for hd in range(N_DIFF_HEADS):
        r0 = hd * VT_ROWS
        vt_ref[0, r0:r0 + V_HEAD_DIM, :] = vt[hd * V_HEAD_DIM:(hd + 1) * V_HEAD_DIM]
        vt_ref[0, r0 + V_HEAD_DIM:r0 + VT_ROWS, :] = ones


def _inproj(x, g, w_in, w_vt, tm):
    t = x.shape[0]
    row = lambda i: (i, 0)
    const = lambda i: (0, 0)
    return pl.pallas_call(
        _inproj_kernel,
        grid=(t // tm,),
        in_specs=[
            pl.BlockSpec((tm, D_MODEL), row),
            pl.BlockSpec((1, D_MODEL), const),
            pl.BlockSpec(w_in.shape, const),
            pl.BlockSpec(w_vt.shape, const),
        ],
        out_specs=[
            pl.BlockSpec((tm, ATTN_WIDTH), row),
            pl.BlockSpec((tm, ATTN_WIDTH), row),
            pl.BlockSpec((1, N_DIFF_HEADS * VT_ROWS, tm), lambda i: (i, 0, 0)),
            pl.BlockSpec((tm, POOL_WIDTH), row),
        ],
        out_shape=[
            jax.ShapeDtypeStruct((t, ATTN_WIDTH), BF16),
            jax.ShapeDtypeStruct((t, ATTN_WIDTH), BF16),
            jax.ShapeDtypeStruct((t // tm, N_DIFF_HEADS * VT_ROWS, tm), BF16),
            jax.ShapeDtypeStruct((t, POOL_WIDTH), F32),
        ],
        compiler_params=_params("parallel"),
        name="inproj",
    )(x, g, w_in, w_vt)


def _attn_kernel(far_ref, q_ref, k_ref, vt_ref, bias_ref, lq_ref, lk_ref, hn_ref, o_ref,
                 qs_scr, sa_scr, sb_scr, m_scr, acc_scr, *, tile, nk, lam_init):
    h = pl.program_id(1)
    i = pl.program_id(2)
    t = tile
    cq = min(MXU_WIDTH, t)
    n_chunks = 2 * t // cq

    q = q_ref[...]
    lane = lax.broadcasted_iota(jnp.int32, (t, LANES), 1)
    zero = jnp.zeros_like(q)
    qs_scr[0:t, :] = jnp.where(lane < DIFF_HEAD_DIM, q, zero)
    qs_scr[t:2 * t, :] = jnp.where(lane >= DIFF_HEAD_DIM, q, zero)
    m_scr[...] = jnp.full(m_scr.shape, MASK_VALUE, F32)
    acc_scr[...] = jnp.zeros(acc_scr.shape, F32)

    def scores(j):
        start = pl.multiple_of(j * t, t)
        return lax.dot_general(k_ref[pl.ds(start, t), :], qs_scr[...], NT_DIMS,
                               preferred_element_type=F32)

    def softmax_pv(j, s_ref, near, c):
        vtj = vt_ref[j]
        for ci in range(n_chunks):
            cols = slice(ci * cq, (ci + 1) * cq)
            s = s_ref[:, cols]
            if near:
                q0 = (ci * cq) % t
                s = s + bias_ref[0, j - i + NEAR_REACH, :, q0:q0 + cq]
            m_prev = m_scr[:, cols]
            m_cur = jnp.max(s, axis=0, keepdims=True)
            if c is not None:
                m_cur = m_cur + c
            m_next = jnp.maximum(m_prev, m_cur)
            p = jnp.exp2(s - (m_next if c is None else m_next - c))
            alpha = jnp.exp2(m_prev - m_next)
            acc_scr[:, cols] = alpha * acc_scr[:, cols] + jnp.dot(vtj, p.astype(BF16),
                                                                  preferred_element_type=F32)
            m_scr[:, cols] = m_next

    def pair(m, near, c):
        j0 = 2 * m
        sb_scr[...] = scores(j0 + 1)
        softmax_pv(j0, sa_scr, near, c)
        sa_scr[...] = scores(jnp.minimum(j0 + 2, nk - 2))
        softmax_pv(j0 + 1, sb_scr, near, c)

    c_left = far_ref[h, 0]
    c_right = far_ref[h, 1]

    def left_body(m, carry):
        pair(m, False, c_left)
        return carry

    def right_body(m, carry):
        pair(m, False, c_right)
        return carry

    first_near = lax.div(i + 1, 2) - 1
    sa_scr[...] = scores(0)
    lax.fori_loop(0, jnp.maximum(first_near, 0), left_body, 0)
    for m in (first_near, first_near + 1):

        @pl.when(jnp.logical_and(m >= 0, m < nk // 2))
        def _():
            pair(m, True, None)

    lax.fori_loop(first_near + 2, nk // 2, right_body, 0)

    o = acc_scr[0:V_HEAD_DIM, :] / acc_scr[V_HEAD_DIM:V_HEAD_DIM + 1, :]
    prod = lq_ref[...] * lk_ref[...]
    lam = (jnp.exp(jnp.sum(prod[0:1], axis=1, keepdims=True))
           - jnp.exp(jnp.sum(prod[1:2], axis=1, keepdims=True)) + lam_init)
    a = (o[:, 0:t] - lam * o[:, t:2 * t]).T
    o_ref[...] = (_rms(a, hn_ref[...]) * (1.0 - lam_init)).astype(o_ref.dtype)


def _attention(q, k, vt, bias_near_t, far, lam_q, lam_k, head_norm, lam_init, tile, seq_len):
    t_all = q.shape[0]
    b = t_all // seq_len
    nk = seq_len // tile
    assert nk % 2 == 0, "key tiles are processed in pairs"
    kern = functools.partial(_attn_kernel, tile=tile, nk=nk, lam_init=lam_init)
    return pl.pallas_call(
        kern,
        grid=(b, N_DIFF_HEADS, nk),
        in_specs=[
            pl.BlockSpec(memory_space=pltpu.SMEM),
            pl.BlockSpec((tile, V_HEAD_DIM), lambda bi, h, i: (bi * nk + i, h)),
            pl.BlockSpec((seq_len, V_HEAD_DIM), lambda bi, h, i: (bi, h)),
            pl.BlockSpec((nk, VT_ROWS, tile), lambda bi, h, i: (bi, h, 0)),
            pl.BlockSpec((1, 2 * NEAR_REACH + 1, tile, tile), lambda bi, h, i: (h, 0, 0, 0)),
            pl.BlockSpec((2, DIFF_HEAD_DIM), lambda bi, h, i: (0, 0)),
            pl.BlockSpec((2, DIFF_HEAD_DIM), lambda bi, h, i: (0, 0)),
            pl.BlockSpec((1, V_HEAD_DIM), lambda bi, h, i: (0, 0)),
        ],
        out_specs=pl.BlockSpec((tile, V_HEAD_DIM), lambda bi, h, i: (bi * nk + i, h)),
        out_shape=jax.ShapeDtypeStruct((t_all, ATTN_WIDTH), BF16),
        scratch_shapes=[
            pltpu.VMEM((2 * tile, V_HEAD_DIM), BF16),
            pltpu.VMEM((tile, 2 * tile), F32),
            pltpu.VMEM((tile, 2 * tile), F32),
            pltpu.VMEM((1, 2 * tile), F32),
            pltpu.VMEM((VT_ROWS, 2 * tile), F32),
        ],
        compiler_params=_params("parallel", "parallel", "arbitrary"),
        name="diff_attention",
    )(far, q, k, vt, bias_near_t, lam_q, lam_k, head_norm)


def _mixout_kernel(a_ref, p_ref, pprev_ref, pnext_ref, x_ref, wout_ref, wpool_ref, ps_ref, g_ref,
                   o_ref, pbuf, *, tm, tiles_per_seq, seq_len):
    ti = lax.rem(pl.program_id(0), tiles_per_seq)
    keep_prev = (ti > 0).astype(F32)
    keep_next = (ti < tiles_per_seq - 1).astype(F32)
    pbuf[0:HALO, :] = pprev_ref[...] * keep_prev
    pbuf[HALO:HALO + tm, :] = p_ref[...]
    pbuf[HALO + tm:, :] = pnext_ref[...] * keep_next

    pos = ti * tm + lax.broadcasted_iota(jnp.int32, (tm, 1), 0)
    mixed = []
    for g, win in enumerate(POOL_WINDOWS):
        lo_off = -(win // 2)
        hi_off = win - win // 2 - 1
        cols = slice(g * POOL_GROUP, (g + 1) * POOL_GROUP)
        total = pbuf[HALO + lo_off:HALO + lo_off + tm, cols]
        for off in range(lo_off + 1, hi_off + 1):
            total = total + pbuf[HALO + off:HALO + off + tm, cols]
        lo = jnp.maximum(pos + lo_off, 0)
        hi = jnp.minimum(pos + hi_off, seq_len - 1)
        cnt = (hi - lo + 1).astype(F32)
        y = total / cnt - pbuf[HALO:HALO + tm, cols]
        mixed.append(jnp.dot(y.astype(BF16), wpool_ref[g], preferred_element_type=F32))
    m = jnp.concatenate(mixed, axis=1) * ps_ref[...]

    o = jnp.dot(a_ref[...], wout_ref[0:ATTN_WIDTH, :], preferred_element_type=F32)
    o = o + jnp.dot(m.astype(BF16), wout_ref[ATTN_WIDTH:, :], preferred_element_type=F32)
    o_ref[...] = x_ref[...] + _rms(o, g_ref[...])


def _halo_specs(tm, width, n_rows):
    per_tile = tm // HALO
    last = n_rows // HALO - 1
    prev = pl.BlockSpec((HALO, width), lambda i, *_: (jnp.maximum(i * per_tile - 1, 0), 0))
    nxt = pl.BlockSpec((HALO, width), lambda i, *_: (jnp.minimum((i + 1) * per_tile, last), 0))
    return prev, nxt


def _mixout(a, p, x, w_out, w_pool, pool_scale, g, seq_len, tm):
    t = x.shape[0]
    row = lambda i: (i, 0)
    const = lambda i: (0, 0)
    prev, nxt = _halo_specs(tm, POOL_WIDTH, t)
    kern = functools.partial(_mixout_kernel, tm=tm, tiles_per_seq=seq_len // tm, seq_len=seq_len)
    return pl.pallas_call(
        kern,
        grid=(t // tm,),
        in_specs=[
            pl.BlockSpec((tm, ATTN_WIDTH), row),
            pl.BlockSpec((tm, POOL_WIDTH), row),
            prev,
            nxt,
            pl.BlockSpec((tm, D_MODEL), row),
            pl.BlockSpec(w_out.shape, const),
            pl.BlockSpec(w_pool.shape, lambda i: (0, 0, 0)),
            pl.BlockSpec((1, POOL_WIDTH), const),
            pl.BlockSpec((1, D_MODEL), const),
        ],
        out_specs=pl.BlockSpec((tm, D_MODEL), row),
        out_shape=jax.ShapeDtypeStruct((t, D_MODEL), F32),
        scratch_shapes=[pltpu.VMEM((tm + 2 * HALO, POOL_WIDTH), F32)],
        compiler_params=_params("parallel"),
        name="pool_outproj",
    )(a, p, p, p, x, w_out, w_pool, pool_scale, g)


def _gelu_tanh(x):
    return 0.5 * x * (1.0 + jnp.tanh(math.sqrt(2.0 / math.pi) * (x + 0.044715 * (x * x * x))))


def _ffn_kernel(x_ref, xprev_ref, xnext_ref, gpre_ref, wg_ref, wv_ref, cwg_ref, cwv_ref, cbg_ref, cbv_ref,
                wd_ref, gpost_ref, o_ref, h_scr, u_scr, acc_scr, *, tm, tiles_per_seq):
    c = pl.program_id(1)

    @pl.when(c == 0)
    def _():
        ti = lax.rem(pl.program_id(0), tiles_per_seq)
        keep_prev = (ti > 0).astype(F32)
        keep_next = (ti < tiles_per_seq - 1).astype(F32)
        g = gpre_ref[...]
        h_scr[0:HALO, :] = _rms(xprev_ref[...], g) * keep_prev
        h_scr[HALO:HALO + tm, :] = _rms(x_ref[...], g)
        h_scr[HALO + tm:, :] = _rms(xnext_ref[...], g) * keep_next
        acc_scr[...] = jnp.zeros(acc_scr.shape, F32)

    h = h_scr[...].astype(BF16)

    def conv(w_ref, cw_ref, cb_ref):
        u_scr[...] = jnp.dot(h, w_ref[...], preferred_element_type=F32)
        cw = cw_ref[...]
        return (u_scr[HALO - 1:HALO - 1 + tm, :] * cw[0:1]
                + u_scr[HALO:HALO + tm, :] * cw[1:2]
                + u_scr[HALO + 1:HALO + 1 + tm, :] * cw[2:3]
                + cb_ref[...])

    gate = conv(wg_ref, cwg_ref, cbg_ref)
    val = conv(wv_ref, cwv_ref, cbv_ref)
    f = (_gelu_tanh(gate) * val).astype(BF16)
    acc_scr[...] += jnp.dot(f, wd_ref[...], preferred_element_type=F32)

    @pl.when(c == pl.num_programs(1) - 1)
    def _():
        o_ref[...] = x_ref[...] + _rms(acc_scr[...], gpost_ref[...])


def _ffn(x, g_pre, w_up, conv_w, conv_b, w_down, g_post, seq_len, tm):
    t = x.shape[0]
    n_chunks = D_FF // FF_CHUNK
    row = lambda i, c: (i, 0)
    const = lambda i, c: (0, 0)
    gate_cols = lambda i, c: (0, c)
    val_cols = lambda i, c: (0, n_chunks + c)
    prev, nxt = _halo_specs(tm, D_MODEL, t)
    kern = functools.partial(_ffn_kernel, tm=tm, tiles_per_seq=seq_len // tm)
    return pl.pallas_call(
        kern,
        grid=(t // tm, n_chunks),
        in_specs=[
            pl.BlockSpec((tm, D_MODEL), row),
            prev,
            nxt,
            pl.BlockSpec((1, D_MODEL), const),
            pl.BlockSpec((D_MODEL, FF_CHUNK), gate_cols),
            pl.BlockSpec((D_MODEL, FF_CHUNK), val_cols),
            pl.BlockSpec((3, FF_CHUNK), gate_cols),
            pl.BlockSpec((3, FF_CHUNK), val_cols),
            pl.BlockSpec((1, FF_CHUNK), gate_cols),
            pl.BlockSpec((1, FF_CHUNK), val_cols),
            pl.BlockSpec((FF_CHUNK, D_MODEL), lambda i, c: (c, 0)),
            pl.BlockSpec((1, D_MODEL), const),
        ],
        out_specs=pl.BlockSpec((tm, D_MODEL), row),
        out_shape=jax.ShapeDtypeStruct((t, D_MODEL), F32),
        scratch_shapes=[
            pltpu.VMEM((tm + 2 * HALO, D_MODEL), F32),
            pltpu.VMEM((tm + 2 * HALO, FF_CHUNK), F32),
            pltpu.VMEM((tm, D_MODEL), F32),
        ],
        compiler_params=_params("parallel", "arbitrary"),
        name="conv_mlp",
    )(x, x, x, g_pre, w_up, w_up, conv_w, conv_w, conv_b, conv_b, w_down, g_post)


def _rel_bucket(rel):
    nb = NUM_BUCKETS // 2
    max_exact = nb // 2
    ret = jnp.where(rel > 0, nb, 0)
    n = jnp.abs(rel)
    nf = jnp.maximum(n, 1).astype(F32)
    large = max_exact + (jnp.log(nf / max_exact) / math.log(MAX_DISTANCE / max_exact)
                         * (nb - max_exact)).astype(jnp.int32)
    large = jnp.minimum(large, nb - 1)
    return ret + jnp.where(n < max_exact, n, large)


def _bias_expand_kernel(y_ref, o_ref, *, tile):
    span = y_ref.shape[-1]
    table = jnp.broadcast_to(y_ref[0], (tile, span))
    skewed = pltpu.roll(table, 0, 1, stride=1, stride_axis=0)
    for di, d in enumerate(range(-NEAR_REACH, NEAR_REACH + 1)):
        start = (-d * tile) % span
        o_ref[0, di] = skewed[:, start:start + tile]


def _bias_tables(rel_bias, tile):
    assert tile >= MAX_DISTANCE, "far tiles must lie beyond the last distance bucket"
    n_near = 2 * NEAR_REACH + 1
    span = 2 * (NEAR_REACH + 1) * tile
    m = jnp.arange(span, dtype=jnp.int32)
    rel = jnp.where(m < span // 2, -m, span - m)
    rel_bias = rel_bias.astype(F32) * LOG2E
    by_rel = rel_bias[_rel_bucket(rel)].T[:, None, :]
    near = pl.pallas_call(
        functools.partial(_bias_expand_kernel, tile=tile),
        grid=(N_DIFF_HEADS,),
        in_specs=[pl.BlockSpec((1, 1, span), lambda h: (h, 0, 0))],
        out_specs=pl.BlockSpec((1, n_near, tile, tile), lambda h: (h, 0, 0, 0)),
        out_shape=jax.ShapeDtypeStruct((N_DIFF_HEADS, n_near, tile, tile), F32),
        compiler_params=_params("parallel"),
        name="bias_expand",
    )(by_rel)
    far = jnp.stack([rel_bias[NUM_BUCKETS // 2 - 1], rel_bias[NUM_BUCKETS - 1]], axis=1)
    return near, far


def kernel(x_prompt, x_sample, rel_bias, ln_mix_pre, ln_mix_post, w_in, lam_q, lam_k, head_norm, w_pool,
           pool_scale, w_out, ln_ffn_pre, ln_ffn_post, w_up, conv_w, conv_b, w_down):
    depth = w_in.shape[0]
    w_in_b, w_out_b, w_pool_b = w_in.astype(BF16), w_out.astype(BF16), w_pool.astype(BF16)
    w_up_b, w_down_b = w_up.astype(BF16), w_down.astype(BF16)
    w_vt_b = jnp.swapaxes(w_in_b[:, :, 2 * ATTN_WIDTH:3 * ATTN_WIDTH], 1, 2)
    tables = {}

    def trunk(x):
        b, s, d = x.shape
        tile = min(ATTN_TILE, s)
        tm = min(ROW_TILE, s)
        assert s % tile == 0 and s % tm == 0 and tm % HALO == 0
        if tile not in tables:
            tables[tile] = _bias_tables(rel_bias, tile)
        bias_near_t, far = tables[tile]
        xf = x.reshape(b * s, d)
        for i in range(depth):
            lam_init = 0.8 - 0.6 * math.exp(-0.3 * i)
            q, k, vt, p = _inproj(xf, ln_mix_pre[i][None], w_in_b[i], w_vt_b[i], tile)
            a = _attention(q, k, vt, bias_near_t, far, lam_q[i], lam_k[i], head_norm[i][None], lam_init,
                           tile, s)
            xf = _mixout(a, p, xf, w_out_b[i], w_pool_b[i], pool_scale[i][None], ln_mix_post[i][None], s, tm)
            xf = _ffn(xf, ln_ffn_pre[i][None], w_up_b[i], conv_w[i], conv_b[i][None], w_down_b[i],
                      ln_ffn_post[i][None], s, tm)
        return xf.reshape(b, s, d)

    return (trunk(x_prompt), trunk(x_sample))
```

```python
import functools
import math

import jax
import jax.numpy as jnp
from jax import lax
from jax.experimental import pallas as pl
from jax.experimental.pallas import tpu as pltpu

D_MODEL = 1024
ATTN_WIDTH = 512
POOL_WIDTH = D_MODEL - ATTN_WIDTH
N_DIFF_HEADS = 4
DIFF_HEAD_DIM = 64
V_HEAD_DIM = 2 * DIFF_HEAD_DIM
POOL_WINDOWS = (2, 4, 8, 16)
POOL_GROUP = POOL_WIDTH // len(POOL_WINDOWS)
D_FF = 2816
NUM_BUCKETS = 32
MAX_DISTANCE = 128
RMS_EPS = 1e-6
QK_SCALE = DIFF_HEAD_DIM ** -0.5
LOG2E = math.log2(math.e)

LANES = 128
SUBLANES = 8
MXU_WIDTH = 256
ATTN_TILE = 512
ROW_TILE = 512
FF_CHUNK = D_FF // 2
HALO = SUBLANES
ONES_ROWS = 16
VT_ROWS = V_HEAD_DIM + ONES_ROWS
NEAR_REACH = 2
MASK_VALUE = -1e30
VMEM_LIMIT_BYTES = 48 * 1024 * 1024

F32 = jnp.float32
BF16 = jnp.bfloat16
NT_DIMS = (((1,), (1,)), ((), ()))


def _rms(x, g):
    var = jnp.mean(x * x, axis=-1, keepdims=True)
    return x * lax.rsqrt(var + RMS_EPS) * g


def _params(*semantics):
    return pltpu.CompilerParams(dimension_semantics=semantics, vmem_limit_bytes=VMEM_LIMIT_BYTES)


def _inproj_kernel(x_ref, g_ref, w_ref, wvt_ref, q_ref, k_ref, vt_ref, p_ref):
    h = _rms(x_ref[...], g_ref[...]).astype(BF16)
    a = ATTN_WIDTH
    q = jnp.dot(h, w_ref[:, 0:a], preferred_element_type=F32)
    q_ref[...] = (q * (QK_SCALE * LOG2E)).astype(BF16)
    k_ref[...] = jnp.dot(h, w_ref[:, a:2 * a], preferred_element_type=F32).astype(BF16)
    p_ref[...] = jnp.dot(h, w_ref[:, 3 * a:], preferred_element_type=F32)
    vt = lax.dot_general(wvt_ref[...], h, NT_DIMS, preferred_element_type=F32).astype(BF16)
    ones = jnp.ones((ONES_ROWS, vt.shape[1]), BF16)
    for hd in range(N_DIFF_HEADS):
        r0 = hd * VT_ROWS
        vt_ref[0, r0:r0 + V_HEAD_DIM, :] = vt[hd * V_HEAD_DIM:(hd + 1) * V_HEAD_DIM]
        vt_ref[0, r0 + V_HEAD_DIM:r0 + VT_ROWS, :] = ones


def _inproj(x, g, w_in, w_vt, tm):
    t = x.shape[0]
    row = lambda i: (i, 0)
    const = lambda i: (0, 0)
    return pl.pallas_call(
        _inproj_kernel,
        grid=(t // tm,),
        in_specs=[
            pl.BlockSpec((tm, D_MODEL), row),
            pl.BlockSpec((1, D_MODEL), const),
            pl.BlockSpec(w_in.shape, const),
            pl.BlockSpec(w_vt.shape, const),
        ],
        out_specs=[
            pl.BlockSpec((tm, ATTN_WIDTH), row),
            pl.BlockSpec((tm, ATTN_WIDTH), row),
            pl.BlockSpec((1, N_DIFF_HEADS * VT_ROWS, tm), lambda i: (i, 0, 0)),
            pl.BlockSpec((tm, POOL_WIDTH), row),
        ],
        out_shape=[
            jax.ShapeDtypeStruct((t, ATTN_WIDTH), BF16),
            jax.ShapeDtypeStruct((t, ATTN_WIDTH), BF16),
            jax.ShapeDtypeStruct((t // tm, N_DIFF_HEADS * VT_ROWS, tm), BF16),
            jax.ShapeDtypeStruct((t, POOL_WIDTH), F32),
        ],
        compiler_params=_params("parallel"),
        name="inproj",
    )(x, g, w_in, w_vt)


def _attn_kernel(far_ref, q_ref, qn_ref, k_ref, vt_ref, bias_ref, lq_ref, lk_ref, hn_ref, o_ref,
                 qs_scr, sa_scr, sb_scr, m_scr, acc_scr, *, tile, nk, lam_init):
    h = pl.program_id(1)
    i = pl.program_id(2)
    t = tile
    cq = min(MXU_WIDTH, t)
    n_chunks = 2 * t // cq

    cur = lax.rem(i, 2)
    nxt = 1 - cur

    def stack_maps(slot, q):
        lane = lax.broadcasted_iota(jnp.int32, (t, LANES), 1)
        zero = jnp.zeros_like(q)
        qs_scr[slot, 0:t, :] = jnp.where(lane < DIFF_HEAD_DIM, q, zero)
        qs_scr[slot, t:2 * t, :] = jnp.where(lane >= DIFF_HEAD_DIM, q, zero)

    def scores(j, slot):
        start = pl.multiple_of(j * t, t)
        return lax.dot_general(k_ref[pl.ds(start, t), :], qs_scr[slot], NT_DIMS,
                               preferred_element_type=F32)

    @pl.when(i == 0)
    def _():
        stack_maps(0, q_ref[...])
        sa_scr[...] = scores(0, 0)

    stack_maps(nxt, qn_ref[...])
    m_scr[...] = jnp.full(m_scr.shape, MASK_VALUE, F32)
    acc_scr[...] = jnp.zeros(acc_scr.shape, F32)

    def softmax_pv(j, s_ref, near, c):
        vtj = vt_ref[j]
        for ci in range(n_chunks):
            cols = slice(ci * cq, (ci + 1) * cq)
            s = s_ref[:, cols]
            if near:
                q0 = (ci * cq) % t
                s = s + bias_ref[0, j - i + NEAR_REACH, :, q0:q0 + cq]
            m_prev = m_scr[:, cols]
            m_cur = jnp.max(s, axis=0, keepdims=True)
            if c is not None:
                m_cur = m_cur + c
            m_next = jnp.maximum(m_prev, m_cur)
            p = jnp.exp2(s - (m_next if c is None else m_next - c))
            alpha = jnp.exp2(m_prev - m_next)
            acc_scr[:, cols] = alpha * acc_scr[:, cols] + jnp.dot(vtj, p.astype(BF16),
                                                                  preferred_element_type=F32)
            m_scr[:, cols] = m_next

    def pair(m, near, c):
        j0 = 2 * m
        last = m == nk // 2 - 1
        sb_scr[...] = scores(j0 + 1, cur)
        softmax_pv(j0, sa_scr, near, c)
        sa_scr[...] = scores(jnp.where(last, 0, j0 + 2), jnp.where(last, nxt, cur))
        softmax_pv(j0 + 1, sb_scr, near, c)

    c_left = far_ref[h, 0]
    c_right = far_ref[h, 1]

    def left_body(m, carry):
        pair(m, False, c_left)
        return carry

    def right_body(m, carry):
        pair(m, False, c_right)
        return carry

    first_near = lax.div(i + 1, 2) - 1
    lax.fori_loop(0, jnp.maximum(first_near, 0), left_body, 0)
    for m in (first_near, first_near + 1):

        @pl.when(jnp.logical_and(m >= 0, m < nk // 2))
        def _():
            pair(m, True, None)

    lax.fori_loop(first_near + 2, nk // 2, right_body, 0)

    o = acc_scr[0:V_HEAD_DIM, :] / acc_scr[V_HEAD_DIM:V_HEAD_DIM + 1, :]
    prod = lq_ref[...] * lk_ref[...]
    lam = (jnp.exp(jnp.sum(prod[0:1], axis=1, keepdims=True))
           - jnp.exp(jnp.sum(prod[1:2], axis=1, keepdims=True)) + lam_init)
    a = (o[:, 0:t] - lam * o[:, t:2 * t]).T
    o_ref[...] = (_rms(a, hn_ref[...]) * (1.0 - lam_init)).astype(o_ref.dtype)


def _attention(q, k, vt, bias_near_t, far, lam_q, lam_k, head_norm, lam_init, tile, seq_len):
    t_all = q.shape[0]
    b = t_all // seq_len
    nk = seq_len // tile
    assert nk % 2 == 0, "key tiles are processed in pairs"
    kern = functools.partial(_attn_kernel, tile=tile, nk=nk, lam_init=lam_init)
    return pl.pallas_call(
        kern,
        grid=(b, N_DIFF_HEADS, nk),
        in_specs=[
            pl.BlockSpec(memory_space=pltpu.SMEM),
            pl.BlockSpec((tile, V_HEAD_DIM), lambda bi, h, i: (bi * nk + i, h)),
            pl.BlockSpec((tile, V_HEAD_DIM), lambda bi, h, i: (bi * nk + jnp.minimum(i + 1, nk - 1), h)),
            pl.BlockSpec((seq_len, V_HEAD_DIM), lambda bi, h, i: (bi, h)),
            pl.BlockSpec((nk, VT_ROWS, tile), lambda bi, h, i: (bi, h, 0)),
            pl.BlockSpec((1, 2 * NEAR_REACH + 1, tile, tile), lambda bi, h, i: (h, 0, 0, 0)),
            pl.BlockSpec((2, DIFF_HEAD_DIM), lambda bi, h, i: (0, 0)),
            pl.BlockSpec((2, DIFF_HEAD_DIM), lambda bi, h, i: (0, 0)),
            pl.BlockSpec((1, V_HEAD_DIM), lambda bi, h, i: (0, 0)),
        ],
        out_specs=pl.BlockSpec((tile, V_HEAD_DIM), lambda bi, h, i: (bi * nk + i, h)),
        out_shape=jax.ShapeDtypeStruct((t_all, ATTN_WIDTH), BF16),
        scratch_shapes=[
            pltpu.VMEM((2, 2 * tile, V_HEAD_DIM), BF16),
            pltpu.VMEM((tile, 2 * tile), F32),
            pltpu.VMEM((tile, 2 * tile), F32),
            pltpu.VMEM((1, 2 * tile), F32),
            pltpu.VMEM((VT_ROWS, 2 * tile), F32),
        ],
        compiler_params=_params("parallel", "parallel", "arbitrary"),
        name="diff_attention",
    )(far, q, q, k, vt, bias_near_t, lam_q, lam_k, head_norm)


def _mixout_kernel(a_ref, p_ref, pprev_ref, pnext_ref, x_ref, wout_ref, wpool_ref, ps_ref, g_ref,
                   o_ref, pbuf, *, tm, tiles_per_seq, seq_len):
    ti = lax.rem(pl.program_id(0), tiles_per_seq)
    keep_prev = (ti > 0).astype(F32)
    keep_next = (ti < tiles_per_seq - 1).astype(F32)
    pbuf[0:HALO, :] = pprev_ref[...] * keep_prev
    pbuf[HALO:HALO + tm, :] = p_ref[...]
    pbuf[HALO + tm:, :] = pnext_ref[...] * keep_next

    pos = ti * tm + lax.broadcasted_iota(jnp.int32, (tm, 1), 0)
    mixed = []
    for g, win in enumerate(POOL_WINDOWS):
        lo_off = -(win // 2)
        hi_off = win - win // 2 - 1
        cols = slice(g * POOL_GROUP, (g + 1) * POOL_GROUP)
        total = pbuf[HALO + lo_off:HALO + lo_off + tm, cols]
        for off in range(lo_off + 1, hi_off + 1):
            total = total + pbuf[HALO + off:HALO + off + tm, cols]
        lo = jnp.maximum(pos + lo_off, 0)
        hi = jnp.minimum(pos + hi_off, seq_len - 1)
        cnt = (hi - lo + 1).astype(F32)
        y = total / cnt - pbuf[HALO:HALO + tm, cols]
        mixed.append(jnp.dot(y.astype(BF16), wpool_ref[g], preferred_element_type=F32))
    m = jnp.concatenate(mixed, axis=1) * ps_ref[...]

    o = jnp.dot(a_ref[...], wout_ref[0:ATTN_WIDTH, :], preferred_element_type=F32)
    o = o + jnp.dot(m.astype(BF16), wout_ref[ATTN_WIDTH:, :], preferred_element_type=F32)
    o_ref[...] = x_ref[...] + _rms(o, g_ref[...])


def _halo_specs(tm, width, n_rows):
    per_tile = tm // HALO
    last = n_rows // HALO - 1
    prev = pl.BlockSpec((HALO, width), lambda i, *_: (jnp.maximum(i * per_tile - 1, 0), 0))
    nxt = pl.BlockSpec((HALO, width), lambda i, *_: (jnp.minimum((i + 1) * per_tile, last), 0))
    return prev, nxt


def _mixout(a, p, x, w_out, w_pool, pool_scale, g, seq_len, tm):
    t = x.shape[0]
    row = lambda i: (i, 0)
    const = lambda i: (0, 0)
    prev, nxt = _halo_specs(tm, POOL_WIDTH, t)
    kern = functools.partial(_mixout_kernel, tm=tm, tiles_per_seq=seq_len // tm, seq_len=seq_len)
    return pl.pallas_call(
        kern,
        grid=(t // tm,),
        in_specs=[
            pl.BlockSpec((tm, ATTN_WIDTH), row),
            pl.BlockSpec((tm, POOL_WIDTH), row),
            prev,
            nxt,
            pl.BlockSpec((tm, D_MODEL), row),
            pl.BlockSpec(w_out.shape, const),
            pl.BlockSpec(w_pool.shape, lambda i: (0, 0, 0)),
            pl.BlockSpec((1, POOL_WIDTH), const),
            pl.BlockSpec((1, D_MODEL), const),
        ],
        out_specs=pl.BlockSpec((tm, D_MODEL), row),
        out_shape=jax.ShapeDtypeStruct((t, D_MODEL), F32),
        scratch_shapes=[pltpu.VMEM((tm + 2 * HALO, POOL_WIDTH), F32)],
        compiler_params=_params("parallel"),
        name="pool_outproj",
    )(a, p, p, p, x, w_out, w_pool, pool_scale, g)


def _gelu_tanh(x):
    return 0.5 * x * (1.0 + jnp.tanh(math.sqrt(2.0 / math.pi) * (x + 0.044715 * (x * x * x))))


def _ffn_kernel(x_ref, xprev_ref, xnext_ref, gpre_ref, wg_ref, wv_ref, cwg_ref, cwv_ref, cbg_ref, cbv_ref,
                wd_ref, gpost_ref, o_ref, h_scr, u_scr, acc_scr, *, tm, tiles_per_seq):
    c = pl.program_id(1)

    @pl.when(c == 0)
    def _():
        ti = lax.rem(pl.program_id(0), tiles_per_seq)
        keep_prev = (ti > 0).astype(F32)
        keep_next = (ti < tiles_per_seq - 1).astype(F32)
        g = gpre_ref[...]
        h_scr[0:HALO, :] = _rms(xprev_ref[...], g) * keep_prev
        h_scr[HALO:HALO + tm, :] = _rms(x_ref[...], g)
        h_scr[HALO + tm:, :] = _rms(xnext_ref[...], g) * keep_next
        acc_scr[...] = jnp.zeros(acc_scr.shape, F32)

    h = h_scr[...].astype(BF16)

    def conv(w_ref, cw_ref, cb_ref):
        u_scr[...] = jnp.dot(h, w_ref[...], preferred_element_type=F32)
        cw = cw_ref[...]
        return (u_scr[HALO - 1:HALO - 1 + tm, :] * cw[0:1]
                + u_scr[HALO:HALO + tm, :] * cw[1:2]
                + u_scr[HALO + 1:HALO + 1 + tm, :] * cw[2:3]
                + cb_ref[...])

    gate = conv(wg_ref, cwg_ref, cbg_ref)
    val = conv(wv_ref, cwv_ref, cbv_ref)
    f = (_gelu_tanh(gate) * val).astype(BF16)
    acc_scr[...] += jnp.dot(f, wd_ref[...], preferred_element_type=F32)

    @pl.when(c == pl.num_programs(1) - 1)
    def _():
        o_ref[...] = x_ref[...] + _rms(acc_scr[...], gpost_ref[...])


def _ffn(x, g_pre, w_up, conv_w, conv_b, w_down, g_post, seq_len, tm):
    t = x.shape[0]
    n_chunks = D_FF // FF_CHUNK
    row = lambda i, c: (i, 0)
    const = lambda i, c: (0, 0)
    gate_cols = lambda i, c: (0, c)
    val_cols = lambda i, c: (0, n_chunks + c)
    prev, nxt = _halo_specs(tm, D_MODEL, t)
    kern = functools.partial(_ffn_kernel, tm=tm, tiles_per_seq=seq_len // tm)
    return pl.pallas_call(
        kern,
        grid=(t // tm, n_chunks),
        in_specs=[
            pl.BlockSpec((tm, D_MODEL), row),
            prev,
            nxt,
            pl.BlockSpec((1, D_MODEL), const),
            pl.BlockSpec((D_MODEL, FF_CHUNK), gate_cols),
            pl.BlockSpec((D_MODEL, FF_CHUNK), val_cols),
            pl.BlockSpec((3, FF_CHUNK), gate_cols),
            pl.BlockSpec((3, FF_CHUNK), val_cols),
            pl.BlockSpec((1, FF_CHUNK), gate_cols),
            pl.BlockSpec((1, FF_CHUNK), val_cols),
            pl.BlockSpec((FF_CHUNK, D_MODEL), lambda i, c: (c, 0)),
            pl.BlockSpec((1, D_MODEL), const),
        ],
        out_specs=pl.BlockSpec((tm, D_MODEL), row),
        out_shape=jax.ShapeDtypeStruct((t, D_MODEL), F32),
        scratch_shapes=[
            pltpu.VMEM((tm + 2 * HALO, D_MODEL), F32),
            pltpu.VMEM((tm + 2 * HALO, FF_CHUNK), F32),
            pltpu.VMEM((tm, D_MODEL), F32),
        ],
        compiler_params=_params("parallel", "arbitrary"),
        name="conv_mlp",
    )(x, x, x, g_pre, w_up, w_up, conv_w, conv_w, conv_b, conv_b, w_down, g_post)


def _rel_bucket(rel):
    nb = NUM_BUCKETS // 2
    max_exact = nb // 2
    ret = jnp.where(rel > 0, nb, 0)
    n = jnp.abs(rel)
    nf = jnp.maximum(n, 1).astype(F32)
    large = max_exact + (jnp.log(nf / max_exact) / math.log(MAX_DISTANCE / max_exact)
                         * (nb - max_exact)).astype(jnp.int32)
    large = jnp.minimum(large, nb - 1)
    return ret + jnp.where(n < max_exact, n, large)


def _bias_expand_kernel(y_ref, o_ref, *, tile):
    span = y_ref.shape[-1]
    table = jnp.broadcast_to(y_ref[0], (tile, span))
    skewed = pltpu.roll(table, 0, 1, stride=1, stride_axis=0)
    for di, d in enumerate(range(-NEAR_REACH, NEAR_REACH + 1)):
        start = (-d * tile) % span
        o_ref[0, di] = skewed[:, start:start + tile]


def _bias_tables(rel_bias, tile):
    assert tile >= MAX_DISTANCE, "far tiles must lie beyond the last distance bucket"
    n_near = 2 * NEAR_REACH + 1
    span = 2 * (NEAR_REACH + 1) * tile
    m = jnp.arange(span, dtype=jnp.int32)
    rel = jnp.where(m < span // 2, -m, span - m)
    rel_bias = rel_bias.astype(F32) * LOG2E
    by_rel = rel_bias[_rel_bucket(rel)].T[:, None, :]
    near = pl.pallas_call(
        functools.partial(_bias_expand_kernel, tile=tile),
        grid=(N_DIFF_HEADS,),
        in_specs=[pl.BlockSpec((1, 1, span), lambda h: (h, 0, 0))],
        out_specs=pl.BlockSpec((1, n_near, tile, tile), lambda h: (h, 0, 0, 0)),
        out_shape=jax.ShapeDtypeStruct((N_DIFF_HEADS, n_near, tile, tile), F32),
        compiler_params=_params("parallel"),
        name="bias_expand",
    )(by_rel)
    far = jnp.stack([rel_bias[NUM_BUCKETS // 2 - 1], rel_bias[NUM_BUCKETS - 1]], axis=1)
    return near, far


def kernel(x_prompt, x_sample, rel_bias, ln_mix_pre, ln_mix_post, w_in, lam_q, lam_k, head_norm, w_pool,
           pool_scale, w_out, ln_ffn_pre, ln_ffn_post, w_up, conv_w, conv_b, w_down):
    depth = w_in.shape[0]
    w_in_b, w_out_b, w_pool_b = w_in.astype(BF16), w_out.astype(BF16), w_pool.astype(BF16)
    w_up_b, w_down_b = w_up.astype(BF16), w_down.astype(BF16)
    w_vt_b = jnp.swapaxes(w_in_b[:, :, 2 * ATTN_WIDTH:3 * ATTN_WIDTH], 1, 2)
    tables = {}

    def trunk(x):
        b, s, d = x.shape
        tile = min(ATTN_TILE, s)
        tm = min(ROW_TILE, s)
        assert s % tile == 0 and s % tm == 0 and tm % HALO == 0
        if tile not in tables:
            tables[tile] = _bias_tables(rel_bias, tile)
        bias_near_t, far = tables[tile]
        xf = x.reshape(b * s, d)
        for i in range(depth):
            lam_init = 0.8 - 0.6 * math.exp(-0.3 * i)
            q, k, vt, p = _inproj(xf, ln_mix_pre[i][None], w_in_b[i], w_vt_b[i], tile)
            a = _attention(q, k, vt, bias_near_t, far, lam_q[i], lam_k[i], head_norm[i][None], lam_init,
                           tile, s)
            xf = _mixout(a, p, xf, w_out_b[i], w_pool_b[i], pool_scale[i][None], ln_mix_post[i][None], s, tm)
            xf = _ffn(xf, ln_ffn_pre[i][None], w_up_b[i], conv_w[i], conv_b[i][None], w_down_b[i],
                      ln_ffn_post[i][None], s, tm)
        return xf.reshape(b, s, d)

    return (trunk(x_prompt), trunk(x_sample))
```

```python
import functools
import math

import jax
import jax.numpy as jnp
from jax import lax
from jax.experimental import pallas as pl
from jax.experimental.pallas import tpu as pltpu

D_MODEL = 1024
ATTN_WIDTH = 512
POOL_WIDTH = D_MODEL - ATTN_WIDTH
N_DIFF_HEADS = 4
DIFF_HEAD_DIM = 64
V_HEAD_DIM = 2 * DIFF_HEAD_DIM
POOL_WINDOWS = (2, 4, 8, 16)
POOL_GROUP = POOL_WIDTH // len(POOL_WINDOWS)
D_FF = 2816
NUM_BUCKETS = 32
MAX_DISTANCE = 128
RMS_EPS = 1e-6
QK_SCALE = DIFF_HEAD_DIM ** -0.5
LOG2E = math.log2(math.e)

LANES = 128
SUBLANES = 8
MXU_WIDTH = 256
ATTN_TILE = 512
ROW_TILE = 512
FF_CHUNK = D_FF // 2
HALO = SUBLANES
NEAR_REACH = 2
MASK_VALUE = -1e30
VMEM_LIMIT_BYTES = 48 * 1024 * 1024

F32 = jnp.float32
BF16 = jnp.bfloat16
NT_DIMS = (((1,), (1,)), ((), ()))


def _rms(x, g):
    var = jnp.mean(x * x, axis=-1, keepdims=True)
    return x * lax.rsqrt(var + RMS_EPS) * g


def _params(*semantics):
    return pltpu.CompilerParams(dimension_semantics=semantics, vmem_limit_bytes=VMEM_LIMIT_BYTES)


def _inproj_kernel(x_ref, g_ref, w_ref, wvt_ref, q_ref, k_ref, vt_ref, p_ref):
    h = _rms(x_ref[...], g_ref[...]).astype(BF16)
    a = ATTN_WIDTH
    q = jnp.dot(h, w_ref[:, 0:a], preferred_element_type=F32)
    q_ref[...] = (q * (QK_SCALE * LOG2E)).astype(BF16)
    k_ref[...] = jnp.dot(h, w_ref[:, a:2 * a], preferred_element_type=F32).astype(BF16)
    p_ref[...] = jnp.dot(h, w_ref[:, 3 * a:], preferred_element_type=F32)
    vt_ref[0] = lax.dot_general(wvt_ref[...], h, NT_DIMS, preferred_element_type=F32).astype(BF16)


def _inproj(x, g, w_in, w_vt, tm):
    t = x.shape[0]
    row = lambda i: (i, 0)
    const = lambda i: (0, 0)
    return pl.pallas_call(
        _inproj_kernel,
        grid=(t // tm,),
        in_specs=[
            pl.BlockSpec((tm, D_MODEL), row),
            pl.BlockSpec((1, D_MODEL), const),
            pl.BlockSpec(w_in.shape, const),
            pl.BlockSpec(w_vt.shape, const),
        ],
        out_specs=[
            pl.BlockSpec((tm, ATTN_WIDTH), row),
            pl.BlockSpec((tm, ATTN_WIDTH), row),
            pl.BlockSpec((1, ATTN_WIDTH, tm), lambda i: (i, 0, 0)),
            pl.BlockSpec((tm, POOL_WIDTH), row),
        ],
        out_shape=[
            jax.ShapeDtypeStruct((t, ATTN_WIDTH), BF16),
            jax.ShapeDtypeStruct((t, ATTN_WIDTH), BF16),
            jax.ShapeDtypeStruct((t // tm, ATTN_WIDTH, tm), BF16),
            jax.ShapeDtypeStruct((t, POOL_WIDTH), F32),
        ],
        compiler_params=_params("parallel"),
        name="inproj",
    )(x, g, w_in, w_vt)


def _attn_kernel(far_ref, q_ref, qn_ref, k_ref, vt_ref, bias_ref, lq_ref, lk_ref, hn_ref, o_ref,
                 qs_scr, sa_scr, sb_scr, m_scr, l_scr, acc_scr, *, tile, nk, lam_init):
    h = pl.program_id(1)
    i = pl.program_id(2)
    t = tile
    cq = min(MXU_WIDTH, t)
    n_chunks = 2 * t // cq

    cur = lax.rem(i, 2)
    nxt = 1 - cur

    def stack_maps(slot, q):
        lane = lax.broadcasted_iota(jnp.int32, (t, LANES), 1)
        zero = jnp.zeros_like(q)
        qs_scr[slot, 0:t, :] = jnp.where(lane < DIFF_HEAD_DIM, q, zero)
        qs_scr[slot, t:2 * t, :] = jnp.where(lane >= DIFF_HEAD_DIM, q, zero)

    def scores(j, slot):
        start = pl.multiple_of(j * t, t)
        return lax.dot_general(k_ref[pl.ds(start, t), :], qs_scr[slot], NT_DIMS,
                               preferred_element_type=F32)

    @pl.when(i == 0)
    def _():
        stack_maps(0, q_ref[...])
        sa_scr[...] = scores(0, 0)

    stack_maps(nxt, qn_ref[...])
    m_scr[...] = jnp.full(m_scr.shape, MASK_VALUE, F32)
    l_scr[...] = jnp.zeros(l_scr.shape, F32)
    acc_scr[...] = jnp.zeros(acc_scr.shape, F32)

    def softmax_pv(j, s_ref, near, c):
        vtj = vt_ref[j]
        for ci in range(n_chunks):
            cols = slice(ci * cq, (ci + 1) * cq)
            s = s_ref[:, cols]
            if near:
                q0 = (ci * cq) % t
                s = s + bias_ref[0, j - i + NEAR_REACH, :, q0:q0 + cq]
            m_prev = m_scr[:, cols]
            m_cur = jnp.max(s, axis=0, keepdims=True)
            if c is not None:
                m_cur = m_cur + c
            m_next = jnp.maximum(m_prev, m_cur)
            p = jnp.exp2(s - (m_next if c is None else m_next - c))
            alpha = jnp.exp2(m_prev - m_next)
            l_scr[:, cols] = alpha * l_scr[:, cols] + jnp.sum(p, axis=0, keepdims=True)
            acc_scr[:, cols] = alpha * acc_scr[:, cols] + jnp.dot(vtj, p.astype(BF16),
                                                                  preferred_element_type=F32)
            m_scr[:, cols] = m_next

    def pair(m, near, c):
        j0 = 2 * m
        last = m == nk // 2 - 1
        sb_scr[...] = scores(j0 + 1, cur)
        softmax_pv(j0, sa_scr, near, c)
        sa_scr[...] = scores(jnp.where(last, 0, j0 + 2), jnp.where(last, nxt, cur))
        softmax_pv(j0 + 1, sb_scr, near, c)

    c_left = far_ref[h, 0]
    c_right = far_ref[h, 1]

    def left_body(m, carry):
        pair(m, False, c_left)
        return carry

    def right_body(m, carry):
        pair(m, False, c_right)
        return carry

    first_near = lax.div(i + 1, 2) - 1
    lax.fori_loop(0, jnp.maximum(first_near, 0), left_body, 0)
    for m in (first_near, first_near + 1):

        @pl.when(jnp.logical_and(m >= 0, m < nk // 2))
        def _():
            pair(m, True, None)

    lax.fori_loop(first_near + 2, nk // 2, right_body, 0)

    o = acc_scr[...] / l_scr[...]
    prod = lq_ref[...] * lk_ref[...]
    lam = (jnp.exp(jnp.sum(prod[0:1], axis=1, keepdims=True))
           - jnp.exp(jnp.sum(prod[1:2], axis=1, keepdims=True)) + lam_init)
    a = (o[:, 0:t] - lam * o[:, t:2 * t]).T
    o_ref[...] = (_rms(a, hn_ref[...]) * (1.0 - lam_init)).astype(o_ref.dtype)


def _attention(q, k, vt, bias_near_t, far, lam_q, lam_k, head_norm, lam_init, tile, seq_len):
    t_all = q.shape[0]
    b = t_all // seq_len
    nk = seq_len // tile
    assert nk % 2 == 0, "key tiles are processed in pairs"
    kern = functools.partial(_attn_kernel, tile=tile, nk=nk, lam_init=lam_init)
    return pl.pallas_call(
        kern,
        grid=(b, N_DIFF_HEADS, nk),
        in_specs=[
            pl.BlockSpec(memory_space=pltpu.SMEM),
            pl.BlockSpec((tile, V_HEAD_DIM), lambda bi, h, i: (bi * nk + i, h)),
            pl.BlockSpec((tile, V_HEAD_DIM), lambda bi, h, i: (bi * nk + jnp.minimum(i + 1, nk - 1), h)),
            pl.BlockSpec((seq_len, V_HEAD_DIM), lambda bi, h, i: (bi, h)),
            pl.BlockSpec((nk, V_HEAD_DIM, tile), lambda bi, h, i: (bi, h, 0)),
            pl.BlockSpec((1, 2 * NEAR_REACH + 1, tile, tile), lambda bi, h, i: (h, 0, 0, 0)),
            pl.BlockSpec((2, DIFF_HEAD_DIM), lambda bi, h, i: (0, 0)),
            pl.BlockSpec((2, DIFF_HEAD_DIM), lambda bi, h, i: (0, 0)),
            pl.BlockSpec((1, V_HEAD_DIM), lambda bi, h, i: (0, 0)),
        ],
        out_specs=pl.BlockSpec((tile, V_HEAD_DIM), lambda bi, h, i: (bi * nk + i, h)),
        out_shape=jax.ShapeDtypeStruct((t_all, ATTN_WIDTH), BF16),
        scratch_shapes=[
            pltpu.VMEM((2, 2 * tile, V_HEAD_DIM), BF16),
            pltpu.VMEM((tile, 2 * tile), F32),
            pltpu.VMEM((tile, 2 * tile), F32),
            pltpu.VMEM((1, 2 * tile), F32),
            pltpu.VMEM((1, 2 * tile), F32),
            pltpu.VMEM((V_HEAD_DIM, 2 * tile), F32),
        ],
        compiler_params=_params("parallel", "parallel", "arbitrary"),
        name="diff_attention",
    )(far, q, q, k, vt, bias_near_t, lam_q, lam_k, head_norm)


def _mixout_kernel(a_ref, p_ref, pprev_ref, pnext_ref, x_ref, wout_ref, wpool_ref, ps_ref, g_ref,
                   o_ref, pbuf, *, tm, tiles_per_seq, seq_len):
    ti = lax.rem(pl.program_id(0), tiles_per_seq)
    keep_prev = (ti > 0).astype(F32)
    keep_next = (ti < tiles_per_seq - 1).astype(F32)
    pbuf[0:HALO, :] = pprev_ref[...] * keep_prev
    pbuf[HALO:HALO + tm, :] = p_ref[...]
    pbuf[HALO + tm:, :] = pnext_ref[...] * keep_next

    pos = ti * tm + lax.broadcasted_iota(jnp.int32, (tm, 1), 0)
    mixed = []
    for g, win in enumerate(POOL_WINDOWS):
        lo_off = -(win // 2)
        hi_off = win - win // 2 - 1
        cols = slice(g * POOL_GROUP, (g + 1) * POOL_GROUP)
        total = pbuf[HALO + lo_off:HALO + lo_off + tm, cols]
        for off in range(lo_off + 1, hi_off + 1):
            total = total + pbuf[HALO + off:HALO + off + tm, cols]
        lo = jnp.maximum(pos + lo_off, 0)
        hi = jnp.minimum(pos + hi_off, seq_len - 1)
        cnt = (hi - lo + 1).astype(F32)
        y = total / cnt - pbuf[HALO:HALO + tm, cols]
        mixed.append(jnp.dot(y.astype(BF16), wpool_ref[g], preferred_element_type=F32))
    m = jnp.concatenate(mixed, axis=1) * ps_ref[...]

    o = jnp.dot(a_ref[...], wout_ref[0:ATTN_WIDTH, :], preferred_element_type=F32)
    o = o + jnp.dot(m.astype(BF16), wout_ref[ATTN_WIDTH:, :], preferred_element_type=F32)
    o_ref[...] = x_ref[...] + _rms(o, g_ref[...])


def _halo_specs(tm, width, n_rows):
    per_tile = tm // HALO
    last = n_rows // HALO - 1
    prev = pl.BlockSpec((HALO, width), lambda i, *_: (jnp.maximum(i * per_tile - 1, 0), 0))
    nxt = pl.BlockSpec((HALO, width), lambda i, *_: (jnp.minimum((i + 1) * per_tile, last), 0))
    return prev, nxt


def _mixout(a, p, x, w_out, w_pool, pool_scale, g, seq_len, tm):
    t = x.shape[0]
    row = lambda i: (i, 0)
    const = lambda i: (0, 0)
    prev, nxt = _halo_specs(tm, POOL_WIDTH, t)
    kern = functools.partial(_mixout_kernel, tm=tm, tiles_per_seq=seq_len // tm, seq_len=seq_len)
    return pl.pallas_call(
        kern,
        grid=(t // tm,),
        in_specs=[
            pl.BlockSpec((tm, ATTN_WIDTH), row),
            pl.BlockSpec((tm, POOL_WIDTH), row),
            prev,
            nxt,
            pl.BlockSpec((tm, D_MODEL), row),
            pl.BlockSpec(w_out.shape, const),
            pl.BlockSpec(w_pool.shape, lambda i: (0, 0, 0)),
            pl.BlockSpec((1, POOL_WIDTH), const),
            pl.BlockSpec((1, D_MODEL), const),
        ],
        out_specs=pl.BlockSpec((tm, D_MODEL), row),
        out_shape=jax.ShapeDtypeStruct((t, D_MODEL), F32),
        scratch_shapes=[pltpu.VMEM((tm + 2 * HALO, POOL_WIDTH), F32)],
        compiler_params=_params("parallel"),
        name="pool_outproj",
    )(a, p, p, p, x, w_out, w_pool, pool_scale, g)


def _gelu_tanh(x):
    return 0.5 * x * (1.0 + jnp.tanh(math.sqrt(2.0 / math.pi) * (x + 0.044715 * (x * x * x))))


def _ffn_kernel(x_ref, xprev_ref, xnext_ref, gpre_ref, wg_ref, wv_ref, cwg_ref, cwv_ref, cbg_ref, cbv_ref,
                wd_ref, gpost_ref, o_ref, h_scr, u_scr, acc_scr, *, tm, tiles_per_seq):
    c = pl.program_id(1)

    @pl.when(c == 0)
    def _():
        ti = lax.rem(pl.program_id(0), tiles_per_seq)
        keep_prev = (ti > 0).astype(F32)
        keep_next = (ti < tiles_per_seq - 1).astype(F32)
        g = gpre_ref[...]
        h_scr[0:HALO, :] = _rms(xprev_ref[...], g) * keep_prev
        h_scr[HALO:HALO + tm, :] = _rms(x_ref[...], g)
        h_scr[HALO + tm:, :] = _rms(xnext_ref[...], g) * keep_next
        acc_scr[...] = jnp.zeros(acc_scr.shape, F32)

    h = h_scr[...].astype(BF16)

    def conv(w_ref, cw_ref, cb_ref):
        u_scr[...] = jnp.dot(h, w_ref[...], preferred_element_type=F32)
        cw = cw_ref[...]
        return (u_scr[HALO - 1:HALO - 1 + tm, :] * cw[0:1]
                + u_scr[HALO:HALO + tm, :] * cw[1:2]
                + u_scr[HALO + 1:HALO + 1 + tm, :] * cw[2:3]
                + cb_ref[...])

    gate = conv(wg_ref, cwg_ref, cbg_ref)
    val = conv(wv_ref, cwv_ref, cbv_ref)
    f = (_gelu_tanh(gate) * val).astype(BF16)
    acc_scr[...] += jnp.dot(f, wd_ref[...], preferred_element_type=F32)

    @pl.when(c == pl.num_programs(1) - 1)
    def _():
        o_ref[...] = x_ref[...] + _rms(acc_scr[...], gpost_ref[...])


def _ffn(x, g_pre, w_up, conv_w, conv_b, w_down, g_post, seq_len, tm):
    t = x.shape[0]
    n_chunks = D_FF // FF_CHUNK
    row = lambda i, c: (i, 0)
    const = lambda i, c: (0, 0)
    gate_cols = lambda i, c: (0, c)
    val_cols = lambda i, c: (0, n_chunks + c)
    prev, nxt = _halo_specs(tm, D_MODEL, t)
    kern = functools.partial(_ffn_kernel, tm=tm, tiles_per_seq=seq_len // tm)
    return pl.pallas_call(
        kern,
        grid=(t // tm, n_chunks),
        in_specs=[
            pl.BlockSpec((tm, D_MODEL), row),
            prev,
            nxt,
            pl.BlockSpec((1, D_MODEL), const),
            pl.BlockSpec((D_MODEL, FF_CHUNK), gate_cols),
            pl.BlockSpec((D_MODEL, FF_CHUNK), val_cols),
            pl.BlockSpec((3, FF_CHUNK), gate_cols),
            pl.BlockSpec((3, FF_CHUNK), val_cols),
            pl.BlockSpec((1, FF_CHUNK), gate_cols),
            pl.BlockSpec((1, FF_CHUNK), val_cols),
            pl.BlockSpec((FF_CHUNK, D_MODEL), lambda i, c: (c, 0)),
            pl.BlockSpec((1, D_MODEL), const),
        ],
        out_specs=pl.BlockSpec((tm, D_MODEL), row),
        out_shape=jax.ShapeDtypeStruct((t, D_MODEL), F32),
        scratch_shapes=[
            pltpu.VMEM((tm + 2 * HALO, D_MODEL), F32),
            pltpu.VMEM((tm + 2 * HALO, FF_CHUNK), F32),
            pltpu.VMEM((tm, D_MODEL), F32),
        ],
        compiler_params=_params("parallel", "arbitrary"),
        name="conv_mlp",
    )(x, x, x, g_pre, w_up, w_up, conv_w, conv_w, conv_b, conv_b, w_down, g_post)


def _rel_bucket(rel):
    nb = NUM_BUCKETS // 2
    max_exact = nb // 2
    ret = jnp.where(rel > 0, nb, 0)
    n = jnp.abs(rel)
    nf = jnp.maximum(n, 1).astype(F32)
    large = max_exact + (jnp.log(nf / max_exact) / math.log(MAX_DISTANCE / max_exact)
                         * (nb - max_exact)).astype(jnp.int32)
    large = jnp.minimum(large, nb - 1)
    return ret + jnp.where(n < max_exact, n, large)


def _bias_expand_kernel(y_ref, o_ref, *, tile):
    span = y_ref.shape[-1]
    table = jnp.broadcast_to(y_ref[0], (tile, span))
    skewed = pltpu.roll(table, 0, 1, stride=1, stride_axis=0)
    for di, d in enumerate(range(-NEAR_REACH, NEAR_REACH + 1)):
        start = (-d * tile) % span
        o_ref[0, di] = skewed[:, start:start + tile]


def _bias_tables(rel_bias, tile):
    assert tile >= MAX_DISTANCE, "far tiles must lie beyond the last distance bucket"
    n_near = 2 * NEAR_REACH + 1
    span = 2 * (NEAR_REACH + 1) * tile
    m = jnp.arange(span, dtype=jnp.int32)
    rel = jnp.where(m < span // 2, -m, span - m)
    rel_bias = rel_bias.astype(F32) * LOG2E
    by_rel = rel_bias[_rel_bucket(rel)].T[:, None, :]
    near = pl.pallas_call(
        functools.partial(_bias_expand_kernel, tile=tile),
        grid=(N_DIFF_HEADS,),
        in_specs=[pl.BlockSpec((1, 1, span), lambda h: (h, 0, 0))],
        out_specs=pl.BlockSpec((1, n_near, tile, tile), lambda h: (h, 0, 0, 0)),
        out_shape=jax.ShapeDtypeStruct((N_DIFF_HEADS, n_near, tile, tile), F32),
        compiler_params=_params("parallel"),
        name="bias_expand",
    )(by_rel)
    far = jnp.stack([rel_bias[NUM_BUCKETS // 2 - 1], rel_bias[NUM_BUCKETS - 1]], axis=1)
    return near, far


def kernel(x_prompt, x_sample, rel_bias, ln_mix_pre, ln_mix_post, w_in, lam_q, lam_k, head_norm, w_pool,
           pool_scale, w_out, ln_ffn_pre, ln_ffn_post, w_up, conv_w, conv_b, w_down):
    depth = w_in.shape[0]
    w_in_b, w_out_b, w_pool_b = w_in.astype(BF16), w_out.astype(BF16), w_pool.astype(BF16)
    w_up_b, w_down_b = w_up.astype(BF16), w_down.astype(BF16)
    w_vt_b = jnp.swapaxes(w_in_b[:, :, 2 * ATTN_WIDTH:3 * ATTN_WIDTH], 1, 2)
    tables = {}

    def trunk(x):
        b, s, d = x.shape
        tile = min(ATTN_TILE, s)
        tm = min(ROW_TILE, s)
        assert s % tile == 0 and s % tm == 0 and tm % HALO == 0
        if tile not in tables:
            tables[tile] = _bias_tables(rel_bias, tile)
        bias_near_t, far = tables[tile]
        xf = x.reshape(b * s, d)
        for i in range(depth):
            lam_init = 0.8 - 0.6 * math.exp(-0.3 * i)
            q, k, vt, p = _inproj(xf, ln_mix_pre[i][None], w_in_b[i], w_vt_b[i], tile)
            a = _attention(q, k, vt, bias_near_t, far, lam_q[i], lam_k[i], head_norm[i][None], lam_init,
                           tile, s)
            xf = _mixout(a, p, xf, w_out_b[i], w_pool_b[i], pool_scale[i][None], ln_mix_post[i][None], s, tm)
            xf = _ffn(xf, ln_ffn_pre[i][None], w_up_b[i], conv_w[i], conv_b[i][None], w_down_b[i],
                      ln_ffn_post[i][None], s, tm)
        return xf.reshape(b, s, d)

    return (trunk(x_prompt), trunk(x_sample))
```

```python
import functools
import math

import jax
import jax.numpy as jnp
from jax import lax
from jax.experimental import pallas as pl
from jax.experimental.pallas import tpu as pltpu

D_MODEL = 1024
ATTN_WIDTH = 512
POOL_WIDTH = D_MODEL - ATTN_WIDTH
N_DIFF_HEADS = 4
DIFF_HEAD_DIM = 64
V_HEAD_DIM = 2 * DIFF_HEAD_DIM
POOL_WINDOWS = (2, 4, 8, 16)
POOL_GROUP = POOL_WIDTH // len(POOL_WINDOWS)
D_FF = 2816
NUM_BUCKETS = 32
MAX_DISTANCE = 128
RMS_EPS = 1e-6
QK_SCALE = DIFF_HEAD_DIM ** -0.5
LOG2E = math.log2(math.e)

LANES = 128
SUBLANES = 8
MXU_WIDTH = 256
ATTN_TILE = 512
ROW_TILE = 512
FF_CHUNK = D_FF // 2
HALO = SUBLANES
NEAR_REACH = 2
MASK_VALUE = -1e30
VMEM_LIMIT_BYTES = 48 * 1024 * 1024

F32 = jnp.float32
BF16 = jnp.bfloat16
NT_DIMS = (((1,), (1,)), ((), ()))


def _rms(x, g):
    var = jnp.mean(x * x, axis=-1, keepdims=True)
    return x * lax.rsqrt(var + RMS_EPS) * g


def _params(*semantics):
    return pltpu.CompilerParams(dimension_semantics=semantics, vmem_limit_bytes=VMEM_LIMIT_BYTES)


def _inproj_kernel(x_ref, g_ref, w_ref, wvt_ref, q_ref, k_ref, vt_ref, p_ref):
    h = _rms(x_ref[...], g_ref[...]).astype(BF16)
    a = ATTN_WIDTH
    q = jnp.dot(h, w_ref[:, 0:a], preferred_element_type=F32)
    q_ref[...] = (q * (QK_SCALE * LOG2E)).astype(BF16)
    k_ref[...] = jnp.dot(h, w_ref[:, a:2 * a], preferred_element_type=F32).astype(BF16)
    p_ref[...] = jnp.dot(h, w_ref[:, 3 * a:], preferred_element_type=F32)
    vt_ref[0] = lax.dot_general(wvt_ref[...], h, NT_DIMS, preferred_element_type=F32).astype(BF16)


def _inproj(x, g, w_in, w_vt, tm):
    t = x.shape[0]
    row = lambda i: (i, 0)
    const = lambda i: (0, 0)
    return pl.pallas_call(
        _inproj_kernel,
        grid=(t // tm,),
        in_specs=[
            pl.BlockSpec((tm, D_MODEL), row),
            pl.BlockSpec((1, D_MODEL), const),
            pl.BlockSpec(w_in.shape, const),
            pl.BlockSpec(w_vt.shape, const),
        ],
        out_specs=[
            pl.BlockSpec((tm, ATTN_WIDTH), row),
            pl.BlockSpec((tm, ATTN_WIDTH), row),
            pl.BlockSpec((1, ATTN_WIDTH, tm), lambda i: (i, 0, 0)),
            pl.BlockSpec((tm, POOL_WIDTH), row),
        ],
        out_shape=[
            jax.ShapeDtypeStruct((t, ATTN_WIDTH), BF16),
            jax.ShapeDtypeStruct((t, ATTN_WIDTH), BF16),
            jax.ShapeDtypeStruct((t // tm, ATTN_WIDTH, tm), BF16),
            jax.ShapeDtypeStruct((t, POOL_WIDTH), F32),
        ],
        compiler_params=_params("parallel"),
        name="inproj",
    )(x, g, w_in, w_vt)


def _attn_kernel(far_ref, q_ref, qn_ref, k_ref, vt_ref, bias_ref, lq_ref, lk_ref, hn_ref, o_ref,
                 qs_scr, sa_scr, sb_scr, m_scr, l_scr, acc_scr, *, tile, nk, lam_init):
    h = pl.program_id(1)
    i = pl.program_id(2)
    t = tile
    cq = min(MXU_WIDTH, t)
    n_chunks = 2 * t // cq

    cur = lax.rem(i, 2)
    nxt = 1 - cur

    def stack_maps(slot, q):
        lane = lax.broadcasted_iota(jnp.int32, (t, LANES), 1)
        zero = jnp.zeros_like(q)
        qs_scr[slot, 0:t, :] = jnp.where(lane < DIFF_HEAD_DIM, q, zero)
        qs_scr[slot, t:2 * t, :] = jnp.where(lane >= DIFF_HEAD_DIM, q, zero)

    def scores(j, slot):
        start = pl.multiple_of(j * t, t)
        return lax.dot_general(k_ref[pl.ds(start, t), :], qs_scr[slot], NT_DIMS,
                               preferred_element_type=F32)

    @pl.when(i == 0)
    def _():
        stack_maps(0, q_ref[...])
        sa_scr[...] = scores(0, 0)

    stack_maps(nxt, qn_ref[...])
    m_scr[...] = jnp.full(m_scr.shape, MASK_VALUE, F32)
    l_scr[...] = jnp.zeros(l_scr.shape, F32)
    acc_scr[...] = jnp.zeros(acc_scr.shape, F32)

    def softmax_pv(j, s_ref, near, c):
        vtj = vt_ref[j]
        for ci in range(n_chunks):
            cols = slice(ci * cq, (ci + 1) * cq)
            s = s_ref[:, cols]
            if near:
                q0 = (ci * cq) % t
                s = s + bias_ref[0, j - i + NEAR_REACH, :, q0:q0 + cq]
            m_prev = m_scr[:, cols]
            m_cur = jnp.max(s, axis=0, keepdims=True)
            if c is not None:
                m_cur = m_cur + c
            m_next = jnp.maximum(m_prev, m_cur)
            p = jnp.exp2(s - (m_next if c is None else m_next - c))
            alpha = jnp.exp2(m_prev - m_next)
            l_scr[:, cols] = alpha * l_scr[:, cols] + jnp.sum(p, axis=0, keepdims=True)
            acc_scr[:, cols] = alpha * acc_scr[:, cols] + jnp.dot(vtj, p.astype(BF16),
                                                                  preferred_element_type=F32)
            m_scr[:, cols] = m_next

    def pair(m, near, c):
        j0 = 2 * m
        last = m == nk // 2 - 1
        sb_scr[...] = scores(j0 + 1, cur)
        softmax_pv(j0, sa_scr, near, c)
        sa_scr[...] = scores(jnp.where(last, 0, j0 + 2), jnp.where(last, nxt, cur))
        softmax_pv(j0 + 1, sb_scr, near, c)

    c_left = far_ref[h, 0]
    c_right = far_ref[h, 1]

    first_near = lax.div(i + 1, 2) - 1
    n_left = jnp.maximum(first_near, 0)
    right_start = first_near + 2
    n_right = jnp.maximum(nk // 2 - right_start, 0)
    right_odd = lax.rem(n_right, 2)

    def left_body(u, carry):
        pair(2 * u, False, c_left)
        pair(2 * u + 1, False, c_left)
        return carry

    def right_body(u, carry):
        m0 = right_start + right_odd + 2 * u
        pair(m0, False, c_right)
        pair(m0 + 1, False, c_right)
        return carry

    lax.fori_loop(0, lax.div(n_left, 2), left_body, 0)

    @pl.when(lax.rem(n_left, 2) == 1)
    def _():
        pair(n_left - 1, False, c_left)

    for m in (first_near, first_near + 1):

        @pl.when(jnp.logical_and(m >= 0, m < nk // 2))
        def _():
            pair(m, True, None)

    @pl.when(right_odd == 1)
    def _():
        pair(right_start, False, c_right)

    lax.fori_loop(0, lax.div(n_right, 2), right_body, 0)

    o = acc_scr[...] / l_scr[...]
    prod = lq_ref[...] * lk_ref[...]
    lam = (jnp.exp(jnp.sum(prod[0:1], axis=1, keepdims=True))
           - jnp.exp(jnp.sum(prod[1:2], axis=1, keepdims=True)) + lam_init)
    a = (o[:, 0:t] - lam * o[:, t:2 * t]).T
    o_ref[...] = (_rms(a, hn_ref[...]) * (1.0 - lam_init)).astype(o_ref.dtype)


def _attention(q, k, vt, bias_near_t, far, lam_q, lam_k, head_norm, lam_init, tile, seq_len):
    t_all = q.shape[0]
    b = t_all // seq_len
    nk = seq_len // tile
    assert nk % 2 == 0, "key tiles are processed in pairs"
    kern = functools.partial(_attn_kernel, tile=tile, nk=nk, lam_init=lam_init)
    return pl.pallas_call(
        kern,
        grid=(b, N_DIFF_HEADS, nk),
        in_specs=[
            pl.BlockSpec(memory_space=pltpu.SMEM),
            pl.BlockSpec((tile, V_HEAD_DIM), lambda bi, h, i: (bi * nk + i, h)),
            pl.BlockSpec((tile, V_HEAD_DIM), lambda bi, h, i: (bi * nk + jnp.minimum(i + 1, nk - 1), h)),
            pl.BlockSpec((seq_len, V_HEAD_DIM), lambda bi, h, i: (bi, h)),
            pl.BlockSpec((nk, V_HEAD_DIM, tile), lambda bi, h, i: (bi, h, 0)),
            pl.BlockSpec((1, 2 * NEAR_REACH + 1, tile, tile), lambda bi, h, i: (h, 0, 0, 0)),
            pl.BlockSpec((2, DIFF_HEAD_DIM), lambda bi, h, i: (0, 0)),
            pl.BlockSpec((2, DIFF_HEAD_DIM), lambda bi, h, i: (0, 0)),
            pl.BlockSpec((1, V_HEAD_DIM), lambda bi, h, i: (0, 0)),
        ],
        out_specs=pl.BlockSpec((tile, V_HEAD_DIM), lambda bi, h, i: (bi * nk + i, h)),
        out_shape=jax.ShapeDtypeStruct((t_all, ATTN_WIDTH), BF16),
        scratch_shapes=[
            pltpu.VMEM((2, 2 * tile, V_HEAD_DIM), BF16),
            pltpu.VMEM((tile, 2 * tile), F32),
            pltpu.VMEM((tile, 2 * tile), F32),
            pltpu.VMEM((1, 2 * tile), F32),
            pltpu.VMEM((1, 2 * tile), F32),
            pltpu.VMEM((V_HEAD_DIM, 2 * tile), F32),
        ],
        compiler_params=_params("parallel", "parallel", "arbitrary"),
        name="diff_attention",
    )(far, q, q, k, vt, bias_near_t, lam_q, lam_k, head_norm)


def _mixout_kernel(a_ref, p_ref, pprev_ref, pnext_ref, x_ref, wout_ref, wpool_ref, ps_ref, g_ref,
                   o_ref, pbuf, *, tm, tiles_per_seq, seq_len):
    ti = lax.rem(pl.program_id(0), tiles_per_seq)
    keep_prev = (ti > 0).astype(F32)
    keep_next = (ti < tiles_per_seq - 1).astype(F32)
    pbuf[0:HALO, :] = pprev_ref[...] * keep_prev
    pbuf[HALO:HALO + tm, :] = p_ref[...]
    pbuf[HALO + tm:, :] = pnext_ref[...] * keep_next

    pos = ti * tm + lax.broadcasted_iota(jnp.int32, (tm, 1), 0)
    mixed = []
    for g, win in enumerate(POOL_WINDOWS):
        lo_off = -(win // 2)
        hi_off = win - win // 2 - 1
        cols = slice(g * POOL_GROUP, (g + 1) * POOL_GROUP)
        total = pbuf[HALO + lo_off:HALO + lo_off + tm, cols]
        for off in range(lo_off + 1, hi_off + 1):
            total = total + pbuf[HALO + off:HALO + off + tm, cols]
        lo = jnp.maximum(pos + lo_off, 0)
        hi = jnp.minimum(pos + hi_off, seq_len - 1)
        cnt = (hi - lo + 1).astype(F32)
        y = total / cnt - pbuf[HALO:HALO + tm, cols]
        mixed.append(jnp.dot(y.astype(BF16), wpool_ref[g], preferred_element_type=F32))
    m = jnp.concatenate(mixed, axis=1) * ps_ref[...]

    o = jnp.dot(a_ref[...], wout_ref[0:ATTN_WIDTH, :], preferred_element_type=F32)
    o = o + jnp.dot(m.astype(BF16), wout_ref[ATTN_WIDTH:, :], preferred_element_type=F32)
    o_ref[...] = x_ref[...] + _rms(o, g_ref[...])


def _halo_specs(tm, width, n_rows):
    per_tile = tm // HALO
    last = n_rows // HALO - 1
    prev = pl.BlockSpec((HALO, width), lambda i, *_: (jnp.maximum(i * per_tile - 1, 0), 0))
    nxt = pl.BlockSpec((HALO, width), lambda i, *_: (jnp.minimum((i + 1) * per_tile, last), 0))
    return prev, nxt


def _mixout(a, p, x, w_out, w_pool, pool_scale, g, seq_len, tm):
    t = x.shape[0]
    row = lambda i: (i, 0)
    const = lambda i: (0, 0)
    prev, nxt = _halo_specs(tm, POOL_WIDTH, t)
    kern = functools.partial(_mixout_kernel, tm=tm, tiles_per_seq=seq_len // tm, seq_len=seq_len)
    return pl.pallas_call(
        kern,
        grid=(t // tm,),
        in_specs=[
            pl.BlockSpec((tm, ATTN_WIDTH), row),
            pl.BlockSpec((tm, POOL_WIDTH), row),
            prev,
            nxt,
            pl.BlockSpec((tm, D_MODEL), row),
            pl.BlockSpec(w_out.shape, const),
            pl.BlockSpec(w_pool.shape, lambda i: (0, 0, 0)),
            pl.BlockSpec((1, POOL_WIDTH), const),
            pl.BlockSpec((1, D_MODEL), const),
        ],
        out_specs=pl.BlockSpec((tm, D_MODEL), row),
        out_shape=jax.ShapeDtypeStruct((t, D_MODEL), F32),
        scratch_shapes=[pltpu.VMEM((tm + 2 * HALO, POOL_WIDTH), F32)],
        compiler_params=_params("parallel"),
        name="pool_outproj",
    )(a, p, p, p, x, w_out, w_pool, pool_scale, g)


def _gelu_tanh(x):
    return 0.5 * x * (1.0 + jnp.tanh(math.sqrt(2.0 / math.pi) * (x + 0.044715 * (x * x * x))))


def _ffn_kernel(x_ref, xprev_ref, xnext_ref, gpre_ref, wg_ref, wv_ref, cwg_ref, cwv_ref, cbg_ref, cbv_ref,
                wd_ref, gpost_ref, o_ref, h_scr, u_scr, acc_scr, *, tm, tiles_per_seq):
    c = pl.program_id(1)

    @pl.when(c == 0)
    def _():
        ti = lax.rem(pl.program_id(0), tiles_per_seq)
        keep_prev = (ti > 0).astype(F32)
        keep_next = (ti < tiles_per_seq - 1).astype(F32)
        g = gpre_ref[...]
        h_scr[0:HALO, :] = _rms(xprev_ref[...], g) * keep_prev
        h_scr[HALO:HALO + tm, :] = _rms(x_ref[...], g)
        h_scr[HALO + tm:, :] = _rms(xnext_ref[...], g) * keep_next
        acc_scr[...] = jnp.zeros(acc_scr.shape, F32)

    h = h_scr[...].astype(BF16)

    def conv(w_ref, cw_ref, cb_ref):
        u_scr[...] = jnp.dot(h, w_ref[...], preferred_element_type=F32)
        cw = cw_ref[...]
        return (u_scr[HALO - 1:HALO - 1 + tm, :] * cw[0:1]
                + u_scr[HALO:HALO + tm, :] * cw[1:2]
                + u_scr[HALO + 1:HALO + 1 + tm, :] * cw[2:3]
                + cb_ref[...])

    gate = conv(wg_ref, cwg_ref, cbg_ref)
    val = conv(wv_ref, cwv_ref, cbv_ref)
    f = (_gelu_tanh(gate) * val).astype(BF16)
    acc_scr[...] += jnp.dot(f, wd_ref[...], preferred_element_type=F32)

    @pl.when(c == pl.num_programs(1) - 1)
    def _():
        o_ref[...] = x_ref[...] + _rms(acc_scr[...], gpost_ref[...])


def _ffn(x, g_pre, w_up, conv_w, conv_b, w_down, g_post, seq_len, tm):
    t = x.shape[0]
    n_chunks = D_FF // FF_CHUNK
    row = lambda i, c: (i, 0)
    const = lambda i, c: (0, 0)
    gate_cols = lambda i, c: (0, c)
    val_cols = lambda i, c: (0, n_chunks + c)
    prev, nxt = _halo_specs(tm, D_MODEL, t)
    kern = functools.partial(_ffn_kernel, tm=tm, tiles_per_seq=seq_len // tm)
    return pl.pallas_call(
        kern,
        grid=(t // tm, n_chunks),
        in_specs=[
            pl.BlockSpec((tm, D_MODEL), row),
            prev,
            nxt,
            pl.BlockSpec((1, D_MODEL), const),
            pl.BlockSpec((D_MODEL, FF_CHUNK), gate_cols),
            pl.BlockSpec((D_MODEL, FF_CHUNK), val_cols),
            pl.BlockSpec((3, FF_CHUNK), gate_cols),
            pl.BlockSpec((3, FF_CHUNK), val_cols),
            pl.BlockSpec((1, FF_CHUNK), gate_cols),
            pl.BlockSpec((1, FF_CHUNK), val_cols),
            pl.BlockSpec((FF_CHUNK, D_MODEL), lambda i, c: (c, 0)),
            pl.BlockSpec((1, D_MODEL), const),
        ],
        out_specs=pl.BlockSpec((tm, D_MODEL), row),
        out_shape=jax.ShapeDtypeStruct((t, D_MODEL), F32),
        scratch_shapes=[
            pltpu.VMEM((tm + 2 * HALO, D_MODEL), F32),
            pltpu.VMEM((tm + 2 * HALO, FF_CHUNK), F32),
            pltpu.VMEM((tm, D_MODEL), F32),
        ],
        compiler_params=_params("parallel", "arbitrary"),
        name="conv_mlp",
    )(x, x, x, g_pre, w_up, w_up, conv_w, conv_w, conv_b, conv_b, w_down, g_post)


def _rel_bucket(rel):
    nb = NUM_BUCKETS // 2
    max_exact = nb // 2
    ret = jnp.where(rel > 0, nb, 0)
    n = jnp.abs(rel)
    nf = jnp.maximum(n, 1).astype(F32)
    large = max_exact + (jnp.log(nf / max_exact) / math.log(MAX_DISTANCE / max_exact)
                         * (nb - max_exact)).astype(jnp.int32)
    large = jnp.minimum(large, nb - 1)
    return ret + jnp.where(n < max_exact, n, large)


def _bias_expand_kernel(y_ref, o_ref, *, tile):
    span = y_ref.shape[-1]
    table = jnp.broadcast_to(y_ref[0], (tile, span))
    skewed = pltpu.roll(table, 0, 1, stride=1, stride_axis=0)
    for di, d in enumerate(range(-NEAR_REACH, NEAR_REACH + 1)):
        start = (-d * tile) % span
        o_ref[0, di] = skewed[:, start:start + tile]


def _bias_tables(rel_bias, tile):
    assert tile >= MAX_DISTANCE, "far tiles must lie beyond the last distance bucket"
    n_near = 2 * NEAR_REACH + 1
    span = 2 * (NEAR_REACH + 1) * tile
    m = jnp.arange(span, dtype=jnp.int32)
    rel = jnp.where(m < span // 2, -m, span - m)
    rel_bias = rel_bias.astype(F32) * LOG2E
    by_rel = rel_bias[_rel_bucket(rel)].T[:, None, :]
    near = pl.pallas_call(
        functools.partial(_bias_expand_kernel, tile=tile),
        grid=(N_DIFF_HEADS,),
        in_specs=[pl.BlockSpec((1, 1, span), lambda h: (h, 0, 0))],
        out_specs=pl.BlockSpec((1, n_near, tile, tile), lambda h: (h, 0, 0, 0)),
        out_shape=jax.ShapeDtypeStruct((N_DIFF_HEADS, n_near, tile, tile), F32),
        compiler_params=_params("parallel"),
        name="bias_expand",
    )(by_rel)
    far = jnp.stack([rel_bias[NUM_BUCKETS // 2 - 1], rel_bias[NUM_BUCKETS - 1]], axis=1)
    return near, far


def kernel(x_prompt, x_sample, rel_bias, ln_mix_pre, ln_mix_post, w_in, lam_q, lam_k, head_norm, w_pool,
           pool_scale, w_out, ln_ffn_pre, ln_ffn_post, w_up, conv_w, conv_b, w_down):
    depth = w_in.shape[0]
    w_in_b, w_out_b, w_pool_b = w_in.astype(BF16), w_out.astype(BF16), w_pool.astype(BF16)
    w_up_b, w_down_b = w_up.astype(BF16), w_down.astype(BF16)
    w_vt_b = jnp.swapaxes(w_in_b[:, :, 2 * ATTN_WIDTH:3 * ATTN_WIDTH], 1, 2)
    tables = {}

    def trunk(x):
        b, s, d = x.shape
        tile = min(ATTN_TILE, s)
        tm = min(ROW_TILE, s)
        assert s % tile == 0 and s % tm == 0 and tm % HALO == 0
        if tile not in tables:
            tables[tile] = _bias_tables(rel_bias, tile)
        bias_near_t, far = tables[tile]
        xf = x.reshape(b * s, d)
        for i in range(depth):
            lam_init = 0.8 - 0.6 * math.exp(-0.3 * i)
            q, k, vt, p = _inproj(xf, ln_mix_pre[i][None], w_in_b[i], w_vt_b[i], tile)
            a = _attention(q, k, vt, bias_near_t, far, lam_q[i], lam_k[i], head_norm[i][None], lam_init,
                           tile, s)
            xf = _mixout(a, p, xf, w_out_b[i], w_pool_b[i], pool_scale[i][None], ln_mix_post[i][None], s, tm)
            xf = _ffn(xf, ln_ffn_pre[i][None], w_up_b[i], conv_w[i], conv_b[i][None], w_down_b[i],
                      ln_ffn_post[i][None], s, tm)
        return xf.reshape(b, s, d)

    return (trunk(x_prompt), trunk(x_sample))
```

```python
import functools
import math

import jax
import jax.numpy as jnp
from jax import lax
from jax.experimental import pallas as pl
from jax.experimental.pallas import tpu as pltpu

D_MODEL = 1024
ATTN_WIDTH = 512
POOL_WIDTH = D_MODEL - ATTN_WIDTH
N_DIFF_HEADS = 4
DIFF_HEAD_DIM = 64
V_HEAD_DIM = 2 * DIFF_HEAD_DIM
POOL_WINDOWS = (2, 4, 8, 16)
POOL_GROUP = POOL_WIDTH // len(POOL_WINDOWS)
D_FF = 2816
NUM_BUCKETS = 32
MAX_DISTANCE = 128
RMS_EPS = 1e-6
QK_SCALE = DIFF_HEAD_DIM ** -0.5
LOG2E = math.log2(math.e)

LANES = 128
SUBLANES = 8
MXU_WIDTH = 256
ATTN_TILE = 512
ROW_TILE = 512
FF_CHUNK = D_FF // 2
HALO = SUBLANES
FAR_BLOCK = 2
NEAR_REACH = 2
MASK_VALUE = -1e30
VMEM_LIMIT_BYTES = 48 * 1024 * 1024

F32 = jnp.float32
BF16 = jnp.bfloat16
NT_DIMS = (((1,), (1,)), ((), ()))


def _rms(x, g):
    var = jnp.mean(x * x, axis=-1, keepdims=True)
    return x * lax.rsqrt(var + RMS_EPS) * g


def _params(*semantics):
    return pltpu.CompilerParams(dimension_semantics=semantics, vmem_limit_bytes=VMEM_LIMIT_BYTES)


def _inproj_kernel(x_ref, g_ref, w_ref, wvt_ref, q_ref, k_ref, vt_ref, p_ref):
    h = _rms(x_ref[...], g_ref[...]).astype(BF16)
    a = ATTN_WIDTH
    q = jnp.dot(h, w_ref[:, 0:a], preferred_element_type=F32)
    q_ref[...] = (q * (QK_SCALE * LOG2E)).astype(BF16)
    k_ref[...] = jnp.dot(h, w_ref[:, a:2 * a], preferred_element_type=F32).astype(BF16)
    p_ref[...] = jnp.dot(h, w_ref[:, 3 * a:], preferred_element_type=F32)
    vt_ref[0] = lax.dot_general(wvt_ref[...], h, NT_DIMS, preferred_element_type=F32).astype(BF16)


def _inproj(x, g, w_in, w_vt, tm):
    t = x.shape[0]
    row = lambda i: (i, 0)
    const = lambda i: (0, 0)
    return pl.pallas_call(
        _inproj_kernel,
        grid=(t // tm,),
        in_specs=[
            pl.BlockSpec((tm, D_MODEL), row),
            pl.BlockSpec((1, D_MODEL), const),
            pl.BlockSpec(w_in.shape, const),
            pl.BlockSpec(w_vt.shape, const),
        ],
        out_specs=[
            pl.BlockSpec((tm, ATTN_WIDTH), row),
            pl.BlockSpec((tm, ATTN_WIDTH), row),
            pl.BlockSpec((1, ATTN_WIDTH, tm), lambda i: (i, 0, 0)),
            pl.BlockSpec((tm, POOL_WIDTH), row),
        ],
        out_shape=[
            jax.ShapeDtypeStruct((t, ATTN_WIDTH), BF16),
            jax.ShapeDtypeStruct((t, ATTN_WIDTH), BF16),
            jax.ShapeDtypeStruct((t // tm, ATTN_WIDTH, tm), BF16),
            jax.ShapeDtypeStruct((t, POOL_WIDTH), F32),
        ],
        compiler_params=_params("parallel"),
        name="inproj",
    )(x, g, w_in, w_vt)


def _attn_kernel(far_ref, q_ref, qn_ref, k_ref, vt_ref, bias_ref, lq_ref, lk_ref, hn_ref, o_ref,
                 qs_scr, sa_scr, sb_scr, m_scr, l_scr, acc_scr, *, tile, nk, lam_init):
    h = pl.program_id(1)
    i = pl.program_id(2)
    t = tile
    cq = min(MXU_WIDTH, t)
    n_chunks = 2 * t // cq

    cur = lax.rem(i, 2)
    nxt = 1 - cur

    def stack_maps(slot, q):
        lane = lax.broadcasted_iota(jnp.int32, (t, LANES), 1)
        zero = jnp.zeros_like(q)
        qs_scr[slot, 0:t, :] = jnp.where(lane < DIFF_HEAD_DIM, q, zero)
        qs_scr[slot, t:2 * t, :] = jnp.where(lane >= DIFF_HEAD_DIM, q, zero)

    def scores(j, slot):
        start = pl.multiple_of(j * t, t)
        return lax.dot_general(k_ref[pl.ds(start, t), :], qs_scr[slot], NT_DIMS,
                               preferred_element_type=F32)

    n_pairs = nk // 2

    def plan(qi):
        first_near = lax.div(qi + 1, 2) - 1
        n_left = jnp.maximum(first_near, 0)
        right_start = first_near + 2
        n_far = n_left + jnp.maximum(n_pairs - right_start, 0)
        return n_left, right_start, n_far, n_left, jnp.minimum(first_near + 1, n_pairs - 1)

    def first_tile(qi):
        n_left_q, right_start_q, n_far_q, near_lo_q, _ = plan(qi)
        first_far = jnp.where(n_left_q > 0, 0, right_start_q)
        return 2 * jnp.where(n_far_q > 0, first_far, near_lo_q)

    @pl.when(i == 0)
    def _():
        stack_maps(0, q_ref[...])
        sa_scr[...] = scores(first_tile(0), 0)

    stack_maps(nxt, qn_ref[...])
    m_scr[...] = jnp.full(m_scr.shape, MASK_VALUE, F32)
    l_scr[...] = jnp.zeros(l_scr.shape, F32)
    acc_scr[...] = jnp.zeros(acc_scr.shape, F32)

    def softmax_pv(j, s_ref, near, c):
        vtj = vt_ref[j]
        for ci in range(n_chunks):
            cols = slice(ci * cq, (ci + 1) * cq)
            s = s_ref[:, cols]
            if near:
                q0 = (ci * cq) % t
                s = s + bias_ref[0, j - i + NEAR_REACH, :, q0:q0 + cq]
            m_prev = m_scr[:, cols]
            m_cur = jnp.max(s, axis=0, keepdims=True)
            if c is not None:
                m_cur = m_cur + c
            m_next = jnp.maximum(m_prev, m_cur)
            p = jnp.exp2(s - (m_next if c is None else m_next - c))
            alpha = jnp.exp2(m_prev - m_next)
            l_scr[:, cols] = alpha * l_scr[:, cols] + jnp.sum(p, axis=0, keepdims=True)
            acc_scr[:, cols] = alpha * acc_scr[:, cols] + jnp.dot(vtj, p.astype(BF16),
                                                                  preferred_element_type=F32)
            m_scr[:, cols] = m_next

    def pair(m, near, c, ahead_tile, ahead_slot):
        j0 = 2 * m
        sb_scr[...] = scores(j0 + 1, cur)
        softmax_pv(j0, sa_scr, near, c)
        sa_scr[...] = scores(ahead_tile, ahead_slot)
        softmax_pv(j0 + 1, sb_scr, near, c)

    n_left, right_start, n_far, near_lo, near_hi = plan(i)
    next_first = first_tile(jnp.minimum(i + 1, nk - 1))

    def far_pair(f):
        def index(g):
            return jnp.where(g < n_left, g, g - n_left + right_start)
        ahead = jnp.where(f + 1 < n_far, index(f + 1), near_lo)
        c = jnp.where(f < n_left, far_ref[h, 0], far_ref[h, 1])
        pair(index(f), False, c, 2 * ahead, cur)

    size = 1
    while size < FAR_BLOCK:
        @pl.when(jnp.bitwise_and(n_far, size) != 0)
        def _(size=size):
            base = jnp.bitwise_and(n_far, size - 1)
            for u in range(size):
                far_pair(base + u)
        size *= 2

    def far_block(u, carry):
        base = jnp.bitwise_and(n_far, FAR_BLOCK - 1) + FAR_BLOCK * u
        for v in range(FAR_BLOCK):
            far_pair(base + v)
        return carry

    lax.fori_loop(0, lax.div(n_far, FAR_BLOCK), far_block, 0)

    @pl.when(near_hi > near_lo)
    def _():
        pair(near_lo, True, None, 2 * near_hi, cur)
        pair(near_hi, True, None, next_first, nxt)

    @pl.when(near_hi == near_lo)
    def _():
        pair(near_lo, True, None, next_first, nxt)

    o = acc_scr[...] / l_scr[...]
    prod = lq_ref[...] * lk_ref[...]
    lam = (jnp.exp(jnp.sum(prod[0:1], axis=1, keepdims=True))
           - jnp.exp(jnp.sum(prod[1:2], axis=1, keepdims=True)) + lam_init)
    a = (o[:, 0:t] - lam * o[:, t:2 * t]).T
    o_ref[...] = (_rms(a, hn_ref[...]) * (1.0 - lam_init)).astype(o_ref.dtype)


def _attention(q, k, vt, bias_near_t, far, lam_q, lam_k, head_norm, lam_init, tile, seq_len):
    t_all = q.shape[0]
    b = t_all // seq_len
    nk = seq_len // tile
    assert nk % 2 == 0, "key tiles are processed in pairs"
    kern = functools.partial(_attn_kernel, tile=tile, nk=nk, lam_init=lam_init)
    return pl.pallas_call(
        kern,
        grid=(b, N_DIFF_HEADS, nk),
        in_specs=[
            pl.BlockSpec(memory_space=pltpu.SMEM),
            pl.BlockSpec((tile, V_HEAD_DIM), lambda bi, h, i: (bi * nk + i, h)),
            pl.BlockSpec((tile, V_HEAD_DIM), lambda bi, h, i: (bi * nk + jnp.minimum(i + 1, nk - 1), h)),
            pl.BlockSpec((seq_len, V_HEAD_DIM), lambda bi, h, i: (bi, h)),
            pl.BlockSpec((nk, V_HEAD_DIM, tile), lambda bi, h, i: (bi, h, 0)),
            pl.BlockSpec((1, 2 * NEAR_REACH + 1, tile, tile), lambda bi, h, i: (h, 0, 0, 0)),
            pl.BlockSpec((2, DIFF_HEAD_DIM), lambda bi, h, i: (0, 0)),
            pl.BlockSpec((2, DIFF_HEAD_DIM), lambda bi, h, i: (0, 0)),
            pl.BlockSpec((1, V_HEAD_DIM), lambda bi, h, i: (0, 0)),
        ],
        out_specs=pl.BlockSpec((tile, V_HEAD_DIM), lambda bi, h, i: (bi * nk + i, h)),
        out_shape=jax.ShapeDtypeStruct((t_all, ATTN_WIDTH), BF16),
        scratch_shapes=[
            pltpu.VMEM((2, 2 * tile, V_HEAD_DIM), BF16),
            pltpu.VMEM((tile, 2 * tile), F32),
            pltpu.VMEM((tile, 2 * tile), F32),
            pltpu.VMEM((1, 2 * tile), F32),
            pltpu.VMEM((1, 2 * tile), F32),
            pltpu.VMEM((V_HEAD_DIM, 2 * tile), F32),
        ],
        compiler_params=_params("parallel", "parallel", "arbitrary"),
        name="diff_attention",
    )(far, q, q, k, vt, bias_near_t, lam_q, lam_k, head_norm)


def _mixout_kernel(a_ref, p_ref, pprev_ref, pnext_ref, x_ref, wout_ref, wpool_ref, ps_ref, g_ref,
                   o_ref, pbuf, *, tm, tiles_per_seq, seq_len):
    ti = lax.rem(pl.program_id(0), tiles_per_seq)
    keep_prev = (ti > 0).astype(F32)
    keep_next = (ti < tiles_per_seq - 1).astype(F32)
    pbuf[0:HALO, :] = pprev_ref[...] * keep_prev
    pbuf[HALO:HALO + tm, :] = p_ref[...]
    pbuf[HALO + tm:, :] = pnext_ref[...] * keep_next

    pos = ti * tm + lax.broadcasted_iota(jnp.int32, (tm, 1), 0)
    mixed = []
    for g, win in enumerate(POOL_WINDOWS):
        lo_off = -(win // 2)
        hi_off = win - win // 2 - 1
        cols = slice(g * POOL_GROUP, (g + 1) * POOL_GROUP)
        total = pbuf[HALO + lo_off:HALO + lo_off + tm, cols]
        for off in range(lo_off + 1, hi_off + 1):
            total = total + pbuf[HALO + off:HALO + off + tm, cols]
        lo = jnp.maximum(pos + lo_off, 0)
        hi = jnp.minimum(pos + hi_off, seq_len - 1)
        cnt = (hi - lo + 1).astype(F32)
        y = total / cnt - pbuf[HALO:HALO + tm, cols]
        mixed.append(jnp.dot(y.astype(BF16), wpool_ref[g], preferred_element_type=F32))
    m = jnp.concatenate(mixed, axis=1) * ps_ref[...]

    o = jnp.dot(a_ref[...], wout_ref[0:ATTN_WIDTH, :], preferred_element_type=F32)
    o = o + jnp.dot(m.astype(BF16), wout_ref[ATTN_WIDTH:, :], preferred_element_type=F32)
    o_ref[...] = x_ref[...] + _rms(o, g_ref[...])


def _halo_specs(tm, width, n_rows):
    per_tile = tm // HALO
    last = n_rows // HALO - 1
    prev = pl.BlockSpec((HALO, width), lambda i, *_: (jnp.maximum(i * per_tile - 1, 0), 0))
    nxt = pl.BlockSpec((HALO, width), lambda i, *_: (jnp.minimum((i + 1) * per_tile, last), 0))
    return prev, nxt


def _mixout(a, p, x, w_out, w_pool, pool_scale, g, seq_len, tm):
    t = x.shape[0]
    row = lambda i: (i, 0)
    const = lambda i: (0, 0)
    prev, nxt = _halo_specs(tm, POOL_WIDTH, t)
    kern = functools.partial(_mixout_kernel, tm=tm, tiles_per_seq=seq_len // tm, seq_len=seq_len)
    return pl.pallas_call(
        kern,
        grid=(t // tm,),
        in_specs=[
            pl.BlockSpec((tm, ATTN_WIDTH), row),
            pl.BlockSpec((tm, POOL_WIDTH), row),
            prev,
            nxt,
            pl.BlockSpec((tm, D_MODEL), row),
            pl.BlockSpec(w_out.shape, const),
            pl.BlockSpec(w_pool.shape, lambda i: (0, 0, 0)),
            pl.BlockSpec((1, POOL_WIDTH), const),
            pl.BlockSpec((1, D_MODEL), const),
        ],
        out_specs=pl.BlockSpec((tm, D_MODEL), row),
        out_shape=jax.ShapeDtypeStruct((t, D_MODEL), F32),
        scratch_shapes=[pltpu.VMEM((tm + 2 * HALO, POOL_WIDTH), F32)],
        compiler_params=_params("parallel"),
        name="pool_outproj",
    )(a, p, p, p, x, w_out, w_pool, pool_scale, g)


def _gelu_tanh(x):
    return 0.5 * x * (1.0 + jnp.tanh(math.sqrt(2.0 / math.pi) * (x + 0.044715 * (x * x * x))))


def _ffn_kernel(x_ref, xprev_ref, xnext_ref, gpre_ref, wg_ref, wv_ref, cwg_ref, cwv_ref, cbg_ref, cbv_ref,
                wd_ref, gpost_ref, o_ref, h_scr, u_scr, acc_scr, *, tm, tiles_per_seq):
    c = pl.program_id(1)

    @pl.when(c == 0)
    def _():
        ti = lax.rem(pl.program_id(0), tiles_per_seq)
        keep_prev = (ti > 0).astype(F32)
        keep_next = (ti < tiles_per_seq - 1).astype(F32)
        g = gpre_ref[...]
        h_scr[0:HALO, :] = _rms(xprev_ref[...], g) * keep_prev
        h_scr[HALO:HALO + tm, :] = _rms(x_ref[...], g)
        h_scr[HALO + tm:, :] = _rms(xnext_ref[...], g) * keep_next
        acc_scr[...] = jnp.zeros(acc_scr.shape, F32)

    h = h_scr[...].astype(BF16)

    def conv(w_ref, cw_ref, cb_ref):
        u_scr[...] = jnp.dot(h, w_ref[...], preferred_element_type=F32)
        cw = cw_ref[...]
        return (u_scr[HALO - 1:HALO - 1 + tm, :] * cw[0:1]
                + u_scr[HALO:HALO + tm, :] * cw[1:2]
                + u_scr[HALO + 1:HALO + 1 + tm, :] * cw[2:3]
                + cb_ref[...])

    gate = conv(wg_ref, cwg_ref, cbg_ref)
    val = conv(wv_ref, cwv_ref, cbv_ref)
    f = (_gelu_tanh(gate) * val).astype(BF16)
    acc_scr[...] += jnp.dot(f, wd_ref[...], preferred_element_type=F32)

    @pl.when(c == pl.num_programs(1) - 1)
    def _():
        o_ref[...] = x_ref[...] + _rms(acc_scr[...], gpost_ref[...])


def _ffn(x, g_pre, w_up, conv_w, conv_b, w_down, g_post, seq_len, tm):
    t = x.shape[0]
    n_chunks = D_FF // FF_CHUNK
    row = lambda i, c: (i, 0)
    const = lambda i, c: (0, 0)
    gate_cols = lambda i, c: (0, c)
    val_cols = lambda i, c: (0, n_chunks + c)
    prev, nxt = _halo_specs(tm, D_MODEL, t)
    kern = functools.partial(_ffn_kernel, tm=tm, tiles_per_seq=seq_len // tm)
    return pl.pallas_call(
        kern,
        grid=(t // tm, n_chunks),
        in_specs=[
            pl.BlockSpec((tm, D_MODEL), row),
            prev,
            nxt,
            pl.BlockSpec((1, D_MODEL), const),
            pl.BlockSpec((D_MODEL, FF_CHUNK), gate_cols),
            pl.BlockSpec((D_MODEL, FF_CHUNK), val_cols),
            pl.BlockSpec((3, FF_CHUNK), gate_cols),
            pl.BlockSpec((3, FF_CHUNK), val_cols),
            pl.BlockSpec((1, FF_CHUNK), gate_cols),
            pl.BlockSpec((1, FF_CHUNK), val_cols),
            pl.BlockSpec((FF_CHUNK, D_MODEL), lambda i, c: (c, 0)),
            pl.BlockSpec((1, D_MODEL), const),
        ],
        out_specs=pl.BlockSpec((tm, D_MODEL), row),
        out_shape=jax.ShapeDtypeStruct((t, D_MODEL), F32),
        scratch_shapes=[
            pltpu.VMEM((tm + 2 * HALO, D_MODEL), F32),
            pltpu.VMEM((tm + 2 * HALO, FF_CHUNK), F32),
            pltpu.VMEM((tm, D_MODEL), F32),
        ],
        compiler_params=_params("parallel", "arbitrary"),
        name="conv_mlp",
    )(x, x, x, g_pre, w_up, w_up, conv_w, conv_w, conv_b, conv_b, w_down, g_post)


def _rel_bucket(rel):
    nb = NUM_BUCKETS // 2
    max_exact = nb // 2
    ret = jnp.where(rel > 0, nb, 0)
    n = jnp.abs(rel)
    nf = jnp.maximum(n, 1).astype(F32)
    large = max_exact + (jnp.log(nf / max_exact) / math.log(MAX_DISTANCE / max_exact)
                         * (nb - max_exact)).astype(jnp.int32)
    large = jnp.minimum(large, nb - 1)
    return ret + jnp.where(n < max_exact, n, large)


def _bias_expand_kernel(y_ref, o_ref, *, tile):
    span = y_ref.shape[-1]
    table = jnp.broadcast_to(y_ref[0], (tile, span))
    skewed = pltpu.roll(table, 0, 1, stride=1, stride_axis=0)
    for di, d in enumerate(range(-NEAR_REACH, NEAR_REACH + 1)):
        start = (-d * tile) % span
        o_ref[0, di] = skewed[:, start:start + tile]


def _bias_tables(rel_bias, tile):
    assert tile >= MAX_DISTANCE, "far tiles must lie beyond the last distance bucket"
    n_near = 2 * NEAR_REACH + 1
    span = 2 * (NEAR_REACH + 1) * tile
    m = jnp.arange(span, dtype=jnp.int32)
    rel = jnp.where(m < span // 2, -m, span - m)
    rel_bias = rel_bias.astype(F32) * LOG2E
    by_rel = rel_bias[_rel_bucket(rel)].T[:, None, :]
    near = pl.pallas_call(
        functools.partial(_bias_expand_kernel, tile=tile),
        grid=(N_DIFF_HEADS,),
        in_specs=[pl.BlockSpec((1, 1, span), lambda h: (h, 0, 0))],
        out_specs=pl.BlockSpec((1, n_near, tile, tile), lambda h: (h, 0, 0, 0)),
        out_shape=jax.ShapeDtypeStruct((N_DIFF_HEADS, n_near, tile, tile), F32),
        compiler_params=_params("parallel"),
        name="bias_expand",
    )(by_rel)
    far = jnp.stack([rel_bias[NUM_BUCKETS // 2 - 1], rel_bias[NUM_BUCKETS - 1]], axis=1)
    return near, far


def kernel(x_prompt, x_sample, rel_bias, ln_mix_pre, ln_mix_post, w_in, lam_q, lam_k, head_norm, w_pool,
           pool_scale, w_out, ln_ffn_pre, ln_ffn_post, w_up, conv_w, conv_b, w_down):
    depth = w_in.shape[0]
    w_in_b, w_out_b, w_pool_b = w_in.astype(BF16), w_out.astype(BF16), w_pool.astype(BF16)
    w_up_b, w_down_b = w_up.astype(BF16), w_down.astype(BF16)
    w_vt_b = jnp.swapaxes(w_in_b[:, :, 2 * ATTN_WIDTH:3 * ATTN_WIDTH], 1, 2)
    tables = {}

    def trunk(x):
        b, s, d = x.shape
        tile = min(ATTN_TILE, s)
        tm = min(ROW_TILE, s)
        assert s % tile == 0 and s % tm == 0 and tm % HALO == 0
        if tile not in tables:
            tables[tile] = _bias_tables(rel_bias, tile)
        bias_near_t, far = tables[tile]
        xf = x.reshape(b * s, d)
        for i in range(depth):
            lam_init = 0.8 - 0.6 * math.exp(-0.3 * i)
            q, k, vt, p = _inproj(xf, ln_mix_pre[i][None], w_in_b[i], w_vt_b[i], tile)
            a = _attention(q, k, vt, bias_near_t, far, lam_q[i], lam_k[i], head_norm[i][None], lam_init,
                           tile, s)
            xf = _mixout(a, p, xf, w_out_b[i], w_pool_b[i], pool_scale[i][None], ln_mix_post[i][None], s, tm)
            xf = _ffn(xf, ln_ffn_pre[i][None], w_up_b[i], conv_w[i], conv_b[i][None], w_down_b[i],
                      ln_ffn_post[i][None], s, tm)
        return xf.reshape(b, s, d)

    return (trunk(x_prompt), trunk(x_sample))
```

```python
import functools
import math

import jax
import jax.numpy as jnp
from jax import lax
from jax.experimental import pallas as pl
from jax.experimental.pallas import tpu as pltpu

D_MODEL = 1024
ATTN_WIDTH = 512
POOL_WIDTH = D_MODEL - ATTN_WIDTH
N_DIFF_HEADS = 4
DIFF_HEAD_DIM = 64
V_HEAD_DIM = 2 * DIFF_HEAD_DIM
POOL_WINDOWS = (2, 4, 8, 16)
POOL_GROUP = POOL_WIDTH // len(POOL_WINDOWS)
D_FF = 2816
NUM_BUCKETS = 32
MAX_DISTANCE = 128
RMS_EPS = 1e-6
QK_SCALE = DIFF_HEAD_DIM ** -0.5
LOG2E = math.log2(math.e)

LANES = 128
SUBLANES = 8
MXU_WIDTH = 256
ATTN_TILE = 512
ROW_TILE = 512
FF_CHUNK = D_FF // 2
HALO = SUBLANES
FAR_BLOCK = 4
NEAR_REACH = 2
MASK_VALUE = -1e30
VMEM_LIMIT_BYTES = 48 * 1024 * 1024

F32 = jnp.float32
BF16 = jnp.bfloat16
NT_DIMS = (((1,), (1,)), ((), ()))


def _rms(x, g):
    var = jnp.mean(x * x, axis=-1, keepdims=True)
    return x * lax.rsqrt(var + RMS_EPS) * g


def _params(*semantics):
    return pltpu.CompilerParams(dimension_semantics=semantics, vmem_limit_bytes=VMEM_LIMIT_BYTES)


def _inproj_kernel(x_ref, g_ref, w_ref, wvt_ref, q_ref, k_ref, vt_ref, p_ref):
    h = _rms(x_ref[...], g_ref[...]).astype(BF16)
    a = ATTN_WIDTH
    q = jnp.dot(h, w_ref[:, 0:a], preferred_element_type=F32)
    q_ref[...] = (q * (QK_SCALE * LOG2E)).astype(BF16)
    k_ref[...] = jnp.dot(h, w_ref[:, a:2 * a], preferred_element_type=F32).astype(BF16)
    p_ref[...] = jnp.dot(h, w_ref[:, 3 * a:], preferred_element_type=F32)
    vt_ref[0] = lax.dot_general(wvt_ref[...], h, NT_DIMS, preferred_element_type=F32).astype(BF16)


def _inproj(x, g, w_in, w_vt, tm):
    t = x.shape[0]
    row = lambda i: (i, 0)
    const = lambda i: (0, 0)
    return pl.pallas_call(
        _inproj_kernel,
        grid=(t // tm,),
        in_specs=[
            pl.BlockSpec((tm, D_MODEL), row),
            pl.BlockSpec((1, D_MODEL), const),
            pl.BlockSpec(w_in.shape, const),
            pl.BlockSpec(w_vt.shape, const),
        ],
        out_specs=[
            pl.BlockSpec((tm, ATTN_WIDTH), row),
            pl.BlockSpec((tm, ATTN_WIDTH), row),
            pl.BlockSpec((1, ATTN_WIDTH, tm), lambda i: (i, 0, 0)),
            pl.BlockSpec((tm, POOL_WIDTH), row),
        ],
        out_shape=[
            jax.ShapeDtypeStruct((t, ATTN_WIDTH), BF16),
            jax.ShapeDtypeStruct((t, ATTN_WIDTH), BF16),
            jax.ShapeDtypeStruct((t // tm, ATTN_WIDTH, tm), BF16),
            jax.ShapeDtypeStruct((t, POOL_WIDTH), F32),
        ],
        compiler_params=_params("parallel"),
        name="inproj",
    )(x, g, w_in, w_vt)


def _attn_kernel(far_ref, q_ref, qn_ref, k_ref, vt_ref, bias_ref, lq_ref, lk_ref, hn_ref, o_ref,
                 qs_scr, sa_scr, sb_scr, m_scr, l_scr, acc_scr, *, tile, nk, lam_init):
    h = pl.program_id(1)
    i = pl.program_id(2)
    t = tile
    cq = min(MXU_WIDTH, t)
    n_chunks = 2 * t // cq

    cur = lax.rem(i, 2)
    nxt = 1 - cur

    def stack_maps(slot, q):
        lane = lax.broadcasted_iota(jnp.int32, (t, LANES), 1)
        zero = jnp.zeros_like(q)
        qs_scr[slot, 0:t, :] = jnp.where(lane < DIFF_HEAD_DIM, q, zero)
        qs_scr[slot, t:2 * t, :] = jnp.where(lane >= DIFF_HEAD_DIM, q, zero)

    def scores(j, slot):
        start = pl.multiple_of(j * t, t)
        return lax.dot_general(k_ref[pl.ds(start, t), :], qs_scr[slot], NT_DIMS,
                               preferred_element_type=F32)

    n_pairs = nk // 2

    def plan(qi):
        first_near = lax.div(qi + 1, 2) - 1
        n_left = jnp.maximum(first_near, 0)
        right_start = first_near + 2
        n_far = n_left + jnp.maximum(n_pairs - right_start, 0)
        return n_left, right_start, n_far, n_left, jnp.minimum(first_near + 1, n_pairs - 1)

    def first_tile(qi):
        n_left_q, right_start_q, n_far_q, near_lo_q, _ = plan(qi)
        first_far = jnp.where(n_left_q > 0, 0, right_start_q)
        return 2 * jnp.where(n_far_q > 0, first_far, near_lo_q)

    @pl.when(i == 0)
    def _():
        stack_maps(0, q_ref[...])
        sa_scr[...] = scores(first_tile(0), 0)

    stack_maps(nxt, qn_ref[...])
    m_scr[...] = jnp.full(m_scr.shape, MASK_VALUE, F32)
    l_scr[...] = jnp.zeros(l_scr.shape, F32)
    acc_scr[...] = jnp.zeros(acc_scr.shape, F32)

    def softmax_pv(j, s_ref, near, c):
        vtj = vt_ref[j]
        for ci in range(n_chunks):
            cols = slice(ci * cq, (ci + 1) * cq)
            s = s_ref[:, cols]
            if near:
                q0 = (ci * cq) % t
                s = s + bias_ref[0, j - i + NEAR_REACH, :, q0:q0 + cq]
            m_prev = m_scr[:, cols]
            m_cur = jnp.max(s, axis=0, keepdims=True)
            if c is not None:
                m_cur = m_cur + c
            m_next = jnp.maximum(m_prev, m_cur)
            p = jnp.exp2(s - (m_next if c is None else m_next - c))
            alpha = jnp.exp2(m_prev - m_next)
            l_scr[:, cols] = alpha * l_scr[:, cols] + jnp.sum(p, axis=0, keepdims=True)
            acc_scr[:, cols] = alpha * acc_scr[:, cols] + jnp.dot(vtj, p.astype(BF16),
                                                                  preferred_element_type=F32)
            m_scr[:, cols] = m_next

    def pair(m, near, c, ahead_tile, ahead_slot):
        j0 = 2 * m
        sb_scr[...] = scores(j0 + 1, cur)
        softmax_pv(j0, sa_scr, near, c)
        sa_scr[...] = scores(ahead_tile, ahead_slot)
        softmax_pv(j0 + 1, sb_scr, near, c)

    n_left, right_start, n_far, near_lo, near_hi = plan(i)
    next_first = first_tile(jnp.minimum(i + 1, nk - 1))

    def far_pair(f):
        def index(g):
            return jnp.where(g < n_left, g, g - n_left + right_start)
        ahead = jnp.where(f + 1 < n_far, index(f + 1), near_lo)
        c = jnp.where(f < n_left, far_ref[h, 0], far_ref[h, 1])
        pair(index(f), False, c, 2 * ahead, cur)

    size = 1
    while size < FAR_BLOCK:
        @pl.when(jnp.bitwise_and(n_far, size) != 0)
        def _(size=size):
            base = jnp.bitwise_and(n_far, size - 1)
            for u in range(size):
                far_pair(base + u)
        size *= 2

    def far_block(u, carry):
        base = jnp.bitwise_and(n_far, FAR_BLOCK - 1) + FAR_BLOCK * u
        for v in range(FAR_BLOCK):
            far_pair(base + v)
        return carry

    lax.fori_loop(0, lax.div(n_far, FAR_BLOCK), far_block, 0)

    @pl.when(near_hi > near_lo)
    def _():
        pair(near_lo, True, None, 2 * near_hi, cur)
        pair(near_hi, True, None, next_first, nxt)

    @pl.when(near_hi == near_lo)
    def _():
        pair(near_lo, True, None, next_first, nxt)

    o = acc_scr[...] / l_scr[...]
    prod = lq_ref[...] * lk_ref[...]
    lam = (jnp.exp(jnp.sum(prod[0:1], axis=1, keepdims=True))
           - jnp.exp(jnp.sum(prod[1:2], axis=1, keepdims=True)) + lam_init)
    a = (o[:, 0:t] - lam * o[:, t:2 * t]).T
    o_ref[...] = (_rms(a, hn_ref[...]) * (1.0 - lam_init)).astype(o_ref.dtype)


def _attention(q, k, vt, bias_near_t, far, lam_q, lam_k, head_norm, lam_init, tile, seq_len):
    t_all = q.shape[0]
    b = t_all // seq_len
    nk = seq_len // tile
    assert nk % 2 == 0, "key tiles are processed in pairs"
    kern = functools.partial(_attn_kernel, tile=tile, nk=nk, lam_init=lam_init)
    return pl.pallas_call(
        kern,
        grid=(b, N_DIFF_HEADS, nk),
        in_specs=[
            pl.BlockSpec(memory_space=pltpu.SMEM),
            pl.BlockSpec((tile, V_HEAD_DIM), lambda bi, h, i: (bi * nk + i, h)),
            pl.BlockSpec((tile, V_HEAD_DIM), lambda bi, h, i: (bi * nk + jnp.minimum(i + 1, nk - 1), h)),
            pl.BlockSpec((seq_len, V_HEAD_DIM), lambda bi, h, i: (bi, h)),
            pl.BlockSpec((nk, V_HEAD_DIM, tile), lambda bi, h, i: (bi, h, 0)),
            pl.BlockSpec((1, 2 * NEAR_REACH + 1, tile, tile), lambda bi, h, i: (h, 0, 0, 0)),
            pl.BlockSpec((2, DIFF_HEAD_DIM), lambda bi, h, i: (0, 0)),
            pl.BlockSpec((2, DIFF_HEAD_DIM), lambda bi, h, i: (0, 0)),
            pl.BlockSpec((1, V_HEAD_DIM), lambda bi, h, i: (0, 0)),
        ],
        out_specs=pl.BlockSpec((tile, V_HEAD_DIM), lambda bi, h, i: (bi * nk + i, h)),
        out_shape=jax.ShapeDtypeStruct((t_all, ATTN_WIDTH), BF16),
        scratch_shapes=[
            pltpu.VMEM((2, 2 * tile, V_HEAD_DIM), BF16),
            pltpu.VMEM((tile, 2 * tile), F32),
            pltpu.VMEM((tile, 2 * tile), F32),
            pltpu.VMEM((1, 2 * tile), F32),
            pltpu.VMEM((1, 2 * tile), F32),
            pltpu.VMEM((V_HEAD_DIM, 2 * tile), F32),
        ],
        compiler_params=_params("parallel", "parallel", "arbitrary"),
        name="diff_attention",
    )(far, q, q, k, vt, bias_near_t, lam_q, lam_k, head_norm)


def _mixout_kernel(a_ref, p_ref, pprev_ref, pnext_ref, x_ref, wout_ref, wpool_ref, ps_ref, g_ref,
                   o_ref, pbuf, *, tm, tiles_per_seq, seq_len):
    ti = lax.rem(pl.program_id(0), tiles_per_seq)
    keep_prev = (ti > 0).astype(F32)
    keep_next = (ti < tiles_per_seq - 1).astype(F32)
    pbuf[0:HALO, :] = pprev_ref[...] * keep_prev
    pbuf[HALO:HALO + tm, :] = p_ref[...]
    pbuf[HALO + tm:, :] = pnext_ref[...] * keep_next

    pos = ti * tm + lax.broadcasted_iota(jnp.int32, (tm, 1), 0)
    mixed = []
    for g, win in enumerate(POOL_WINDOWS):
        lo_off = -(win // 2)
        hi_off = win - win // 2 - 1
        cols = slice(g * POOL_GROUP, (g + 1) * POOL_GROUP)
        total = pbuf[HALO + lo_off:HALO + lo_off + tm, cols]
        for off in range(lo_off + 1, hi_off + 1):
            total = total + pbuf[HALO + off:HALO + off + tm, cols]
        lo = jnp.maximum(pos + lo_off, 0)
        hi = jnp.minimum(pos + hi_off, seq_len - 1)
        cnt = (hi - lo + 1).astype(F32)
        y = total / cnt - pbuf[HALO:HALO + tm, cols]
        mixed.append(jnp.dot(y.astype(BF16), wpool_ref[g], preferred_element_type=F32))
    m = jnp.concatenate(mixed, axis=1) * ps_ref[...]

    o = jnp.dot(a_ref[...], wout_ref[0:ATTN_WIDTH, :], preferred_element_type=F32)
    o = o + jnp.dot(m.astype(BF16), wout_ref[ATTN_WIDTH:, :], preferred_element_type=F32)
    o_ref[...] = x_ref[...] + _rms(o, g_ref[...])


def _halo_specs(tm, width, n_rows):
    per_tile = tm // HALO
    last = n_rows // HALO - 1
    prev = pl.BlockSpec((HALO, width), lambda i, *_: (jnp.maximum(i * per_tile - 1, 0), 0))
    nxt = pl.BlockSpec((HALO, width), lambda i, *_: (jnp.minimum((i + 1) * per_tile, last), 0))
    return prev, nxt


def _mixout(a, p, x, w_out, w_pool, pool_scale, g, seq_len, tm):
    t = x.shape[0]
    row = lambda i: (i, 0)
    const = lambda i: (0, 0)
    prev, nxt = _halo_specs(tm, POOL_WIDTH, t)
    kern = functools.partial(_mixout_kernel, tm=tm, tiles_per_seq=seq_len // tm, seq_len=seq_len)
    return pl.pallas_call(
        kern,
        grid=(t // tm,),
        in_specs=[
            pl.BlockSpec((tm, ATTN_WIDTH), row),
            pl.BlockSpec((tm, POOL_WIDTH), row),
            prev,
            nxt,
            pl.BlockSpec((tm, D_MODEL), row),
            pl.BlockSpec(w_out.shape, const),
            pl.BlockSpec(w_pool.shape, lambda i: (0, 0, 0)),
            pl.BlockSpec((1, POOL_WIDTH), const),
            pl.BlockSpec((1, D_MODEL), const),
        ],
        out_specs=pl.BlockSpec((tm, D_MODEL), row),
        out_shape=jax.ShapeDtypeStruct((t, D_MODEL), F32),
        scratch_shapes=[pltpu.VMEM((tm + 2 * HALO, POOL_WIDTH), F32)],
        compiler_params=_params("parallel"),
        name="pool_outproj",
    )(a, p, p, p, x, w_out, w_pool, pool_scale, g)


def _gelu_tanh(x):
    return 0.5 * x * (1.0 + jnp.tanh(math.sqrt(2.0 / math.pi) * (x + 0.044715 * (x * x * x))))


def _ffn_kernel(x_ref, xprev_ref, xnext_ref, gpre_ref, wg_ref, wv_ref, cwg_ref, cwv_ref, cbg_ref, cbv_ref,
                wd_ref, gpost_ref, o_ref, h_scr, u_scr, acc_scr, *, tm, tiles_per_seq):
    c = pl.program_id(1)

    @pl.when(c == 0)
    def _():
        ti = lax.rem(pl.program_id(0), tiles_per_seq)
        keep_prev = (ti > 0).astype(F32)
        keep_next = (ti < tiles_per_seq - 1).astype(F32)
        g = gpre_ref[...]
        h_scr[0:HALO, :] = _rms(xprev_ref[...], g) * keep_prev
        h_scr[HALO:HALO + tm, :] = _rms(x_ref[...], g)
        h_scr[HALO + tm:, :] = _rms(xnext_ref[...], g) * keep_next
        acc_scr[...] = jnp.zeros(acc_scr.shape, F32)

    h = h_scr[...].astype(BF16)

    def conv(w_ref, cw_ref, cb_ref):
        u_scr[...] = jnp.dot(h, w_ref[...], preferred_element_type=F32)
        cw = cw_ref[...]
        return (u_scr[HALO - 1:HALO - 1 + tm, :] * cw[0:1]
                + u_scr[HALO:HALO + tm, :] * cw[1:2]
                + u_scr[HALO + 1:HALO + 1 + tm, :] * cw[2:3]
                + cb_ref[...])

    gate = conv(wg_ref, cwg_ref, cbg_ref)
    val = conv(wv_ref, cwv_ref, cbv_ref)
    f = (_gelu_tanh(gate) * val).astype(BF16)
    acc_scr[...] += jnp.dot(f, wd_ref[...], preferred_element_type=F32)

    @pl.when(c == pl.num_programs(1) - 1)
    def _():
        o_ref[...] = x_ref[...] + _rms(acc_scr[...], gpost_ref[...])


def _ffn(x, g_pre, w_up, conv_w, conv_b, w_down, g_post, seq_len, tm):
    t = x.shape[0]
    n_chunks = D_FF // FF_CHUNK
    row = lambda i, c: (i, 0)
    const = lambda i, c: (0, 0)
    gate_cols = lambda i, c: (0, c)
    val_cols = lambda i, c: (0, n_chunks + c)
    prev, nxt = _halo_specs(tm, D_MODEL, t)
    kern = functools.partial(_ffn_kernel, tm=tm, tiles_per_seq=seq_len // tm)
    return pl.pallas_call(
        kern,
        grid=(t // tm, n_chunks),
        in_specs=[
            pl.BlockSpec((tm, D_MODEL), row),
            prev,
            nxt,
            pl.BlockSpec((1, D_MODEL), const),
            pl.BlockSpec((D_MODEL, FF_CHUNK), gate_cols),
            pl.BlockSpec((D_MODEL, FF_CHUNK), val_cols),
            pl.BlockSpec((3, FF_CHUNK), gate_cols),
            pl.BlockSpec((3, FF_CHUNK), val_cols),
            pl.BlockSpec((1, FF_CHUNK), gate_cols),
            pl.BlockSpec((1, FF_CHUNK), val_cols),
            pl.BlockSpec((FF_CHUNK, D_MODEL), lambda i, c: (c, 0)),
            pl.BlockSpec((1, D_MODEL), const),
        ],
        out_specs=pl.BlockSpec((tm, D_MODEL), row),
        out_shape=jax.ShapeDtypeStruct((t, D_MODEL), F32),
        scratch_shapes=[
            pltpu.VMEM((tm + 2 * HALO, D_MODEL), F32),
            pltpu.VMEM((tm + 2 * HALO, FF_CHUNK), F32),
            pltpu.VMEM((tm, D_MODEL), F32),
        ],
        compiler_params=_params("parallel", "arbitrary"),
        name="conv_mlp",
    )(x, x, x, g_pre, w_up, w_up, conv_w, conv_w, conv_b, conv_b, w_down, g_post)


def _rel_bucket(rel):
    nb = NUM_BUCKETS // 2
    max_exact = nb // 2
    ret = jnp.where(rel > 0, nb, 0)
    n = jnp.abs(rel)
    nf = jnp.maximum(n, 1).astype(F32)
    large = max_exact + (jnp.log(nf / max_exact) / math.log(MAX_DISTANCE / max_exact)
                         * (nb - max_exact)).astype(jnp.int32)
    large = jnp.minimum(large, nb - 1)
    return ret + jnp.where(n < max_exact, n, large)


def _bias_expand_kernel(y_ref, o_ref, *, tile):
    span = y_ref.shape[-1]
    table = jnp.broadcast_to(y_ref[0], (tile, span))
    skewed = pltpu.roll(table, 0, 1, stride=1, stride_axis=0)
    for di, d in enumerate(range(-NEAR_REACH, NEAR_REACH + 1)):
        start = (-d * tile) % span
        o_ref[0, di] = skewed[:, start:start + tile]


def _bias_tables(rel_bias, tile):
    assert tile >= MAX_DISTANCE, "far tiles must lie beyond the last distance bucket"
    n_near = 2 * NEAR_REACH + 1
    span = 2 * (NEAR_REACH + 1) * tile
    m = jnp.arange(span, dtype=jnp.int32)
    rel = jnp.where(m < span // 2, -m, span - m)
    rel_bias = rel_bias.astype(F32) * LOG2E
    by_rel = rel_bias[_rel_bucket(rel)].T[:, None, :]
    near = pl.pallas_call(
        functools.partial(_bias_expand_kernel, tile=tile),
        grid=(N_DIFF_HEADS,),
        in_specs=[pl.BlockSpec((1, 1, span), lambda h: (h, 0, 0))],
        out_specs=pl.BlockSpec((1, n_near, tile, tile), lambda h: (h, 0, 0, 0)),
        out_shape=jax.ShapeDtypeStruct((N_DIFF_HEADS, n_near, tile, tile), F32),
        compiler_params=_params("parallel"),
        name="bias_expand",
    )(by_rel)
    far = jnp.stack([rel_bias[NUM_BUCKETS // 2 - 1], rel_bias[NUM_BUCKETS - 1]], axis=1)
    return near, far


def kernel(x_prompt, x_sample, rel_bias, ln_mix_pre, ln_mix_post, w_in, lam_q, lam_k, head_norm, w_pool,
           pool_scale, w_out, ln_ffn_pre, ln_ffn_post, w_up, conv_w, conv_b, w_down):
    depth = w_in.shape[0]
    w_in_b, w_out_b, w_pool_b = w_in.astype(BF16), w_out.astype(BF16), w_pool.astype(BF16)
    w_up_b, w_down_b = w_up.astype(BF16), w_down.astype(BF16)
    w_vt_b = jnp.swapaxes(w_in_b[:, :, 2 * ATTN_WIDTH:3 * ATTN_WIDTH], 1, 2)
    tables = {}

    def trunk(x):
        b, s, d = x.shape
        tile = min(ATTN_TILE, s)
        tm = min(ROW_TILE, s)
        assert s % tile == 0 and s % tm == 0 and tm % HALO == 0
        if tile not in tables:
            tables[tile] = _bias_tables(rel_bias, tile)
        bias_near_t, far = tables[tile]
        xf = x.reshape(b * s, d)
        for i in range(depth):
            lam_init = 0.8 - 0.6 * math.exp(-0.3 * i)
            q, k, vt, p = _inproj(xf, ln_mix_pre[i][None], w_in_b[i], w_vt_b[i], tile)
            a = _attention(q, k, vt, bias_near_t, far, lam_q[i], lam_k[i], head_norm[i][None], lam_init,
                           tile, s)
            xf = _mixout(a, p, xf, w_out_b[i], w_pool_b[i], pool_scale[i][None], ln_mix_post[i][None], s, tm)
            xf = _ffn(xf, ln_ffn_pre[i][None], w_up_b[i], conv_w[i], conv_b[i][None], w_down_b[i],
                      ln_ffn_post[i][None], s, tm)
        return xf.reshape(b, s, d)

    return (trunk(x_prompt), trunk(x_sample))
```

```python
import functools
import math

import jax
import jax.numpy as jnp
from jax import lax
from jax.experimental import pallas as pl
from jax.experimental.pallas import tpu as pltpu

D_MODEL = 1024
ATTN_WIDTH = 512
POOL_WIDTH = D_MODEL - ATTN_WIDTH
N_DIFF_HEADS = 4
DIFF_HEAD_DIM = 64
V_HEAD_DIM = 2 * DIFF_HEAD_DIM
POOL_WINDOWS = (2, 4, 8, 16)
POOL_GROUP = POOL_WIDTH // len(POOL_WINDOWS)
D_FF = 2816
NUM_BUCKETS = 32
MAX_DISTANCE = 128
RMS_EPS = 1e-6
QK_SCALE = DIFF_HEAD_DIM ** -0.5
LOG2E = math.log2(math.e)

LANES = 128
SUBLANES = 8
MXU_WIDTH = 256
ATTN_TILE = 512
ROW_TILE = 512
FF_CHUNK = D_FF // 2
HALO = SUBLANES
FAR_BLOCK = 4
NEAR_LEAD = 2
NEAR_REACH = 2
MASK_VALUE = -1e30
VMEM_LIMIT_BYTES = 48 * 1024 * 1024

F32 = jnp.float32
BF16 = jnp.bfloat16
NT_DIMS = (((1,), (1,)), ((), ()))


def _rms(x, g):
    var = jnp.mean(x * x, axis=-1, keepdims=True)
    return x * lax.rsqrt(var + RMS_EPS) * g


def _params(*semantics):
    return pltpu.CompilerParams(dimension_semantics=semantics, vmem_limit_bytes=VMEM_LIMIT_BYTES)


def _inproj_kernel(x_ref, g_ref, w_ref, wvt_ref, q_ref, k_ref, vt_ref, p_ref):
    h = _rms(x_ref[...], g_ref[...]).astype(BF16)
    a = ATTN_WIDTH
    q = jnp.dot(h, w_ref[:, 0:a], preferred_element_type=F32)
    q_ref[...] = (q * (QK_SCALE * LOG2E)).astype(BF16)
    k_ref[...] = jnp.dot(h, w_ref[:, a:2 * a], preferred_element_type=F32).astype(BF16)
    p_ref[...] = jnp.dot(h, w_ref[:, 3 * a:], preferred_element_type=F32)
    vt_ref[0] = lax.dot_general(wvt_ref[...], h, NT_DIMS, preferred_element_type=F32).astype(BF16)


def _inproj(x, g, w_in, w_vt, tm):
    t = x.shape[0]
    row = lambda i: (i, 0)
    const = lambda i: (0, 0)
    return pl.pallas_call(
        _inproj_kernel,
        grid=(t // tm,),
        in_specs=[
            pl.BlockSpec((tm, D_MODEL), row),
            pl.BlockSpec((1, D_MODEL), const),
            pl.BlockSpec(w_in.shape, const),
            pl.BlockSpec(w_vt.shape, const),
        ],
        out_specs=[
            pl.BlockSpec((tm, ATTN_WIDTH), row),
            pl.BlockSpec((tm, ATTN_WIDTH), row),
            pl.BlockSpec((1, ATTN_WIDTH, tm), lambda i: (i, 0, 0)),
            pl.BlockSpec((tm, POOL_WIDTH), row),
        ],
        out_shape=[
            jax.ShapeDtypeStruct((t, ATTN_WIDTH), BF16),
            jax.ShapeDtypeStruct((t, ATTN_WIDTH), BF16),
            jax.ShapeDtypeStruct((t // tm, ATTN_WIDTH, tm), BF16),
            jax.ShapeDtypeStruct((t, POOL_WIDTH), F32),
        ],
        compiler_params=_params("parallel"),
        name="inproj",
    )(x, g, w_in, w_vt)


def _attn_kernel(far_ref, q_ref, qn_ref, k_ref, vt_ref, bias_ref, lq_ref, lk_ref, hn_ref, o_ref,
                 qs_scr, sa_scr, sb_scr, m_scr, l_scr, acc_scr, *, tile, nk, lam_init):
    h = pl.program_id(1)
    i = pl.program_id(2)
    t = tile
    cq = min(MXU_WIDTH, t)
    n_chunks = 2 * t // cq

    cur = lax.rem(i, 2)
    nxt = 1 - cur

    def stack_maps(slot, q):
        lane = lax.broadcasted_iota(jnp.int32, (t, LANES), 1)
        zero = jnp.zeros_like(q)
        qs_scr[slot, 0:t, :] = jnp.where(lane < DIFF_HEAD_DIM, q, zero)
        qs_scr[slot, t:2 * t, :] = jnp.where(lane >= DIFF_HEAD_DIM, q, zero)

    def scores(j, slot):
        start = pl.multiple_of(j * t, t)
        return lax.dot_general(k_ref[pl.ds(start, t), :], qs_scr[slot], NT_DIMS,
                               preferred_element_type=F32)

    n_pairs = nk // 2

    def plan(qi):
        first_near = lax.div(qi + 1, 2) - 1
        n_left = jnp.maximum(first_near, 0)
        right_start = first_near + 2
        n_far = n_left + jnp.maximum(n_pairs - right_start, 0)
        return n_left, right_start, n_far, n_left, jnp.minimum(first_near + 1, n_pairs - 1)

    def first_tile(qi):
        n_left_q, right_start_q, n_far_q, near_lo_q, _ = plan(qi)
        first_far = jnp.where(n_left_q > 0, 0, right_start_q)
        return 2 * jnp.where(n_far_q > 0, first_far, near_lo_q)

    @pl.when(i == 0)
    def _():
        stack_maps(0, q_ref[...])
        sa_scr[...] = scores(first_tile(0), 0)

    stack_maps(nxt, qn_ref[...])
    m_scr[...] = jnp.full(m_scr.shape, MASK_VALUE, F32)
    l_scr[...] = jnp.zeros(l_scr.shape, F32)
    acc_scr[...] = jnp.zeros(acc_scr.shape, F32)

    def softmax_pv(j, s_ref, near, c):
        vtj = vt_ref[j]
        for ci in range(n_chunks):
            cols = slice(ci * cq, (ci + 1) * cq)
            s = s_ref[:, cols]
            if near:
                q0 = (ci * cq) % t
                s = s + bias_ref[0, j - i + NEAR_REACH, :, q0:q0 + cq]
            m_prev = m_scr[:, cols]
            m_cur = jnp.max(s, axis=0, keepdims=True)
            if c is not None:
                m_cur = m_cur + c
            m_next = jnp.maximum(m_prev, m_cur)
            p = jnp.exp2(s - (m_next if c is None else m_next - c))
            alpha = jnp.exp2(m_prev - m_next)
            l_scr[:, cols] = alpha * l_scr[:, cols] + jnp.sum(p, axis=0, keepdims=True)
            acc_scr[:, cols] = alpha * acc_scr[:, cols] + jnp.dot(vtj, p.astype(BF16),
                                                                  preferred_element_type=F32)
            m_scr[:, cols] = m_next

    def pair(m, near, c, ahead_tile, ahead_slot):
        j0 = 2 * m
        sb_scr[...] = scores(j0 + 1, cur)
        softmax_pv(j0, sa_scr, near, c)
        sa_scr[...] = scores(ahead_tile, ahead_slot)
        softmax_pv(j0 + 1, sb_scr, near, c)

    n_left, right_start, n_far, near_lo, near_hi = plan(i)
    next_first = first_tile(jnp.minimum(i + 1, nk - 1))

    def far_pair(f):
        def index(g):
            return jnp.where(g < n_left, g, g - n_left + right_start)
        ahead = jnp.where(f + 1 < n_far, index(f + 1), near_lo)
        c = jnp.where(f < n_left, far_ref[h, 0], far_ref[h, 1])
        pair(index(f), False, c, 2 * ahead, cur)

    lead = jnp.where(n_far >= NEAR_LEAD, NEAR_LEAD, 0)
    n_loop = n_far - lead
    size = 1
    while size < FAR_BLOCK:
        @pl.when(jnp.bitwise_and(n_loop, size) != 0)
        def _(size=size):
            base = jnp.bitwise_and(n_loop, size - 1)
            for u in range(size):
                far_pair(base + u)
        size *= 2

    def far_block(u, carry):
        base = jnp.bitwise_and(n_loop, FAR_BLOCK - 1) + FAR_BLOCK * u
        for v in range(FAR_BLOCK):
            far_pair(base + v)
        return carry

    lax.fori_loop(0, lax.div(n_loop, FAR_BLOCK), far_block, 0)

    def near_block(n_lead, both):
        for v in range(n_lead):
            far_pair(n_loop + v)
        if both:
            pair(near_lo, True, None, 2 * near_hi, cur)
            pair(near_hi, True, None, next_first, nxt)
        else:
            pair(near_lo, True, None, next_first, nxt)

    for n_lead in sorted({0, NEAR_LEAD}):
        for both in (True, False):
            @pl.when(jnp.logical_and(lead == n_lead, (near_hi > near_lo) == both))
            def _(n_lead=n_lead, both=both):
                near_block(n_lead, both)

    o = acc_scr[...] * (1.0 / l_scr[...])
    prod = lq_ref[...] * lk_ref[...]
    lam = (jnp.exp(jnp.sum(prod[0:1], axis=1, keepdims=True))
           - jnp.exp(jnp.sum(prod[1:2], axis=1, keepdims=True)) + lam_init)
    a = (o[:, 0:t] - lam * o[:, t:2 * t]).T
    o_ref[...] = (_rms(a, hn_ref[...]) * (1.0 - lam_init)).astype(o_ref.dtype)


def _attention(q, k, vt, bias_near_t, far, lam_q, lam_k, head_norm, lam_init, tile, seq_len):
    t_all = q.shape[0]
    b = t_all // seq_len
    nk = seq_len // tile
    assert nk % 2 == 0, "key tiles are processed in pairs"
    kern = functools.partial(_attn_kernel, tile=tile, nk=nk, lam_init=lam_init)
    return pl.pallas_call(
        kern,
        grid=(b, N_DIFF_HEADS, nk),
        in_specs=[
            pl.BlockSpec(memory_space=pltpu.SMEM),
            pl.BlockSpec((tile, V_HEAD_DIM), lambda bi, h, i: (bi * nk + i, h)),
            pl.BlockSpec((tile, V_HEAD_DIM), lambda bi, h, i: (bi * nk + jnp.minimum(i + 1, nk - 1), h)),
            pl.BlockSpec((seq_len, V_HEAD_DIM), lambda bi, h, i: (bi, h)),
            pl.BlockSpec((nk, V_HEAD_DIM, tile), lambda bi, h, i: (bi, h, 0)),
            pl.BlockSpec((1, 2 * NEAR_REACH + 1, tile, tile), lambda bi, h, i: (h, 0, 0, 0)),
            pl.BlockSpec((2, DIFF_HEAD_DIM), lambda bi, h, i: (0, 0)),
            pl.BlockSpec((2, DIFF_HEAD_DIM), lambda bi, h, i: (0, 0)),
            pl.BlockSpec((1, V_HEAD_DIM), lambda bi, h, i: (0, 0)),
        ],
        out_specs=pl.BlockSpec((tile, V_HEAD_DIM), lambda bi, h, i: (bi * nk + i, h)),
        out_shape=jax.ShapeDtypeStruct((t_all, ATTN_WIDTH), BF16),
        scratch_shapes=[
            pltpu.VMEM((2, 2 * tile, V_HEAD_DIM), BF16),
            pltpu.VMEM((tile, 2 * tile), F32),
            pltpu.VMEM((tile, 2 * tile), F32),
            pltpu.VMEM((1, 2 * tile), F32),
            pltpu.VMEM((1, 2 * tile), F32),
            pltpu.VMEM((V_HEAD_DIM, 2 * tile), F32),
        ],
        compiler_params=_params("parallel", "parallel", "arbitrary"),
        name="diff_attention",
    )(far, q, q, k, vt, bias_near_t, lam_q, lam_k, head_norm)


def _mixout_kernel(a_ref, p_ref, pprev_ref, pnext_ref, x_ref, wout_ref, wpool_ref, ps_ref, g_ref,
                   o_ref, pbuf, *, tm, tiles_per_seq, seq_len):
    ti = lax.rem(pl.program_id(0), tiles_per_seq)
    keep_prev = (ti > 0).astype(F32)
    keep_next = (ti < tiles_per_seq - 1).astype(F32)
    pbuf[0:HALO, :] = pprev_ref[...] * keep_prev
    pbuf[HALO:HALO + tm, :] = p_ref[...]
    pbuf[HALO + tm:, :] = pnext_ref[...] * keep_next

    pos = ti * tm + lax.broadcasted_iota(jnp.int32, (tm, 1), 0)
    mixed = []
    for g, win in enumerate(POOL_WINDOWS):
        lo_off = -(win // 2)
        hi_off = win - win // 2 - 1
        cols = slice(g * POOL_GROUP, (g + 1) * POOL_GROUP)
        total = pbuf[HALO + lo_off:HALO + lo_off + tm, cols]
        for off in range(lo_off + 1, hi_off + 1):
            total = total + pbuf[HALO + off:HALO + off + tm, cols]
        lo = jnp.maximum(pos + lo_off, 0)
        hi = jnp.minimum(pos + hi_off, seq_len - 1)
        cnt = (hi - lo + 1).astype(F32)
        y = total / cnt - pbuf[HALO:HALO + tm, cols]
        mixed.append(jnp.dot(y.astype(BF16), wpool_ref[g], preferred_element_type=F32))
    m = jnp.concatenate(mixed, axis=1) * ps_ref[...]

    o = jnp.dot(a_ref[...], wout_ref[0:ATTN_WIDTH, :], preferred_element_type=F32)
    o = o + jnp.dot(m.astype(BF16), wout_ref[ATTN_WIDTH:, :], preferred_element_type=F32)
    o_ref[...] = x_ref[...] + _rms(o, g_ref[...])


def _halo_specs(tm, width, n_rows):
    per_tile = tm // HALO
    last = n_rows // HALO - 1
    prev = pl.BlockSpec((HALO, width), lambda i, *_: (jnp.maximum(i * per_tile - 1, 0), 0))
    nxt = pl.BlockSpec((HALO, width), lambda i, *_: (jnp.minimum((i + 1) * per_tile, last), 0))
    return prev, nxt


def _mixout(a, p, x, w_out, w_pool, pool_scale, g, seq_len, tm):
    t = x.shape[0]
    row = lambda i: (i, 0)
    const = lambda i: (0, 0)
    prev, nxt = _halo_specs(tm, POOL_WIDTH, t)
    kern = functools.partial(_mixout_kernel, tm=tm, tiles_per_seq=seq_len // tm, seq_len=seq_len)
    return pl.pallas_call(
        kern,
        grid=(t // tm,),
        in_specs=[
            pl.BlockSpec((tm, ATTN_WIDTH), row),
            pl.BlockSpec((tm, POOL_WIDTH), row),
            prev,
            nxt,
            pl.BlockSpec((tm, D_MODEL), row),
            pl.BlockSpec(w_out.shape, const),
            pl.BlockSpec(w_pool.shape, lambda i: (0, 0, 0)),
            pl.BlockSpec((1, POOL_WIDTH), const),
            pl.BlockSpec((1, D_MODEL), const),
        ],
        out_specs=pl.BlockSpec((tm, D_MODEL), row),
        out_shape=jax.ShapeDtypeStruct((t, D_MODEL), F32),
        scratch_shapes=[pltpu.VMEM((tm + 2 * HALO, POOL_WIDTH), F32)],
        compiler_params=_params("parallel"),
        name="pool_outproj",
    )(a, p, p, p, x, w_out, w_pool, pool_scale, g)


def _gelu_tanh(x):
    return 0.5 * x * (1.0 + jnp.tanh(math.sqrt(2.0 / math.pi) * (x + 0.044715 * (x * x * x))))


def _ffn_kernel(x_ref, xprev_ref, xnext_ref, gpre_ref, wg_ref, wv_ref, cwg_ref, cwv_ref, cbg_ref, cbv_ref,
                wd_ref, gpost_ref, o_ref, h_scr, u_scr, acc_scr, *, tm, tiles_per_seq):
    c = pl.program_id(1)

    @pl.when(c == 0)
    def _():
        ti = lax.rem(pl.program_id(0), tiles_per_seq)
        keep_prev = (ti > 0).astype(F32)
        keep_next = (ti < tiles_per_seq - 1).astype(F32)
        g = gpre_ref[...]
        h_scr[0:HALO, :] = _rms(xprev_ref[...], g) * keep_prev
        h_scr[HALO:HALO + tm, :] = _rms(x_ref[...], g)
        h_scr[HALO + tm:, :] = _rms(xnext_ref[...], g) * keep_next
        acc_scr[...] = jnp.zeros(acc_scr.shape, F32)

    h = h_scr[...].astype(BF16)

    def conv(w_ref, cw_ref, cb_ref):
        u_scr[...] = jnp.dot(h, w_ref[...], preferred_element_type=F32)
        cw = cw_ref[...]
        return (u_scr[HALO - 1:HALO - 1 + tm, :] * cw[0:1]
                + u_scr[HALO:HALO + tm, :] * cw[1:2]
                + u_scr[HALO + 1:HALO + 1 + tm, :] * cw[2:3]
                + cb_ref[...])

    gate = conv(wg_ref, cwg_ref, cbg_ref)
    val = conv(wv_ref, cwv_ref, cbv_ref)
    f = (_gelu_tanh(gate) * val).astype(BF16)
    acc_scr[...] += jnp.dot(f, wd_ref[...], preferred_element_type=F32)

    @pl.when(c == pl.num_programs(1) - 1)
    def _():
        o_ref[...] = x_ref[...] + _rms(acc_scr[...], gpost_ref[...])


def _ffn(x, g_pre, w_up, conv_w, conv_b, w_down, g_post, seq_len, tm):
    t = x.shape[0]
    n_chunks = D_FF // FF_CHUNK
    row = lambda i, c: (i, 0)
    const = lambda i, c: (0, 0)
    gate_cols = lambda i, c: (0, c)
    val_cols = lambda i, c: (0, n_chunks + c)
    prev, nxt = _halo_specs(tm, D_MODEL, t)
    kern = functools.partial(_ffn_kernel, tm=tm, tiles_per_seq=seq_len // tm)
    return pl.pallas_call(
        kern,
        grid=(t // tm, n_chunks),
        in_specs=[
            pl.BlockSpec((tm, D_MODEL), row),
            prev,
            nxt,
            pl.BlockSpec((1, D_MODEL), const),
            pl.BlockSpec((D_MODEL, FF_CHUNK), gate_cols),
            pl.BlockSpec((D_MODEL, FF_CHUNK), val_cols),
            pl.BlockSpec((3, FF_CHUNK), gate_cols),
            pl.BlockSpec((3, FF_CHUNK), val_cols),
            pl.BlockSpec((1, FF_CHUNK), gate_cols),
            pl.BlockSpec((1, FF_CHUNK), val_cols),
            pl.BlockSpec((FF_CHUNK, D_MODEL), lambda i, c: (c, 0)),
            pl.BlockSpec((1, D_MODEL), const),
        ],
        out_specs=pl.BlockSpec((tm, D_MODEL), row),
        out_shape=jax.ShapeDtypeStruct((t, D_MODEL), F32),
        scratch_shapes=[
            pltpu.VMEM((tm + 2 * HALO, D_MODEL), F32),
            pltpu.VMEM((tm + 2 * HALO, FF_CHUNK), F32),
            pltpu.VMEM((tm, D_MODEL), F32),
        ],
        compiler_params=_params("parallel", "arbitrary"),
        name="conv_mlp",
    )(x, x, x, g_pre, w_up, w_up, conv_w, conv_w, conv_b, conv_b, w_down, g_post)


def _rel_bucket(rel):
    nb = NUM_BUCKETS // 2
    max_exact = nb // 2
    ret = jnp.where(rel > 0, nb, 0)
    n = jnp.abs(rel)
    nf = jnp.maximum(n, 1).astype(F32)
    large = max_exact + (jnp.log(nf / max_exact) / math.log(MAX_DISTANCE / max_exact)
                         * (nb - max_exact)).astype(jnp.int32)
    large = jnp.minimum(large, nb - 1)
    return ret + jnp.where(n < max_exact, n, large)


def _bias_expand_kernel(y_ref, o_ref, *, tile):
    span = y_ref.shape[-1]
    table = jnp.broadcast_to(y_ref[0], (tile, span))
    skewed = pltpu.roll(table, 0, 1, stride=1, stride_axis=0)
    for di, d in enumerate(range(-NEAR_REACH, NEAR_REACH + 1)):
        start = (-d * tile) % span
        o_ref[0, di] = skewed[:, start:start + tile]


def _bias_tables(rel_bias, tile):
    assert tile >= MAX_DISTANCE, "far tiles must lie beyond the last distance bucket"
    n_near = 2 * NEAR_REACH + 1
    span = 2 * (NEAR_REACH + 1) * tile
    m = jnp.arange(span, dtype=jnp.int32)
    rel = jnp.where(m < span // 2, -m, span - m)
    rel_bias = rel_bias.astype(F32) * LOG2E
    by_rel = rel_bias[_rel_bucket(rel)].T[:, None, :]
    near = pl.pallas_call(
        functools.partial(_bias_expand_kernel, tile=tile),
        grid=(N_DIFF_HEADS,),
        in_specs=[pl.BlockSpec((1, 1, span), lambda h: (h, 0, 0))],
        out_specs=pl.BlockSpec((1, n_near, tile, tile), lambda h: (h, 0, 0, 0)),
        out_shape=jax.ShapeDtypeStruct((N_DIFF_HEADS, n_near, tile, tile), F32),
        compiler_params=_params("parallel"),
        name="bias_expand",
    )(by_rel)
    far = jnp.stack([rel_bias[NUM_BUCKETS // 2 - 1], rel_bias[NUM_BUCKETS - 1]], axis=1)
    return near, far


def kernel(x_prompt, x_sample, rel_bias, ln_mix_pre, ln_mix_post, w_in, lam_q, lam_k, head_norm, w_pool,
           pool_scale, w_out, ln_ffn_pre, ln_ffn_post, w_up, conv_w, conv_b, w_down):
    depth = w_in.shape[0]
    w_in_b, w_out_b, w_pool_b = w_in.astype(BF16), w_out.astype(BF16), w_pool.astype(BF16)
    w_up_b, w_down_b = w_up.astype(BF16), w_down.astype(BF16)
    w_vt_b = jnp.swapaxes(w_in_b[:, :, 2 * ATTN_WIDTH:3 * ATTN_WIDTH], 1, 2)
    tables = {}

    def trunk(x):
        b, s, d = x.shape
        tile = min(ATTN_TILE, s)
        tm = min(ROW_TILE, s)
        assert s % tile == 0 and s % tm == 0 and tm % HALO == 0
        if tile not in tables:
            tables[tile] = _bias_tables(rel_bias, tile)
        bias_near_t, far = tables[tile]
        xf = x.reshape(b * s, d)
        for i in range(depth):
            lam_init = 0.8 - 0.6 * math.exp(-0.3 * i)
            q, k, vt, p = _inproj(xf, ln_mix_pre[i][None], w_in_b[i], w_vt_b[i], tile)
            a = _attention(q, k, vt, bias_near_t, far, lam_q[i], lam_k[i], head_norm[i][None], lam_init,
                           tile, s)
            xf = _mixout(a, p, xf, w_out_b[i], w_pool_b[i], pool_scale[i][None], ln_mix_post[i][None], s, tm)
            xf = _ffn(xf, ln_ffn_pre[i][None], w_up_b[i], conv_w[i], conv_b[i][None], w_down_b[i],
                      ln_ffn_post[i][None], s, tm)
        return xf.reshape(b, s, d)

    return (trunk(x_prompt), trunk(x_sample))
```

```python
import functools
import math

import jax
import jax.numpy as jnp
from jax import lax
from jax.experimental import pallas as pl
from jax.experimental.pallas import tpu as pltpu

D_MODEL = 1024
ATTN_WIDTH = 512
POOL_WIDTH = D_MODEL - ATTN_WIDTH
N_DIFF_HEADS = 4
DIFF_HEAD_DIM = 64
V_HEAD_DIM = 2 * DIFF_HEAD_DIM
POOL_WINDOWS = (2, 4, 8, 16)
POOL_GROUP = POOL_WIDTH // len(POOL_WINDOWS)
D_FF = 2816
NUM_BUCKETS = 32
MAX_DISTANCE = 128
RMS_EPS = 1e-6
QK_SCALE = DIFF_HEAD_DIM ** -0.5
LOG2E = math.log2(math.e)

LANES = 128
SUBLANES = 8
MXU_WIDTH = 256
ATTN_TILE = 512
ROW_TILE = 512
FF_CHUNK = D_FF // 2
HALO = SUBLANES
FAR_BLOCK = 4
NEAR_LEAD = 6
NEAR_REACH = 2
MASK_VALUE = -1e30
VMEM_LIMIT_BYTES = 48 * 1024 * 1024

F32 = jnp.float32
BF16 = jnp.bfloat16
NT_DIMS = (((1,), (1,)), ((), ()))


def _rms(x, g):
    var = jnp.mean(x * x, axis=-1, keepdims=True)
    return x * lax.rsqrt(var + RMS_EPS) * g


def _params(*semantics):
    return pltpu.CompilerParams(dimension_semantics=semantics, vmem_limit_bytes=VMEM_LIMIT_BYTES)


def _inproj_kernel(x_ref, g_ref, w_ref, wvt_ref, q_ref, k_ref, vt_ref, p_ref):
    h = _rms(x_ref[...], g_ref[...]).astype(BF16)
    a = ATTN_WIDTH
    q = jnp.dot(h, w_ref[:, 0:a], preferred_element_type=F32)
    q_ref[...] = (q * (QK_SCALE * LOG2E)).astype(BF16)
    k_ref[...] = jnp.dot(h, w_ref[:, a:2 * a], preferred_element_type=F32).astype(BF16)
    p_ref[...] = jnp.dot(h, w_ref[:, 3 * a:], preferred_element_type=F32)
    vt_ref[0] = lax.dot_general(wvt_ref[...], h, NT_DIMS, preferred_element_type=F32).astype(BF16)


def _inproj(x, g, w_in, w_vt, tm):
    t = x.shape[0]
    row = lambda i: (i, 0)
    const = lambda i: (0, 0)
    return pl.pallas_call(
        _inproj_kernel,
        grid=(t // tm,),
        in_specs=[
            pl.BlockSpec((tm, D_MODEL), row),
            pl.BlockSpec((1, D_MODEL), const),
            pl.BlockSpec(w_in.shape, const),
            pl.BlockSpec(w_vt.shape, const),
        ],
        out_specs=[
            pl.BlockSpec((tm, ATTN_WIDTH), row),
            pl.BlockSpec((tm, ATTN_WIDTH), row),
            pl.BlockSpec((1, ATTN_WIDTH, tm), lambda i: (i, 0, 0)),
            pl.BlockSpec((tm, POOL_WIDTH), row),
        ],
        out_shape=[
            jax.ShapeDtypeStruct((t, ATTN_WIDTH), BF16),
            jax.ShapeDtypeStruct((t, ATTN_WIDTH), BF16),
            jax.ShapeDtypeStruct((t // tm, ATTN_WIDTH, tm), BF16),
            jax.ShapeDtypeStruct((t, POOL_WIDTH), F32),
        ],
        compiler_params=_params("parallel"),
        name="inproj",
    )(x, g, w_in, w_vt)


def _attn_kernel(far_ref, q_ref, qn_ref, k_ref, vt_ref, bias_ref, lq_ref, lk_ref, hn_ref, o_ref,
                 qs_scr, sa_scr, sb_scr, m_scr, l_scr, acc_scr, *, tile, nk, lam_init):
    h = pl.program_id(1)
    i = pl.program_id(2)
    t = tile
    cq = min(MXU_WIDTH, t)
    n_chunks = 2 * t // cq

    cur = lax.rem(i, 2)
    nxt = 1 - cur

    def stack_maps(slot, q):
        lane = lax.broadcasted_iota(jnp.int32, (t, LANES), 1)
        zero = jnp.zeros_like(q)
        qs_scr[slot, 0:t, :] = jnp.where(lane < DIFF_HEAD_DIM, q, zero)
        qs_scr[slot, t:2 * t, :] = jnp.where(lane >= DIFF_HEAD_DIM, q, zero)

    def scores(j, slot):
        start = pl.multiple_of(j * t, t)
        return lax.dot_general(k_ref[pl.ds(start, t), :], qs_scr[slot], NT_DIMS,
                               preferred_element_type=F32)

    n_pairs = nk // 2

    def plan(qi):
        first_near = lax.div(qi + 1, 2) - 1
        n_left = jnp.maximum(first_near, 0)
        right_start = first_near + 2
        n_far = n_left + jnp.maximum(n_pairs - right_start, 0)
        return n_left, right_start, n_far, n_left, jnp.minimum(first_near + 1, n_pairs - 1)

    def first_tile(qi):
        n_left_q, right_start_q, n_far_q, near_lo_q, _ = plan(qi)
        first_far = jnp.where(n_left_q > 0, 0, right_start_q)
        return 2 * jnp.where(n_far_q > 0, first_far, near_lo_q)

    @pl.when(i == 0)
    def _():
        stack_maps(0, q_ref[...])
        sa_scr[...] = scores(first_tile(0), 0)

    stack_maps(nxt, qn_ref[...])
    m_scr[...] = jnp.full(m_scr.shape, MASK_VALUE, F32)
    l_scr[...] = jnp.zeros(l_scr.shape, F32)
    acc_scr[...] = jnp.zeros(acc_scr.shape, F32)

    def softmax_pv(j, s_ref, near, c):
        vtj = vt_ref[j]
        for ci in range(n_chunks):
            cols = slice(ci * cq, (ci + 1) * cq)
            s = s_ref[:, cols]
            if near:
                q0 = (ci * cq) % t
                s = s + bias_ref[0, j - i + NEAR_REACH, :, q0:q0 + cq]
            m_prev = m_scr[:, cols]
            m_cur = jnp.max(s, axis=0, keepdims=True)
            if c is not None:
                m_cur = m_cur + c
            m_next = jnp.maximum(m_prev, m_cur)
            p = jnp.exp2(s - (m_next if c is None else m_next - c))
            alpha = jnp.exp2(m_prev - m_next)
            l_scr[:, cols] = alpha * l_scr[:, cols] + jnp.sum(p, axis=0, keepdims=True)
            acc_scr[:, cols] = alpha * acc_scr[:, cols] + jnp.dot(vtj, p.astype(BF16),
                                                                  preferred_element_type=F32)
            m_scr[:, cols] = m_next

    def pair(m, near, c, ahead_tile, ahead_slot):
        j0 = 2 * m
        sb_scr[...] = scores(j0 + 1, cur)
        softmax_pv(j0, sa_scr, near, c)
        sa_scr[...] = scores(ahead_tile, ahead_slot)
        softmax_pv(j0 + 1, sb_scr, near, c)

    n_left, right_start, n_far, near_lo, near_hi = plan(i)
    next_first = first_tile(jnp.minimum(i + 1, nk - 1))

    def far_pair(f):
        def index(g):
            return jnp.where(g < n_left, g, g - n_left + right_start)
        ahead = jnp.where(f + 1 < n_far, index(f + 1), near_lo)
        c = jnp.where(f < n_left, far_ref[h, 0], far_ref[h, 1])
        pair(index(f), False, c, 2 * ahead, cur)

    lead = jnp.where(n_far >= NEAR_LEAD, NEAR_LEAD, 0)
    n_loop = n_far - lead
    size = 1
    while size < FAR_BLOCK:
        @pl.when(jnp.bitwise_and(n_loop, size) != 0)
        def _(size=size):
            base = jnp.bitwise_and(n_loop, size - 1)
            for u in range(size):
                far_pair(base + u)
        size *= 2

    def far_block(u, carry):
        base = jnp.bitwise_and(n_loop, FAR_BLOCK - 1) + FAR_BLOCK * u
        for v in range(FAR_BLOCK):
            far_pair(base + v)
        return carry

    lax.fori_loop(0, lax.div(n_loop, FAR_BLOCK), far_block, 0)

    def near_block(n_lead, both):
        for v in range(n_lead):
            far_pair(n_loop + v)
        if both:
            pair(near_lo, True, None, 2 * near_hi, cur)
            pair(near_hi, True, None, next_first, nxt)
        else:
            pair(near_lo, True, None, next_first, nxt)

    for n_lead in sorted({0, NEAR_LEAD}):
        for both in (True, False):
            @pl.when(jnp.logical_and(lead == n_lead, (near_hi > near_lo) == both))
            def _(n_lead=n_lead, both=both):
                near_block(n_lead, both)

    o = acc_scr[...] * (1.0 / l_scr[...])
    prod = lq_ref[...] * lk_ref[...]
    lam = (jnp.exp(jnp.sum(prod[0:1], axis=1, keepdims=True))
           - jnp.exp(jnp.sum(prod[1:2], axis=1, keepdims=True)) + lam_init)
    a = (o[:, 0:t] - lam * o[:, t:2 * t]).T
    o_ref[...] = (_rms(a, hn_ref[...]) * (1.0 - lam_init)).astype(o_ref.dtype)


def _attention(q, k, vt, bias_near_t, far, lam_q, lam_k, head_norm, lam_init, tile, seq_len):
    t_all = q.shape[0]
    b = t_all // seq_len
    nk = seq_len // tile
    assert nk % 2 == 0, "key tiles are processed in pairs"
    kern = functools.partial(_attn_kernel, tile=tile, nk=nk, lam_init=lam_init)
    return pl.pallas_call(
        kern,
        grid=(b, N_DIFF_HEADS, nk),
        in_specs=[
            pl.BlockSpec(memory_space=pltpu.SMEM),
            pl.BlockSpec((tile, V_HEAD_DIM), lambda bi, h, i: (bi * nk + i, h)),
            pl.BlockSpec((tile, V_HEAD_DIM), lambda bi, h, i: (bi * nk + jnp.minimum(i + 1, nk - 1), h)),
            pl.BlockSpec((seq_len, V_HEAD_DIM), lambda bi, h, i: (bi, h)),
            pl.BlockSpec((nk, V_HEAD_DIM, tile), lambda bi, h, i: (bi, h, 0)),
            pl.BlockSpec((1, 2 * NEAR_REACH + 1, tile, tile), lambda bi, h, i: (h, 0, 0, 0)),
            pl.BlockSpec((2, DIFF_HEAD_DIM), lambda bi, h, i: (0, 0)),
            pl.BlockSpec((2, DIFF_HEAD_DIM), lambda bi, h, i: (0, 0)),
            pl.BlockSpec((1, V_HEAD_DIM), lambda bi, h, i: (0, 0)),
        ],
        out_specs=pl.BlockSpec((tile, V_HEAD_DIM), lambda bi, h, i: (bi * nk + i, h)),
        out_shape=jax.ShapeDtypeStruct((t_all, ATTN_WIDTH), BF16),
        scratch_shapes=[
            pltpu.VMEM((2, 2 * tile, V_HEAD_DIM), BF16),
            pltpu.VMEM((tile, 2 * tile), F32),
            pltpu.VMEM((tile, 2 * tile), F32),
            pltpu.VMEM((1, 2 * tile), F32),
            pltpu.VMEM((1, 2 * tile), F32),
            pltpu.VMEM((V_HEAD_DIM, 2 * tile), F32),
        ],
        compiler_params=_params("parallel", "parallel", "arbitrary"),
        name="diff_attention",
    )(far, q, q, k, vt, bias_near_t, lam_q, lam_k, head_norm)


def _mixout_kernel(a_ref, p_ref, pprev_ref, pnext_ref, x_ref, wout_ref, wpool_ref, ps_ref, g_ref,
                   o_ref, pbuf, *, tm, tiles_per_seq, seq_len):
    ti = lax.rem(pl.program_id(0), tiles_per_seq)
    keep_prev = (ti > 0).astype(F32)
    keep_next = (ti < tiles_per_seq - 1).astype(F32)
    pbuf[0:HALO, :] = pprev_ref[...] * keep_prev
    pbuf[HALO:HALO + tm, :] = p_ref[...]
    pbuf[HALO + tm:, :] = pnext_ref[...] * keep_next

    pos = ti * tm + lax.broadcasted_iota(jnp.int32, (tm, 1), 0)
    mixed = []
    for g, win in enumerate(POOL_WINDOWS):
        lo_off = -(win // 2)
        hi_off = win - win // 2 - 1
        cols = slice(g * POOL_GROUP, (g + 1) * POOL_GROUP)
        total = pbuf[HALO + lo_off:HALO + lo_off + tm, cols]
        for off in range(lo_off + 1, hi_off + 1):
            total = total + pbuf[HALO + off:HALO + off + tm, cols]
        lo = jnp.maximum(pos + lo_off, 0)
        hi = jnp.minimum(pos + hi_off, seq_len - 1)
        cnt = (hi - lo + 1).astype(F32)
        y = total / cnt - pbuf[HALO:HALO + tm, cols]
        mixed.append(jnp.dot(y.astype(BF16), wpool_ref[g], preferred_element_type=F32))
    m = jnp.concatenate(mixed, axis=1) * ps_ref[...]

    o = jnp.dot(a_ref[...], wout_ref[0:ATTN_WIDTH, :], preferred_element_type=F32)
    o = o + jnp.dot(m.astype(BF16), wout_ref[ATTN_WIDTH:, :], preferred_element_type=F32)
    o_ref[...] = x_ref[...] + _rms(o, g_ref[...])


def _halo_specs(tm, width, n_rows):
    per_tile = tm // HALO
    last = n_rows // HALO - 1
    prev = pl.BlockSpec((HALO, width), lambda i, *_: (jnp.maximum(i * per_tile - 1, 0), 0))
    nxt = pl.BlockSpec((HALO, width), lambda i, *_: (jnp.minimum((i + 1) * per_tile, last), 0))
    return prev, nxt


def _mixout(a, p, x, w_out, w_pool, pool_scale, g, seq_len, tm):
    t = x.shape[0]
    row = lambda i: (i, 0)
    const = lambda i: (0, 0)
    prev, nxt = _halo_specs(tm, POOL_WIDTH, t)
    kern = functools.partial(_mixout_kernel, tm=tm, tiles_per_seq=seq_len // tm, seq_len=seq_len)
    return pl.pallas_call(
        kern,
        grid=(t // tm,),
        in_specs=[
            pl.BlockSpec((tm, ATTN_WIDTH), row),
            pl.BlockSpec((tm, POOL_WIDTH), row),
            prev,
            nxt,
            pl.BlockSpec((tm, D_MODEL), row),
            pl.BlockSpec(w_out.shape, const),
            pl.BlockSpec(w_pool.shape, lambda i: (0, 0, 0)),
            pl.BlockSpec((1, POOL_WIDTH), const),
            pl.BlockSpec((1, D_MODEL), const),
        ],
        out_specs=pl.BlockSpec((tm, D_MODEL), row),
        out_shape=jax.ShapeDtypeStruct((t, D_MODEL), F32),
        scratch_shapes=[pltpu.VMEM((tm + 2 * HALO, POOL_WIDTH), F32)],
        compiler_params=_params("parallel"),
        name="pool_outproj",
    )(a, p, p, p, x, w_out, w_pool, pool_scale, g)


def _gelu_tanh(x):
    return 0.5 * x * (1.0 + jnp.tanh(math.sqrt(2.0 / math.pi) * (x + 0.044715 * (x * x * x))))


def _ffn_kernel(x_ref, xprev_ref, xnext_ref, gpre_ref, wg_ref, wv_ref, cwg_ref, cwv_ref, cbg_ref, cbv_ref,
                wd_ref, gpost_ref, o_ref, h_scr, u_scr, acc_scr, *, tm, tiles_per_seq):
    c = pl.program_id(1)

    @pl.when(c == 0)
    def _():
        ti = lax.rem(pl.program_id(0), tiles_per_seq)
        keep_prev = (ti > 0).astype(F32)
        keep_next = (ti < tiles_per_seq - 1).astype(F32)
        g = gpre_ref[...]
        h_scr[0:HALO, :] = _rms(xprev_ref[...], g) * keep_prev
        h_scr[HALO:HALO + tm, :] = _rms(x_ref[...], g)
        h_scr[HALO + tm:, :] = _rms(xnext_ref[...], g) * keep_next
        acc_scr[...] = jnp.zeros(acc_scr.shape, F32)

    h = h_scr[...].astype(BF16)

    def conv(w_ref, cw_ref, cb_ref):
        u_scr[...] = jnp.dot(h, w_ref[...], preferred_element_type=F32)
        cw = cw_ref[...]
        return (u_scr[HALO - 1:HALO - 1 + tm, :] * cw[0:1]
                + u_scr[HALO:HALO + tm, :] * cw[1:2]
                + u_scr[HALO + 1:HALO + 1 + tm, :] * cw[2:3]
                + cb_ref[...])

    gate = conv(wg_ref, cwg_ref, cbg_ref)
    val = conv(wv_ref, cwv_ref, cbv_ref)
    f = (_gelu_tanh(gate) * val).astype(BF16)
    acc_scr[...] += jnp.dot(f, wd_ref[...], preferred_element_type=F32)

    @pl.when(c == pl.num_programs(1) - 1)
    def _():
        o_ref[...] = x_ref[...] + _rms(acc_scr[...], gpost_ref[...])


def _ffn(x, g_pre, w_up, conv_w, conv_b, w_down, g_post, seq_len, tm):
    t = x.shape[0]
    n_chunks = D_FF // FF_CHUNK
    row = lambda i, c: (i, 0)
    const = lambda i, c: (0, 0)
    gate_cols = lambda i, c: (0, c)
    val_cols = lambda i, c: (0, n_chunks + c)
    prev, nxt = _halo_specs(tm, D_MODEL, t)
    kern = functools.partial(_ffn_kernel, tm=tm, tiles_per_seq=seq_len // tm)
    return pl.pallas_call(
        kern,
        grid=(t // tm, n_chunks),
        in_specs=[
            pl.BlockSpec((tm, D_MODEL), row),
            prev,
            nxt,
            pl.BlockSpec((1, D_MODEL), const),
            pl.BlockSpec((D_MODEL, FF_CHUNK), gate_cols),
            pl.BlockSpec((D_MODEL, FF_CHUNK), val_cols),
            pl.BlockSpec((3, FF_CHUNK), gate_cols),
            pl.BlockSpec((3, FF_CHUNK), val_cols),
            pl.BlockSpec((1, FF_CHUNK), gate_cols),
            pl.BlockSpec((1, FF_CHUNK), val_cols),
            pl.BlockSpec((FF_CHUNK, D_MODEL), lambda i, c: (c, 0)),
            pl.BlockSpec((1, D_MODEL), const),
        ],
        out_specs=pl.BlockSpec((tm, D_MODEL), row),
        out_shape=jax.ShapeDtypeStruct((t, D_MODEL), F32),
        scratch_shapes=[
            pltpu.VMEM((tm + 2 * HALO, D_MODEL), F32),
            pltpu.VMEM((tm + 2 * HALO, FF_CHUNK), F32),
            pltpu.VMEM((tm, D_MODEL), F32),
        ],
        compiler_params=_params("parallel", "arbitrary"),
        name="conv_mlp",
    )(x, x, x, g_pre, w_up, w_up, conv_w, conv_w, conv_b, conv_b, w_down, g_post)


def _rel_bucket(rel):
    nb = NUM_BUCKETS // 2
    max_exact = nb // 2
    ret = jnp.where(rel > 0, nb, 0)
    n = jnp.abs(rel)
    nf = jnp.maximum(n, 1).astype(F32)
    large = max_exact + (jnp.log(nf / max_exact) / math.log(MAX_DISTANCE / max_exact)
                         * (nb - max_exact)).astype(jnp.int32)
    large = jnp.minimum(large, nb - 1)
    return ret + jnp.where(n < max_exact, n, large)


def _bias_expand_kernel(y_ref, o_ref, *, tile):
    span = y_ref.shape[-1]
    table = jnp.broadcast_to(y_ref[0], (tile, span))
    skewed = pltpu.roll(table, 0, 1, stride=1, stride_axis=0)
    for di, d in enumerate(range(-NEAR_REACH, NEAR_REACH + 1)):
        start = (-d * tile) % span
        o_ref[0, di] = skewed[:, start:start + tile]


def _bias_tables(rel_bias, tile):
    assert tile >= MAX_DISTANCE, "far tiles must lie beyond the last distance bucket"
    n_near = 2 * NEAR_REACH + 1
    span = 2 * (NEAR_REACH + 1) * tile
    m = jnp.arange(span, dtype=jnp.int32)
    rel = jnp.where(m < span // 2, -m, span - m)
    rel_bias = rel_bias.astype(F32) * LOG2E
    by_rel = rel_bias[_rel_bucket(rel)].T[:, None, :]
    near = pl.pallas_call(
        functools.partial(_bias_expand_kernel, tile=tile),
        grid=(N_DIFF_HEADS,),
        in_specs=[pl.BlockSpec((1, 1, span), lambda h: (h, 0, 0))],
        out_specs=pl.BlockSpec((1, n_near, tile, tile), lambda h: (h, 0, 0, 0)),
        out_shape=jax.ShapeDtypeStruct((N_DIFF_HEADS, n_near, tile, tile), F32),
        compiler_params=_params("parallel"),
        name="bias_expand",
    )(by_rel)
    far = jnp.stack([rel_bias[NUM_BUCKETS // 2 - 1], rel_bias[NUM_BUCKETS - 1]], axis=1)
    return near, far


def kernel(x_prompt, x_sample, rel_bias, ln_mix_pre, ln_mix_post, w_in, lam_q, lam_k, head_norm, w_pool,
           pool_scale, w_out, ln_ffn_pre, ln_ffn_post, w_up, conv_w, conv_b, w_down):
    depth = w_in.shape[0]
    w_in_b, w_out_b, w_pool_b = w_in.astype(BF16), w_out.astype(BF16), w_pool.astype(BF16)
    w_up_b, w_down_b = w_up.astype(BF16), w_down.astype(BF16)
    w_vt_b = jnp.swapaxes(w_in_b[:, :, 2 * ATTN_WIDTH:3 * ATTN_WIDTH], 1, 2)
    tables = {}

    def trunk(x):
        b, s, d = x.shape
        tile = min(ATTN_TILE, s)
        tm = min(ROW_TILE, s)
        assert s % tile == 0 and s % tm == 0 and tm % HALO == 0
        if tile not in tables:
            tables[tile] = _bias_tables(rel_bias, tile)
        bias_near_t, far = tables[tile]
        xf = x.reshape(b * s, d)
        for i in range(depth):
            lam_init = 0.8 - 0.6 * math.exp(-0.3 * i)
            q, k, vt, p = _inproj(xf, ln_mix_pre[i][None], w_in_b[i], w_vt_b[i], tile)
            a = _attention(q, k, vt, bias_near_t, far, lam_q[i], lam_k[i], head_norm[i][None], lam_init,
                           tile, s)
            xf = _mixout(a, p, xf, w_out_b[i], w_pool_b[i], pool_scale[i][None], ln_mix_post[i][None], s, tm)
            xf = _ffn(xf, ln_ffn_pre[i][None], w_up_b[i], conv_w[i], conv_b[i][None], w_down_b[i],
                      ln_ffn_post[i][None], s, tm)
        return xf.reshape(b, s, d)

    return (trunk(x_prompt), trunk(x_sample))
```

```python
import functools
import math

import jax
import jax.numpy as jnp
from jax import lax
from jax.experimental import pallas as pl
from jax.experimental.pallas import tpu as pltpu

D_MODEL = 1024
ATTN_WIDTH = 512
POOL_WIDTH = D_MODEL - ATTN_WIDTH
N_DIFF_HEADS = 4
DIFF_HEAD_DIM = 64
V_HEAD_DIM = 2 * DIFF_HEAD_DIM
POOL_WINDOWS = (2, 4, 8, 16)
POOL_GROUP = POOL_WIDTH // len(POOL_WINDOWS)
D_FF = 2816
NUM_BUCKETS = 32
MAX_DISTANCE = 128
RMS_EPS = 1e-6
QK_SCALE = DIFF_HEAD_DIM ** -0.5
LOG2E = math.log2(math.e)

LANES = 128
SUBLANES = 8
MXU_WIDTH = 256
ATTN_TILE = 512
ROW_TILE = 512
FF_CHUNK = D_FF // 2
HALO = SUBLANES
FAR_BLOCK = 4
NEAR_LEAD = 6
NEAR_REACH = 2
MASK_VALUE = -1e30
VMEM_LIMIT_BYTES = 48 * 1024 * 1024

F32 = jnp.float32
BF16 = jnp.bfloat16
NT_DIMS = (((1,), (1,)), ((), ()))


def _rms(x, g):
    var = jnp.mean(x * x, axis=-1, keepdims=True)
    return x * lax.rsqrt(var + RMS_EPS) * g


def _params(*semantics):
    return pltpu.CompilerParams(dimension_semantics=semantics, vmem_limit_bytes=VMEM_LIMIT_BYTES)


def _inproj_kernel(x_ref, g_ref, w_ref, wvt_ref, q_ref, k_ref, vt_ref, p_ref):
    h = _rms(x_ref[...], g_ref[...]).astype(BF16)
    a = ATTN_WIDTH
    q = jnp.dot(h, w_ref[:, 0:a], preferred_element_type=F32)
    q_ref[...] = (q * (QK_SCALE * LOG2E)).astype(BF16)
    k_ref[...] = jnp.dot(h, w_ref[:, a:2 * a], preferred_element_type=F32).astype(BF16)
    p_ref[...] = jnp.dot(h, w_ref[:, 3 * a:], preferred_element_type=F32)
    vt_ref[0] = lax.dot_general(wvt_ref[...], h, NT_DIMS, preferred_element_type=F32).astype(BF16)


def _inproj(x, g, w_in, w_vt, tm):
    t = x.shape[0]
    row = lambda i: (i, 0)
    const = lambda i: (0, 0)
    return pl.pallas_call(
        _inproj_kernel,
        grid=(t // tm,),
        in_specs=[
            pl.BlockSpec((tm, D_MODEL), row),
            pl.BlockSpec((1, D_MODEL), const),
            pl.BlockSpec(w_in.shape, const),
            pl.BlockSpec(w_vt.shape, const),
        ],
        out_specs=[
            pl.BlockSpec((tm, ATTN_WIDTH), row),
            pl.BlockSpec((tm, ATTN_WIDTH), row),
            pl.BlockSpec((1, ATTN_WIDTH, tm), lambda i: (i, 0, 0)),
            pl.BlockSpec((tm, POOL_WIDTH), row),
        ],
        out_shape=[
            jax.ShapeDtypeStruct((t, ATTN_WIDTH), BF16),
            jax.ShapeDtypeStruct((t, ATTN_WIDTH), BF16),
            jax.ShapeDtypeStruct((t // tm, ATTN_WIDTH, tm), BF16),
            jax.ShapeDtypeStruct((t, POOL_WIDTH), F32),
        ],
        compiler_params=_params("parallel"),
        name="inproj",
    )(x, g, w_in, w_vt)


def _attn_kernel(far_ref, q_ref, qn_ref, k_ref, vt_ref, bias_ref, lq_ref, lk_ref, hn_ref, o_ref,
                 qs_scr, sa_scr, sb_scr, m_scr, l_scr, acc_scr, *, tile, nk, lam_init):
    h = pl.program_id(1)
    i = pl.program_id(2)
    t = tile
    cq = min(MXU_WIDTH, t)
    n_chunks = 2 * t // cq

    cur = lax.rem(i, 2)
    nxt = 1 - cur

    def stack_maps(slot, q):
        lane = lax.broadcasted_iota(jnp.int32, (t, LANES), 1)
        zero = jnp.zeros_like(q)
        qs_scr[slot, 0:t, :] = jnp.where(lane < DIFF_HEAD_DIM, q, zero)
        qs_scr[slot, t:2 * t, :] = jnp.where(lane >= DIFF_HEAD_DIM, q, zero)

    def scores(j, slot):
        start = pl.multiple_of(j * t, t)
        return lax.dot_general(k_ref[pl.ds(start, t), :], qs_scr[slot], NT_DIMS,
                               preferred_element_type=F32)

    n_pairs = nk // 2

    def plan(qi):
        first_near = lax.div(qi + 1, 2) - 1
        n_left = jnp.maximum(first_near, 0)
        right_start = first_near + 2
        n_far = n_left + jnp.maximum(n_pairs - right_start, 0)
        return n_left, right_start, n_far, n_left, jnp.minimum(first_near + 1, n_pairs - 1)

    def first_tile(qi):
        n_left_q, right_start_q, n_far_q, near_lo_q, _ = plan(qi)
        first_far = jnp.where(n_left_q > 0, 0, right_start_q)
        return 2 * jnp.where(n_far_q > 0, first_far, near_lo_q)

    @pl.when(i == 0)
    def _():
        stack_maps(0, q_ref[...])
        sa_scr[...] = scores(first_tile(0), 0)

    stack_maps(nxt, qn_ref[...])
    m_scr[...] = jnp.full(m_scr.shape, MASK_VALUE, F32)
    l_scr[...] = jnp.zeros(l_scr.shape, F32)
    acc_scr[...] = jnp.zeros(acc_scr.shape, F32)

    def softmax_pv(j, s_ref, near, c):
        vtj = vt_ref[j]
        for ci in range(n_chunks):
            cols = slice(ci * cq, (ci + 1) * cq)
            s = s_ref[:, cols]
            if near:
                q0 = (ci * cq) % t
                s = s + bias_ref[0, j - i + NEAR_REACH, :, q0:q0 + cq]
            m_prev = m_scr[:, cols]
            m_cur = jnp.max(s, axis=0, keepdims=True)
            if c is not None:
                m_cur = m_cur + c
            m_next = jnp.maximum(m_prev, m_cur)
            p = jnp.exp2(s - (m_next if c is None else m_next - c))
            alpha = jnp.exp2(m_prev - m_next)
            l_scr[:, cols] = alpha * l_scr[:, cols] + jnp.sum(p, axis=0, keepdims=True)
            acc_scr[:, cols] = alpha * acc_scr[:, cols] + jnp.dot(vtj, p.astype(BF16),
                                                                  preferred_element_type=F32)
            m_scr[:, cols] = m_next

    def pair(m, near, c, ahead_tile, ahead_slot, ahead_last=False):
        j0 = 2 * m
        sb_scr[...] = scores(j0 + 1, cur)
        softmax_pv(j0, sa_scr, near, c)
        if not ahead_last:
            sa_scr[...] = scores(ahead_tile, ahead_slot)
        softmax_pv(j0 + 1, sb_scr, near, c)
        if ahead_last:
            sa_scr[...] = scores(ahead_tile, ahead_slot)

    def finalize():
        o = acc_scr[...] * (1.0 / l_scr[...])
        prod = lq_ref[...] * lk_ref[...]
        lam = (jnp.exp(jnp.sum(prod[0:1], axis=1, keepdims=True))
               - jnp.exp(jnp.sum(prod[1:2], axis=1, keepdims=True)) + lam_init)
        a = (o[:, 0:t] - lam * o[:, t:2 * t]).T
        o_ref[...] = (_rms(a, hn_ref[...]) * (1.0 - lam_init)).astype(o_ref.dtype)

    n_left, right_start, n_far, near_lo, near_hi = plan(i)
    next_first = first_tile(jnp.minimum(i + 1, nk - 1))

    def far_pair(f):
        def index(g):
            return jnp.where(g < n_left, g, g - n_left + right_start)
        ahead = jnp.where(f + 1 < n_far, index(f + 1), near_lo)
        c = jnp.where(f < n_left, far_ref[h, 0], far_ref[h, 1])
        pair(index(f), False, c, 2 * ahead, cur)

    lead = jnp.where(n_far >= NEAR_LEAD, NEAR_LEAD, 0)
    n_loop = n_far - lead
    size = 1
    while size < FAR_BLOCK:
        @pl.when(jnp.bitwise_and(n_loop, size) != 0)
        def _(size=size):
            base = jnp.bitwise_and(n_loop, size - 1)
            for u in range(size):
                far_pair(base + u)
        size *= 2

    def far_block(u, carry):
        base = jnp.bitwise_and(n_loop, FAR_BLOCK - 1) + FAR_BLOCK * u
        for v in range(FAR_BLOCK):
            far_pair(base + v)
        return carry

    lax.fori_loop(0, lax.div(n_loop, FAR_BLOCK), far_block, 0)

    def near_block(n_lead, both):
        for v in range(n_lead):
            far_pair(n_loop + v)
        if both:
            pair(near_lo, True, None, 2 * near_hi, cur)
            pair(near_hi, True, None, next_first, nxt, ahead_last=True)
        else:
            pair(near_lo, True, None, next_first, nxt, ahead_last=True)
        finalize()

    for n_lead in sorted({0, NEAR_LEAD}):
        for both in (True, False):
            @pl.when(jnp.logical_and(lead == n_lead, (near_hi > near_lo) == both))
            def _(n_lead=n_lead, both=both):
                near_block(n_lead, both)


def _attention(q, k, vt, bias_near_t, far, lam_q, lam_k, head_norm, lam_init, tile, seq_len):
    t_all = q.shape[0]
    b = t_all // seq_len
    nk = seq_len // tile
    assert nk % 2 == 0, "key tiles are processed in pairs"
    kern = functools.partial(_attn_kernel, tile=tile, nk=nk, lam_init=lam_init)
    return pl.pallas_call(
        kern,
        grid=(b, N_DIFF_HEADS, nk),
        in_specs=[
            pl.BlockSpec(memory_space=pltpu.SMEM),
            pl.BlockSpec((tile, V_HEAD_DIM), lambda bi, h, i: (bi * nk + i, h)),
            pl.BlockSpec((tile, V_HEAD_DIM), lambda bi, h, i: (bi * nk + jnp.minimum(i + 1, nk - 1), h)),
            pl.BlockSpec((seq_len, V_HEAD_DIM), lambda bi, h, i: (bi, h)),
            pl.BlockSpec((nk, V_HEAD_DIM, tile), lambda bi, h, i: (bi, h, 0)),
            pl.BlockSpec((1, 2 * NEAR_REACH + 1, tile, tile), lambda bi, h, i: (h, 0, 0, 0)),
            pl.BlockSpec((2, DIFF_HEAD_DIM), lambda bi, h, i: (0, 0)),
            pl.BlockSpec((2, DIFF_HEAD_DIM), lambda bi, h, i: (0, 0)),
            pl.BlockSpec((1, V_HEAD_DIM), lambda bi, h, i: (0, 0)),
        ],
        out_specs=pl.BlockSpec((tile, V_HEAD_DIM), lambda bi, h, i: (bi * nk + i, h)),
        out_shape=jax.ShapeDtypeStruct((t_all, ATTN_WIDTH), BF16),
        scratch_shapes=[
            pltpu.VMEM((2, 2 * tile, V_HEAD_DIM), BF16),
            pltpu.VMEM((tile, 2 * tile), F32),
            pltpu.VMEM((tile, 2 * tile), F32),
            pltpu.VMEM((1, 2 * tile), F32),
            pltpu.VMEM((1, 2 * tile), F32),
            pltpu.VMEM((V_HEAD_DIM, 2 * tile), F32),
        ],
        compiler_params=_params("parallel", "parallel", "arbitrary"),
        name="diff_attention",
    )(far, q, q, k, vt, bias_near_t, lam_q, lam_k, head_norm)


def _mixout_kernel(a_ref, p_ref, pprev_ref, pnext_ref, x_ref, wout_ref, wpool_ref, ps_ref, g_ref,
                   o_ref, pbuf, *, tm, tiles_per_seq, seq_len):
    ti = lax.rem(pl.program_id(0), tiles_per_seq)
    keep_prev = (ti > 0).astype(F32)
    keep_next = (ti < tiles_per_seq - 1).astype(F32)
    pbuf[0:HALO, :] = pprev_ref[...] * keep_prev
    pbuf[HALO:HALO + tm, :] = p_ref[...]
    pbuf[HALO + tm:, :] = pnext_ref[...] * keep_next

    pos = ti * tm + lax.broadcasted_iota(jnp.int32, (tm, 1), 0)
    mixed = []
    for g, win in enumerate(POOL_WINDOWS):
        lo_off = -(win // 2)
        hi_off = win - win // 2 - 1
        cols = slice(g * POOL_GROUP, (g + 1) * POOL_GROUP)
        total = pbuf[HALO + lo_off:HALO + lo_off + tm, cols]
        for off in range(lo_off + 1, hi_off + 1):
            total = total + pbuf[HALO + off:HALO + off + tm, cols]
        lo = jnp.maximum(pos + lo_off, 0)
        hi = jnp.minimum(pos + hi_off, seq_len - 1)
        cnt = (hi - lo + 1).astype(F32)
        y = total / cnt - pbuf[HALO:HALO + tm, cols]
        mixed.append(jnp.dot(y.astype(BF16), wpool_ref[g], preferred_element_type=F32))
    m = jnp.concatenate(mixed, axis=1) * ps_ref[...]

    o = jnp.dot(a_ref[...], wout_ref[0:ATTN_WIDTH, :], preferred_element_type=F32)
    o = o + jnp.dot(m.astype(BF16), wout_ref[ATTN_WIDTH:, :], preferred_element_type=F32)
    o_ref[...] = x_ref[...] + _rms(o, g_ref[...])


def _halo_specs(tm, width, n_rows):
    per_tile = tm // HALO
    last = n_rows // HALO - 1
    prev = pl.BlockSpec((HALO, width), lambda i, *_: (jnp.maximum(i * per_tile - 1, 0), 0))
    nxt = pl.BlockSpec((HALO, width), lambda i, *_: (jnp.minimum((i + 1) * per_tile, last), 0))
    return prev, nxt


def _mixout(a, p, x, w_out, w_pool, pool_scale, g, seq_len, tm):
    t = x.shape[0]
    row = lambda i: (i, 0)
    const = lambda i: (0, 0)
    prev, nxt = _halo_specs(tm, POOL_WIDTH, t)
    kern = functools.partial(_mixout_kernel, tm=tm, tiles_per_seq=seq_len // tm, seq_len=seq_len)
    return pl.pallas_call(
        kern,
        grid=(t // tm,),
        in_specs=[
            pl.BlockSpec((tm, ATTN_WIDTH), row),
            pl.BlockSpec((tm, POOL_WIDTH), row),
            prev,
            nxt,
            pl.BlockSpec((tm, D_MODEL), row),
            pl.BlockSpec(w_out.shape, const),
            pl.BlockSpec(w_pool.shape, lambda i: (0, 0, 0)),
            pl.BlockSpec((1, POOL_WIDTH), const),
            pl.BlockSpec((1, D_MODEL), const),
        ],
        out_specs=pl.BlockSpec((tm, D_MODEL), row),
        out_shape=jax.ShapeDtypeStruct((t, D_MODEL), F32),
        scratch_shapes=[pltpu.VMEM((tm + 2 * HALO, POOL_WIDTH), F32)],
        compiler_params=_params("parallel"),
        name="pool_outproj",
    )(a, p, p, p, x, w_out, w_pool, pool_scale, g)


def _gelu_tanh(x):
    return 0.5 * x * (1.0 + jnp.tanh(math.sqrt(2.0 / math.pi) * (x + 0.044715 * (x * x * x))))


def _ffn_kernel(x_ref, xprev_ref, xnext_ref, gpre_ref, wg_ref, wv_ref, cwg_ref, cwv_ref, cbg_ref, cbv_ref,
                wd_ref, gpost_ref, o_ref, h_scr, u_scr, acc_scr, *, tm, tiles_per_seq):
    c = pl.program_id(1)

    @pl.when(c == 0)
    def _():
        ti = lax.rem(pl.program_id(0), tiles_per_seq)
        keep_prev = (ti > 0).astype(F32)
        keep_next = (ti < tiles_per_seq - 1).astype(F32)
        g = gpre_ref[...]
        h_scr[0:HALO, :] = _rms(xprev_ref[...], g) * keep_prev
        h_scr[HALO:HALO + tm, :] = _rms(x_ref[...], g)
        h_scr[HALO + tm:, :] = _rms(xnext_ref[...], g) * keep_next
        acc_scr[...] = jnp.zeros(acc_scr.shape, F32)

    h = h_scr[...].astype(BF16)

    def conv(w_ref, cw_ref, cb_ref):
        u_scr[...] = jnp.dot(h, w_ref[...], preferred_element_type=F32)
        cw = cw_ref[...]
        return (u_scr[HALO - 1:HALO - 1 + tm, :] * cw[0:1]
                + u_scr[HALO:HALO + tm, :] * cw[1:2]
                + u_scr[HALO + 1:HALO + 1 + tm, :] * cw[2:3]
                + cb_ref[...])

    gate = conv(wg_ref, cwg_ref, cbg_ref)
    val = conv(wv_ref, cwv_ref, cbv_ref)
    f = (_gelu_tanh(gate) * val).astype(BF16)
    acc_scr[...] += jnp.dot(f, wd_ref[...], preferred_element_type=F32)

    @pl.when(c == pl.num_programs(1) - 1)
    def _():
        o_ref[...] = x_ref[...] + _rms(acc_scr[...], gpost_ref[...])


def _ffn(x, g_pre, w_up, conv_w, conv_b, w_down, g_post, seq_len, tm):
    t = x.shape[0]
    n_chunks = D_FF // FF_CHUNK
    row = lambda i, c: (i, 0)
    const = lambda i, c: (0, 0)
    gate_cols = lambda i, c: (0, c)
    val_cols = lambda i, c: (0, n_chunks + c)
    prev, nxt = _halo_specs(tm, D_MODEL, t)
    kern = functools.partial(_ffn_kernel, tm=tm, tiles_per_seq=seq_len // tm)
    return pl.pallas_call(
        kern,
        grid=(t // tm, n_chunks),
        in_specs=[
            pl.BlockSpec((tm, D_MODEL), row),
            prev,
            nxt,
            pl.BlockSpec((1, D_MODEL), const),
            pl.BlockSpec((D_MODEL, FF_CHUNK), gate_cols),
            pl.BlockSpec((D_MODEL, FF_CHUNK), val_cols),
            pl.BlockSpec((3, FF_CHUNK), gate_cols),
            pl.BlockSpec((3, FF_CHUNK), val_cols),
            pl.BlockSpec((1, FF_CHUNK), gate_cols),
            pl.BlockSpec((1, FF_CHUNK), val_cols),
            pl.BlockSpec((FF_CHUNK, D_MODEL), lambda i, c: (c, 0)),
            pl.BlockSpec((1, D_MODEL), const),
        ],
        out_specs=pl.BlockSpec((tm, D_MODEL), row),
        out_shape=jax.ShapeDtypeStruct((t, D_MODEL), F32),
        scratch_shapes=[
            pltpu.VMEM((tm + 2 * HALO, D_MODEL), F32),
            pltpu.VMEM((tm + 2 * HALO, FF_CHUNK), F32),
            pltpu.VMEM((tm, D_MODEL), F32),
        ],
        compiler_params=_params("parallel", "arbitrary"),
        name="conv_mlp",
    )(x, x, x, g_pre, w_up, w_up, conv_w, conv_w, conv_b, conv_b, w_down, g_post)


def _rel_bucket(rel):
    nb = NUM_BUCKETS // 2
    max_exact = nb // 2
    ret = jnp.where(rel > 0, nb, 0)
    n = jnp.abs(rel)
    nf = jnp.maximum(n, 1).astype(F32)
    large = max_exact + (jnp.log(nf / max_exact) / math.log(MAX_DISTANCE / max_exact)
                         * (nb - max_exact)).astype(jnp.int32)
    large = jnp.minimum(large, nb - 1)
    return ret + jnp.where(n < max_exact, n, large)


def _bias_expand_kernel(y_ref, o_ref, *, tile):
    span = y_ref.shape[-1]
    table = jnp.broadcast_to(y_ref[0], (tile, span))
    skewed = pltpu.roll(table, 0, 1, stride=1, stride_axis=0)
    for di, d in enumerate(range(-NEAR_REACH, NEAR_REACH + 1)):
        start = (-d * tile) % span
        o_ref[0, di] = skewed[:, start:start + tile]


def _bias_tables(rel_bias, tile):
    assert tile >= MAX_DISTANCE, "far tiles must lie beyond the last distance bucket"
    n_near = 2 * NEAR_REACH + 1
    span = 2 * (NEAR_REACH + 1) * tile
    m = jnp.arange(span, dtype=jnp.int32)
    rel = jnp.where(m < span // 2, -m, span - m)
    rel_bias = rel_bias.astype(F32) * LOG2E
    by_rel = rel_bias[_rel_bucket(rel)].T[:, None, :]
    near = pl.pallas_call(
        functools.partial(_bias_expand_kernel, tile=tile),
        grid=(N_DIFF_HEADS,),
        in_specs=[pl.BlockSpec((1, 1, span), lambda h: (h, 0, 0))],
        out_specs=pl.BlockSpec((1, n_near, tile, tile), lambda h: (h, 0, 0, 0)),
        out_shape=jax.ShapeDtypeStruct((N_DIFF_HEADS, n_near, tile, tile), F32),
        compiler_params=_params("parallel"),
        name="bias_expand",
    )(by_rel)
    far = jnp.stack([rel_bias[NUM_BUCKETS // 2 - 1], rel_bias[NUM_BUCKETS - 1]], axis=1)
    return near, far


def kernel(x_prompt, x_sample, rel_bias, ln_mix_pre, ln_mix_post, w_in, lam_q, lam_k, head_norm, w_pool,
           pool_scale, w_out, ln_ffn_pre, ln_ffn_post, w_up, conv_w, conv_b, w_down):
    depth = w_in.shape[0]
    w_in_b, w_out_b, w_pool_b = w_in.astype(BF16), w_out.astype(BF16), w_pool.astype(BF16)
    w_up_b, w_down_b = w_up.astype(BF16), w_down.astype(BF16)
    w_vt_b = jnp.swapaxes(w_in_b[:, :, 2 * ATTN_WIDTH:3 * ATTN_WIDTH], 1, 2)
    tables = {}

    def trunk(x):
        b, s, d = x.shape
        tile = min(ATTN_TILE, s)
        tm = min(ROW_TILE, s)
        assert s % tile == 0 and s % tm == 0 and tm % HALO == 0
        if tile not in tables:
            tables[tile] = _bias_tables(rel_bias, tile)
        bias_near_t, far = tables[tile]
        xf = x.reshape(b * s, d)
        for i in range(depth):
            lam_init = 0.8 - 0.6 * math.exp(-0.3 * i)
            q, k, vt, p = _inproj(xf, ln_mix_pre[i][None], w_in_b[i], w_vt_b[i], tile)
            a = _attention(q, k, vt, bias_near_t, far, lam_q[i], lam_k[i], head_norm[i][None], lam_init,
                           tile, s)
            xf = _mixout(a, p, xf, w_out_b[i], w_pool_b[i], pool_scale[i][None], ln_mix_post[i][None], s, tm)
            xf = _ffn(xf, ln_ffn_pre[i][None], w_up_b[i], conv_w[i], conv_b[i][None], w_down_b[i],
                      ln_ffn_post[i][None], s, tm)
        return xf.reshape(b, s, d)

    return (trunk(x_prompt), trunk(x_sample))
```

```python
import functools
import math

import jax
import jax.numpy as jnp
from jax import lax
from jax.experimental import pallas as pl
from jax.experimental.pallas import tpu as pltpu

D_MODEL = 1024
ATTN_WIDTH = 512
POOL_WIDTH = D_MODEL - ATTN_WIDTH
N_DIFF_HEADS = 4
DIFF_HEAD_DIM = 64
V_HEAD_DIM = 2 * DIFF_HEAD_DIM
POOL_WINDOWS = (2, 4, 8, 16)
POOL_GROUP = POOL_WIDTH // len(POOL_WINDOWS)
D_FF = 2816
NUM_BUCKETS = 32
MAX_DISTANCE = 128
RMS_EPS = 1e-6
QK_SCALE = DIFF_HEAD_DIM ** -0.5
LOG2E = math.log2(math.e)

LANES = 128
SUBLANES = 8
MXU_WIDTH = 256
ATTN_TILE = 512
ROW_TILE = 512
FF_CHUNK = D_FF // 2
HALO = SUBLANES
FAR_BLOCK = 4
NEAR_LEAD = 6
NEAR_REACH = 2
MASK_VALUE = -1e30
VMEM_LIMIT_BYTES = 48 * 1024 * 1024

F32 = jnp.float32
BF16 = jnp.bfloat16
NT_DIMS = (((1,), (1,)), ((), ()))


def _rms(x, g):
    var = jnp.mean(x * x, axis=-1, keepdims=True)
    return x * lax.rsqrt(var + RMS_EPS) * g


def _params(*semantics):
    return pltpu.CompilerParams(dimension_semantics=semantics, vmem_limit_bytes=VMEM_LIMIT_BYTES)


def _inproj_kernel(x_ref, g_ref, w_ref, wvt_ref, q_ref, k_ref, vt_ref, p_ref):
    h = _rms(x_ref[...], g_ref[...]).astype(BF16)
    a = ATTN_WIDTH
    q = jnp.dot(h, w_ref[:, 0:a], preferred_element_type=F32)
    q_ref[...] = (q * (QK_SCALE * LOG2E)).astype(BF16)
    k_ref[...] = jnp.dot(h, w_ref[:, a:2 * a], preferred_element_type=F32).astype(BF16)
    p_ref[...] = jnp.dot(h, w_ref[:, 3 * a:], preferred_element_type=F32)
    vt_ref[0] = lax.dot_general(wvt_ref[...], h, NT_DIMS, preferred_element_type=F32).astype(BF16)


def _inproj(x, g, w_in, w_vt, tm):
    t = x.shape[0]
    row = lambda i: (i, 0)
    const = lambda i: (0, 0)
    return pl.pallas_call(
        _inproj_kernel,
        grid=(t // tm,),
        in_specs=[
            pl.BlockSpec((tm, D_MODEL), row),
            pl.BlockSpec((1, D_MODEL), const),
            pl.BlockSpec(w_in.shape, const),
            pl.BlockSpec(w_vt.shape, const),
        ],
        out_specs=[
            pl.BlockSpec((tm, ATTN_WIDTH), row),
            pl.BlockSpec((tm, ATTN_WIDTH), row),
            pl.BlockSpec((1, ATTN_WIDTH, tm), lambda i: (i, 0, 0)),
            pl.BlockSpec((tm, POOL_WIDTH), row),
        ],
        out_shape=[
            jax.ShapeDtypeStruct((t, ATTN_WIDTH), BF16),
            jax.ShapeDtypeStruct((t, ATTN_WIDTH), BF16),
            jax.ShapeDtypeStruct((t // tm, ATTN_WIDTH, tm), BF16),
            jax.ShapeDtypeStruct((t, POOL_WIDTH), F32),
        ],
        compiler_params=_params("parallel"),
        name="inproj",
    )(x, g, w_in, w_vt)


def _attn_kernel(far_ref, q_ref, qn_ref, k_ref, vt_ref, bias_ref, lq_ref, lk_ref, hn_ref, o_ref,
                 qs_scr, sa_scr, sb_scr, m_scr, l_scr, acc_scr, *, tile, nk, lam_init):
    h = pl.program_id(1)
    i = pl.program_id(2)
    t = tile
    cq = min(MXU_WIDTH, t)
    n_chunks = 2 * t // cq

    cur = lax.rem(i, 2)
    nxt = 1 - cur

    def stack_maps(slot, q):
        lane = lax.broadcasted_iota(jnp.int32, (t, LANES), 1)
        zero = jnp.zeros_like(q)
        qs_scr[slot, 0:t, :] = jnp.where(lane < DIFF_HEAD_DIM, q, zero)
        qs_scr[slot, t:2 * t, :] = jnp.where(lane >= DIFF_HEAD_DIM, q, zero)

    def scores(j, slot):
        start = pl.multiple_of(j * t, t)
        return lax.dot_general(k_ref[pl.ds(start, t), :], qs_scr[slot], NT_DIMS,
                               preferred_element_type=F32)

    n_pairs = nk // 2

    def plan(qi):
        first_near = lax.div(qi + 1, 2) - 1
        n_left = jnp.maximum(first_near, 0)
        right_start = first_near + 2
        n_far = n_left + jnp.maximum(n_pairs - right_start, 0)
        return n_left, right_start, n_far, n_left, jnp.minimum(first_near + 1, n_pairs - 1)

    def first_tile(qi):
        n_left_q, right_start_q, n_far_q, near_lo_q, _ = plan(qi)
        first_far = jnp.where(n_left_q > 0, 0, right_start_q)
        return 2 * jnp.where(n_far_q > 0, first_far, near_lo_q)

    @pl.when(i == 0)
    def _():
        stack_maps(0, q_ref[...])
        sa_scr[...] = scores(first_tile(0), 0)

    stack_maps(nxt, qn_ref[...])
    m_scr[...] = jnp.full(m_scr.shape, MASK_VALUE, F32)
    l_scr[...] = jnp.zeros(l_scr.shape, F32)
    acc_scr[...] = jnp.zeros(acc_scr.shape, F32)

    def softmax_pv(j, s_ref, near, c):
        vtj = vt_ref[j]
        for ci in range(n_chunks):
            cols = slice(ci * cq, (ci + 1) * cq)
            s = s_ref[:, cols]
            if near:
                q0 = (ci * cq) % t
                s = s + bias_ref[0, j - i + NEAR_REACH, :, q0:q0 + cq]
            m_prev = m_scr[:, cols]
            m_cur = jnp.max(s, axis=0, keepdims=True)
            if c is not None:
                m_cur = m_cur + c
            m_next = jnp.maximum(m_prev, m_cur)
            p = jnp.exp2(s - (m_next if c is None else m_next - c))
            alpha = jnp.exp2(m_prev - m_next)
            l_scr[:, cols] = alpha * l_scr[:, cols] + jnp.sum(p, axis=0, keepdims=True)
            acc_scr[:, cols] = alpha * acc_scr[:, cols] + jnp.dot(vtj, p.astype(BF16),
                                                                  preferred_element_type=F32)
            m_scr[:, cols] = m_next

    def pair(m, near, c, ahead_tile, ahead_slot):
        j0 = 2 * m
        sb_scr[...] = scores(j0 + 1, cur)
        softmax_pv(j0, sa_scr, near, c)
        sa_scr[...] = scores(ahead_tile, ahead_slot)
        softmax_pv(j0 + 1, sb_scr, near, c)

    n_left, right_start, n_far, near_lo, near_hi = plan(i)
    next_first = first_tile(jnp.minimum(i + 1, nk - 1))

    def far_pair(f):
        def index(g):
            return jnp.where(g < n_left, g, g - n_left + right_start)
        ahead = jnp.where(f + 1 < n_far, index(f + 1), near_lo)
        c = jnp.where(f < n_left, far_ref[h, 0], far_ref[h, 1])
        pair(index(f), False, c, 2 * ahead, cur)

    lead = jnp.where(n_far >= NEAR_LEAD, NEAR_LEAD, 0)
    n_loop = n_far - lead
    size = 1
    while size < FAR_BLOCK:
        @pl.when(jnp.bitwise_and(n_loop, size) != 0)
        def _(size=size):
            base = jnp.bitwise_and(n_loop, size - 1)
            for u in range(size):
                far_pair(base + u)
        size *= 2

    def far_block(u, carry):
        base = jnp.bitwise_and(n_loop, FAR_BLOCK - 1) + FAR_BLOCK * u
        for v in range(FAR_BLOCK):
            far_pair(base + v)
        return carry

    lax.fori_loop(0, lax.div(n_loop, FAR_BLOCK), far_block, 0)

    def near_block(n_lead, both):
        for v in range(n_lead):
            far_pair(n_loop + v)
        if both:
            pair(near_lo, True, None, 2 * near_hi, cur)
            pair(near_hi, True, None, next_first, nxt)
        else:
            pair(near_lo, True, None, next_first, nxt)

    for n_lead in sorted({0, NEAR_LEAD}):
        for both in (True, False):
            @pl.when(jnp.logical_and(lead == n_lead, (near_hi > near_lo) == both))
            def _(n_lead=n_lead, both=both):
                near_block(n_lead, both)

    o = acc_scr[...] * (1.0 / l_scr[...])
    prod = lq_ref[...] * lk_ref[...]
    lam = (jnp.exp(jnp.sum(prod[0:1], axis=1, keepdims=True))
           - jnp.exp(jnp.sum(prod[1:2], axis=1, keepdims=True)) + lam_init)
    a = (o[:, 0:t] - lam * o[:, t:2 * t]).T
    o_ref[...] = (_rms(a, hn_ref[...]) * (1.0 - lam_init)).astype(o_ref.dtype)


def _attention(q, k, vt, bias_near_t, far, lam_q, lam_k, head_norm, lam_init, tile, seq_len):
    t_all = q.shape[0]
    b = t_all // seq_len
    nk = seq_len // tile
    assert nk % 2 == 0, "key tiles are processed in pairs"
    kern = functools.partial(_attn_kernel, tile=tile, nk=nk, lam_init=lam_init)
    return pl.pallas_call(
        kern,
        grid=(b, N_DIFF_HEADS, nk),
        in_specs=[
            pl.BlockSpec(memory_space=pltpu.SMEM),
            pl.BlockSpec((tile, V_HEAD_DIM), lambda bi, h, i: (bi * nk + i, h)),
            pl.BlockSpec((tile, V_HEAD_DIM), lambda bi, h, i: (bi * nk + jnp.minimum(i + 1, nk - 1), h)),
            pl.BlockSpec((seq_len, V_HEAD_DIM), lambda bi, h, i: (bi, h)),
            pl.BlockSpec((nk, V_HEAD_DIM, tile), lambda bi, h, i: (bi, h, 0)),
            pl.BlockSpec((1, 2 * NEAR_REACH + 1, tile, tile), lambda bi, h, i: (h, 0, 0, 0)),
            pl.BlockSpec((2, DIFF_HEAD_DIM), lambda bi, h, i: (0, 0)),
            pl.BlockSpec((2, DIFF_HEAD_DIM), lambda bi, h, i: (0, 0)),
            pl.BlockSpec((1, V_HEAD_DIM), lambda bi, h, i: (0, 0)),
        ],
        out_specs=pl.BlockSpec((tile, V_HEAD_DIM), lambda bi, h, i: (bi * nk + i, h)),
        out_shape=jax.ShapeDtypeStruct((t_all, ATTN_WIDTH), BF16),
        scratch_shapes=[
            pltpu.VMEM((2, 2 * tile, V_HEAD_DIM), BF16),
            pltpu.VMEM((tile, 2 * tile), F32),
            pltpu.VMEM((tile, 2 * tile), F32),
            pltpu.VMEM((1, 2 * tile), F32),
            pltpu.VMEM((1, 2 * tile), F32),
            pltpu.VMEM((V_HEAD_DIM, 2 * tile), F32),
        ],
        compiler_params=_params("parallel", "parallel", "arbitrary"),
        name="diff_attention",
    )(far, q, q, k, vt, bias_near_t, lam_q, lam_k, head_norm)


def _mixout_kernel(a_ref, p_ref, pprev_ref, pnext_ref, x_ref, wout_ref, wpool_ref, ps_ref, g_ref,
                   o_ref, pbuf, *, tm, tiles_per_seq, seq_len):
    ti = lax.rem(pl.program_id(0), tiles_per_seq)
    keep_prev = (ti > 0).astype(F32)
    keep_next = (ti < tiles_per_seq - 1).astype(F32)
    pbuf[0:HALO, :] = pprev_ref[...] * keep_prev
    pbuf[HALO:HALO + tm, :] = p_ref[...]
    pbuf[HALO + tm:, :] = pnext_ref[...] * keep_next

    pos = ti * tm + lax.broadcasted_iota(jnp.int32, (tm, 1), 0)
    mixed = []
    for g, win in enumerate(POOL_WINDOWS):
        lo_off = -(win // 2)
        hi_off = win - win // 2 - 1
        cols = slice(g * POOL_GROUP, (g + 1) * POOL_GROUP)
        total = pbuf[HALO + lo_off:HALO + lo_off + tm, cols]
        for off in range(lo_off + 1, hi_off + 1):
            total = total + pbuf[HALO + off:HALO + off + tm, cols]
        lo = jnp.maximum(pos + lo_off, 0)
        hi = jnp.minimum(pos + hi_off, seq_len - 1)
        cnt = (hi - lo + 1).astype(F32)
        y = total / cnt - pbuf[HALO:HALO + tm, cols]
        mixed.append(jnp.dot(y.astype(BF16), wpool_ref[g], preferred_element_type=F32))
    m = jnp.concatenate(mixed, axis=1) * ps_ref[...]

    o = jnp.dot(a_ref[...], wout_ref[0:ATTN_WIDTH, :], preferred_element_type=F32)
    o = o + jnp.dot(m.astype(BF16), wout_ref[ATTN_WIDTH:, :], preferred_element_type=F32)
    o_ref[...] = x_ref[...] + _rms(o, g_ref[...])


def _halo_specs(tm, width, n_rows):
    per_tile = tm // HALO
    last = n_rows // HALO - 1
    prev = pl.BlockSpec((HALO, width), lambda i, *_: (jnp.maximum(i * per_tile - 1, 0), 0))
    nxt = pl.BlockSpec((HALO, width), lambda i, *_: (jnp.minimum((i + 1) * per_tile, last), 0))
    return prev, nxt


def _mixout(a, p, x, w_out, w_pool, pool_scale, g, seq_len, tm):
    t = x.shape[0]
    row = lambda i: (i, 0)
    const = lambda i: (0, 0)
    prev, nxt = _halo_specs(tm, POOL_WIDTH, t)
    kern = functools.partial(_mixout_kernel, tm=tm, tiles_per_seq=seq_len // tm, seq_len=seq_len)
    return pl.pallas_call(
        kern,
        grid=(t // tm,),
        in_specs=[
            pl.BlockSpec((tm, ATTN_WIDTH), row),
            pl.BlockSpec((tm, POOL_WIDTH), row),
            prev,
            nxt,
            pl.BlockSpec((tm, D_MODEL), row),
            pl.BlockSpec(w_out.shape, const),
            pl.BlockSpec(w_pool.shape, lambda i: (0, 0, 0)),
            pl.BlockSpec((1, POOL_WIDTH), const),
            pl.BlockSpec((1, D_MODEL), const),
        ],
        out_specs=pl.BlockSpec((tm, D_MODEL), row),
        out_shape=jax.ShapeDtypeStruct((t, D_MODEL), F32),
        scratch_shapes=[pltpu.VMEM((tm + 2 * HALO, POOL_WIDTH), F32)],
        compiler_params=_params("parallel"),
        name="pool_outproj",
    )(a, p, p, p, x, w_out, w_pool, pool_scale, g)


def _gelu_tanh(x):
    return 0.5 * x * (1.0 + jnp.tanh(math.sqrt(2.0 / math.pi) * (x + 0.044715 * (x * x * x))))


def _ffn_kernel(x_ref, xprev_ref, xnext_ref, gpre_ref, wup_ref, cw_ref, cb_ref, wd_ref, gpost_ref, o_ref,
                h_scr, u_scr, acc_scr, *, tm, tiles_per_seq):
    ti = lax.rem(pl.program_id(0), tiles_per_seq)
    keep_prev = (ti > 0).astype(F32)
    keep_next = (ti < tiles_per_seq - 1).astype(F32)
    g = gpre_ref[...]
    h_scr[0:HALO, :] = _rms(xprev_ref[...], g) * keep_prev
    h_scr[HALO:HALO + tm, :] = _rms(x_ref[...], g)
    h_scr[HALO + tm:, :] = _rms(xnext_ref[...], g) * keep_next
    h = h_scr[...].astype(BF16)

    def conv(c0, u_ref):
        cols = slice(c0, c0 + FF_CHUNK)
        u_ref[...] = jnp.dot(h, wup_ref[:, cols], preferred_element_type=F32)
        cw = cw_ref[:, cols]
        return (u_ref[HALO - 1:HALO - 1 + tm, :] * cw[0:1]
                + u_ref[HALO:HALO + tm, :] * cw[1:2]
                + u_ref[HALO + 1:HALO + 1 + tm, :] * cw[2:3]
                + cb_ref[:, cols])

    for ci in range(D_FF // FF_CHUNK):
        c0 = ci * FF_CHUNK
        gate = conv(c0, u_scr.at[0])
        val = conv(D_FF + c0, u_scr.at[1])
        f = (_gelu_tanh(gate) * val).astype(BF16)
        down = jnp.dot(f, wd_ref[c0:c0 + FF_CHUNK, :], preferred_element_type=F32)
        if ci == 0:
            acc_scr[...] = down
        else:
            acc_scr[...] += down

    o_ref[...] = x_ref[...] + _rms(acc_scr[...], gpost_ref[...])


def _ffn(x, g_pre, w_up, conv_w, conv_b, w_down, g_post, seq_len, tm):
    t = x.shape[0]
    row = lambda i: (i, 0)
    const = lambda i: (0, 0)
    resident = lambda shape: pl.BlockSpec(shape, const, pipeline_mode=pl.Buffered(1))
    prev, nxt = _halo_specs(tm, D_MODEL, t)
    kern = functools.partial(_ffn_kernel, tm=tm, tiles_per_seq=seq_len // tm)
    return pl.pallas_call(
        kern,
        grid=(t // tm,),
        in_specs=[
            pl.BlockSpec((tm, D_MODEL), row),
            prev,
            nxt,
            pl.BlockSpec((1, D_MODEL), const),
            resident(w_up.shape),
            pl.BlockSpec(conv_w.shape, const),
            pl.BlockSpec(conv_b.shape, const),
            resident(w_down.shape),
            pl.BlockSpec((1, D_MODEL), const),
        ],
        out_specs=pl.BlockSpec((tm, D_MODEL), row),
        out_shape=jax.ShapeDtypeStruct((t, D_MODEL), F32),
        scratch_shapes=[
            pltpu.VMEM((tm + 2 * HALO, D_MODEL), F32),
            pltpu.VMEM((2, tm + 2 * HALO, FF_CHUNK), F32),
            pltpu.VMEM((tm, D_MODEL), F32),
        ],
        compiler_params=_params("parallel"),
        name="conv_mlp",
    )(x, x, x, g_pre, w_up, conv_w, conv_b, w_down, g_post)


def _rel_bucket(rel):
    nb = NUM_BUCKETS // 2
    max_exact = nb // 2
    ret = jnp.where(rel > 0, nb, 0)
    n = jnp.abs(rel)
    nf = jnp.maximum(n, 1).astype(F32)
    large = max_exact + (jnp.log(nf / max_exact) / math.log(MAX_DISTANCE / max_exact)
                         * (nb - max_exact)).astype(jnp.int32)
    large = jnp.minimum(large, nb - 1)
    return ret + jnp.where(n < max_exact, n, large)


def _bias_expand_kernel(y_ref, o_ref, *, tile):
    span = y_ref.shape[-1]
    table = jnp.broadcast_to(y_ref[0], (tile, span))
    skewed = pltpu.roll(table, 0, 1, stride=1, stride_axis=0)
    for di, d in enumerate(range(-NEAR_REACH, NEAR_REACH + 1)):
        start = (-d * tile) % span
        o_ref[0, di] = skewed[:, start:start + tile]


def _bias_tables(rel_bias, tile):
    assert tile >= MAX_DISTANCE, "far tiles must lie beyond the last distance bucket"
    n_near = 2 * NEAR_REACH + 1
    span = 2 * (NEAR_REACH + 1) * tile
    m = jnp.arange(span, dtype=jnp.int32)
    rel = jnp.where(m < span // 2, -m, span - m)
    rel_bias = rel_bias.astype(F32) * LOG2E
    by_rel = rel_bias[_rel_bucket(rel)].T[:, None, :]
    near = pl.pallas_call(
        functools.partial(_bias_expand_kernel, tile=tile),
        grid=(N_DIFF_HEADS,),
        in_specs=[pl.BlockSpec((1, 1, span), lambda h: (h, 0, 0))],
        out_specs=pl.BlockSpec((1, n_near, tile, tile), lambda h: (h, 0, 0, 0)),
        out_shape=jax.ShapeDtypeStruct((N_DIFF_HEADS, n_near, tile, tile), F32),
        compiler_params=_params("parallel"),
        name="bias_expand",
    )(by_rel)
    far = jnp.stack([rel_bias[NUM_BUCKETS // 2 - 1], rel_bias[NUM_BUCKETS - 1]], axis=1)
    return near, far


def kernel(x_prompt, x_sample, rel_bias, ln_mix_pre, ln_mix_post, w_in, lam_q, lam_k, head_norm, w_pool,
           pool_scale, w_out, ln_ffn_pre, ln_ffn_post, w_up, conv_w, conv_b, w_down):
    depth = w_in.shape[0]
    w_in_b, w_out_b, w_pool_b = w_in.astype(BF16), w_out.astype(BF16), w_pool.astype(BF16)
    w_up_b, w_down_b = w_up.astype(BF16), w_down.astype(BF16)
    w_vt_b = jnp.swapaxes(w_in_b[:, :, 2 * ATTN_WIDTH:3 * ATTN_WIDTH], 1, 2)
    tables = {}

    def trunk(x):
        b, s, d = x.shape
        tile = min(ATTN_TILE, s)
        tm = min(ROW_TILE, s)
        assert s % tile == 0 and s % tm == 0 and tm % HALO == 0
        if tile not in tables:
            tables[tile] = _bias_tables(rel_bias, tile)
        bias_near_t, far = tables[tile]
        xf = x.reshape(b * s, d)
        for i in range(depth):
            lam_init = 0.8 - 0.6 * math.exp(-0.3 * i)
            q, k, vt, p = _inproj(xf, ln_mix_pre[i][None], w_in_b[i], w_vt_b[i], tile)
            a = _attention(q, k, vt, bias_near_t, far, lam_q[i], lam_k[i], head_norm[i][None], lam_init,
                           tile, s)
            xf = _mixout(a, p, xf, w_out_b[i], w_pool_b[i], pool_scale[i][None], ln_mix_post[i][None], s, tm)
            xf = _ffn(xf, ln_ffn_pre[i][None], w_up_b[i], conv_w[i], conv_b[i][None], w_down_b[i],
                      ln_ffn_post[i][None], s, tm)
        return xf.reshape(b, s, d)

    return (trunk(x_prompt), trunk(x_sample))
```

```python
import functools
import math

import jax
import jax.numpy as jnp
from jax import lax
from jax.experimental import pallas as pl
from jax.experimental.pallas import tpu as pltpu

D_MODEL = 1024
ATTN_WIDTH = 512
POOL_WIDTH = D_MODEL - ATTN_WIDTH
N_DIFF_HEADS = 4
DIFF_HEAD_DIM = 64
V_HEAD_DIM = 2 * DIFF_HEAD_DIM
POOL_WINDOWS = (2, 4, 8, 16)
POOL_GROUP = POOL_WIDTH // len(POOL_WINDOWS)
D_FF = 2816
NUM_BUCKETS = 32
MAX_DISTANCE = 128
RMS_EPS = 1e-6
QK_SCALE = DIFF_HEAD_DIM ** -0.5
LOG2E = math.log2(math.e)

LANES = 128
SUBLANES = 8
MXU_WIDTH = 256
ATTN_TILE = 512
ROW_TILE = 512
FF_CHUNK = D_FF // 2
HALO = SUBLANES
FAR_BLOCK = 8
NEAR_LEAD = 6
NEAR_REACH = 2
MASK_VALUE = -1e30
VMEM_LIMIT_BYTES = 48 * 1024 * 1024

F32 = jnp.float32
BF16 = jnp.bfloat16
NT_DIMS = (((1,), (1,)), ((), ()))


def _rms(x, g):
    var = jnp.mean(x * x, axis=-1, keepdims=True)
    return x * lax.rsqrt(var + RMS_EPS) * g


def _params(*semantics):
    return pltpu.CompilerParams(dimension_semantics=semantics, vmem_limit_bytes=VMEM_LIMIT_BYTES)


def _inproj_kernel(x_ref, g_ref, w_ref, wvt_ref, q_ref, k_ref, vt_ref, p_ref):
    h = _rms(x_ref[...], g_ref[...]).astype(BF16)
    a = ATTN_WIDTH
    q = jnp.dot(h, w_ref[:, 0:a], preferred_element_type=F32)
    q_ref[...] = (q * (QK_SCALE * LOG2E)).astype(BF16)
    k_ref[...] = jnp.dot(h, w_ref[:, a:2 * a], preferred_element_type=F32).astype(BF16)
    p_ref[...] = jnp.dot(h, w_ref[:, 3 * a:], preferred_element_type=F32)
    vt_ref[0] = lax.dot_general(wvt_ref[...], h, NT_DIMS, preferred_element_type=F32).astype(BF16)


def _inproj(x, g, w_in, w_vt, tm):
    t = x.shape[0]
    row = lambda i: (i, 0)
    const = lambda i: (0, 0)
    return pl.pallas_call(
        _inproj_kernel,
        grid=(t // tm,),
        in_specs=[
            pl.BlockSpec((tm, D_MODEL), row),
            pl.BlockSpec((1, D_MODEL), const),
            pl.BlockSpec(w_in.shape, const),
            pl.BlockSpec(w_vt.shape, const),
        ],
        out_specs=[
            pl.BlockSpec((tm, ATTN_WIDTH), row),
            pl.BlockSpec((tm, ATTN_WIDTH), row),
            pl.BlockSpec((1, ATTN_WIDTH, tm), lambda i: (i, 0, 0)),
            pl.BlockSpec((tm, POOL_WIDTH), row),
        ],
        out_shape=[
            jax.ShapeDtypeStruct((t, ATTN_WIDTH), BF16),
            jax.ShapeDtypeStruct((t, ATTN_WIDTH), BF16),
            jax.ShapeDtypeStruct((t // tm, ATTN_WIDTH, tm), BF16),
            jax.ShapeDtypeStruct((t, POOL_WIDTH), F32),
        ],
        compiler_params=_params("parallel"),
        name="inproj",
    )(x, g, w_in, w_vt)


def _attn_kernel(far_ref, q_ref, qn_ref, k_ref, vt_ref, bias_ref, lq_ref, lk_ref, hn_ref, o_ref,
                 qs_scr, sa_scr, sb_scr, m_scr, l_scr, acc_scr, *, tile, nk, lam_init):
    h = pl.program_id(1)
    i = pl.program_id(2)
    t = tile
    cq = min(MXU_WIDTH, t)
    n_chunks = 2 * t // cq

    cur = lax.rem(i, 2)
    nxt = 1 - cur

    def stack_maps(slot, q):
        lane = lax.broadcasted_iota(jnp.int32, (t, LANES), 1)
        zero = jnp.zeros_like(q)
        qs_scr[slot, 0:t, :] = jnp.where(lane < DIFF_HEAD_DIM, q, zero)
        qs_scr[slot, t:2 * t, :] = jnp.where(lane >= DIFF_HEAD_DIM, q, zero)

    def scores(j, slot):
        start = pl.multiple_of(j * t, t)
        return lax.dot_general(k_ref[pl.ds(start, t), :], qs_scr[slot], NT_DIMS,
                               preferred_element_type=F32)

    n_pairs = nk // 2

    def plan(qi):
        first_near = lax.div(qi + 1, 2) - 1
        n_left = jnp.maximum(first_near, 0)
        right_start = first_near + 2
        n_far = n_left + jnp.maximum(n_pairs - right_start, 0)
        return n_left, right_start, n_far, n_left, jnp.minimum(first_near + 1, n_pairs - 1)

    def first_tile(qi):
        n_left_q, right_start_q, n_far_q, near_lo_q, _ = plan(qi)
        first_far = jnp.where(n_left_q > 0, 0, right_start_q)
        return 2 * jnp.where(n_far_q > 0, first_far, near_lo_q)

    @pl.when(i == 0)
    def _():
        stack_maps(0, q_ref[...])
        sa_scr[...] = scores(first_tile(0), 0)

    stack_maps(nxt, qn_ref[...])
    m_scr[...] = jnp.full(m_scr.shape, MASK_VALUE, F32)
    l_scr[...] = jnp.zeros(l_scr.shape, F32)
    acc_scr[...] = jnp.zeros(acc_scr.shape, F32)

    def softmax_pv(j, s_ref, near, c):
        vtj = vt_ref[j]
        for ci in range(n_chunks):
            cols = slice(ci * cq, (ci + 1) * cq)
            s = s_ref[:, cols]
            if near:
                q0 = (ci * cq) % t
                s = s + bias_ref[0, j - i + NEAR_REACH, :, q0:q0 + cq]
            m_prev = m_scr[:, cols]
            m_cur = jnp.max(s, axis=0, keepdims=True)
            if c is not None:
                m_cur = m_cur + c
            m_next = jnp.maximum(m_prev, m_cur)
            p = jnp.exp2(s - (m_next if c is None else m_next - c))
            alpha = jnp.exp2(m_prev - m_next)
            l_scr[:, cols] = alpha * l_scr[:, cols] + jnp.sum(p, axis=0, keepdims=True)
            acc_scr[:, cols] = alpha * acc_scr[:, cols] + jnp.dot(vtj, p.astype(BF16),
                                                                  preferred_element_type=F32)
            m_scr[:, cols] = m_next

    def pair(m, near, c, ahead_tile, ahead_slot):
        j0 = 2 * m
        sb_scr[...] = scores(j0 + 1, cur)
        softmax_pv(j0, sa_scr, near, c)
        sa_scr[...] = scores(ahead_tile, ahead_slot)
        softmax_pv(j0 + 1, sb_scr, near, c)

    n_left, right_start, n_far, near_lo, near_hi = plan(i)
    next_first = first_tile(jnp.minimum(i + 1, nk - 1))

    def far_pair(f):
        def index(g):
            return jnp.where(g < n_left, g, g - n_left + right_start)
        ahead = jnp.where(f + 1 < n_far, index(f + 1), near_lo)
        c = jnp.where(f < n_left, far_ref[h, 0], far_ref[h, 1])
        pair(index(f), False, c, 2 * ahead, cur)

    lead = jnp.where(n_far >= NEAR_LEAD, NEAR_LEAD, 0)
    n_loop = n_far - lead
    size = 1
    while size < FAR_BLOCK:
        @pl.when(jnp.bitwise_and(n_loop, size) != 0)
        def _(size=size):
            base = jnp.bitwise_and(n_loop, size - 1)
            for u in range(size):
                far_pair(base + u)
        size *= 2

    def far_block(u, carry):
        base = jnp.bitwise_and(n_loop, FAR_BLOCK - 1) + FAR_BLOCK * u
        for v in range(FAR_BLOCK):
            far_pair(base + v)
        return carry

    lax.fori_loop(0, lax.div(n_loop, FAR_BLOCK), far_block, 0)

    def near_block(n_lead, both):
        for v in range(n_lead):
            far_pair(n_loop + v)
        if both:
            pair(near_lo, True, None, 2 * near_hi, cur)
            pair(near_hi, True, None, next_first, nxt)
        else:
            pair(near_lo, True, None, next_first, nxt)

    for n_lead in sorted({0, NEAR_LEAD}):
        for both in (True, False):
            @pl.when(jnp.logical_and(lead == n_lead, (near_hi > near_lo) == both))
            def _(n_lead=n_lead, both=both):
                near_block(n_lead, both)

    o = acc_scr[...] * (1.0 / l_scr[...])
    prod = lq_ref[...] * lk_ref[...]
    lam = (jnp.exp(jnp.sum(prod[0:1], axis=1, keepdims=True))
           - jnp.exp(jnp.sum(prod[1:2], axis=1, keepdims=True)) + lam_init)
    a = (o[:, 0:t] - lam * o[:, t:2 * t]).T
    o_ref[...] = (_rms(a, hn_ref[...]) * (1.0 - lam_init)).astype(o_ref.dtype)


def _attention(q, k, vt, bias_near_t, far, lam_q, lam_k, head_norm, lam_init, tile, seq_len):
    t_all = q.shape[0]
    b = t_all // seq_len
    nk = seq_len // tile
    assert nk % 2 == 0, "key tiles are processed in pairs"
    kern = functools.partial(_attn_kernel, tile=tile, nk=nk, lam_init=lam_init)
    return pl.pallas_call(
        kern,
        grid=(b, N_DIFF_HEADS, nk),
        in_specs=[
            pl.BlockSpec(memory_space=pltpu.SMEM),
            pl.BlockSpec((tile, V_HEAD_DIM), lambda bi, h, i: (bi * nk + i, h)),
            pl.BlockSpec((tile, V_HEAD_DIM), lambda bi, h, i: (bi * nk + jnp.minimum(i + 1, nk - 1), h)),
            pl.BlockSpec((seq_len, V_HEAD_DIM), lambda bi, h, i: (bi, h)),
            pl.BlockSpec((nk, V_HEAD_DIM, tile), lambda bi, h, i: (bi, h, 0)),
            pl.BlockSpec((1, 2 * NEAR_REACH + 1, tile, tile), lambda bi, h, i: (h, 0, 0, 0)),
            pl.BlockSpec((2, DIFF_HEAD_DIM), lambda bi, h, i: (0, 0)),
            pl.BlockSpec((2, DIFF_HEAD_DIM), lambda bi, h, i: (0, 0)),
            pl.BlockSpec((1, V_HEAD_DIM), lambda bi, h, i: (0, 0)),
        ],
        out_specs=pl.BlockSpec((tile, V_HEAD_DIM), lambda bi, h, i: (bi * nk + i, h)),
        out_shape=jax.ShapeDtypeStruct((t_all, ATTN_WIDTH), BF16),
        scratch_shapes=[
            pltpu.VMEM((2, 2 * tile, V_HEAD_DIM), BF16),
            pltpu.VMEM((tile, 2 * tile), F32),
            pltpu.VMEM((tile, 2 * tile), F32),
            pltpu.VMEM((1, 2 * tile), F32),
            pltpu.VMEM((1, 2 * tile), F32),
            pltpu.VMEM((V_HEAD_DIM, 2 * tile), F32),
        ],
        compiler_params=_params("parallel", "parallel", "arbitrary"),
        name="diff_attention",
    )(far, q, q, k, vt, bias_near_t, lam_q, lam_k, head_norm)


def _mixout_kernel(a_ref, p_ref, pprev_ref, pnext_ref, x_ref, wout_ref, wpool_ref, ps_ref, g_ref,
                   o_ref, pbuf, *, tm, tiles_per_seq, seq_len):
    ti = lax.rem(pl.program_id(0), tiles_per_seq)
    keep_prev = (ti > 0).astype(F32)
    keep_next = (ti < tiles_per_seq - 1).astype(F32)
    pbuf[0:HALO, :] = pprev_ref[...] * keep_prev
    pbuf[HALO:HALO + tm, :] = p_ref[...]
    pbuf[HALO + tm:, :] = pnext_ref[...] * keep_next

    pos = ti * tm + lax.broadcasted_iota(jnp.int32, (tm, 1), 0)
    mixed = []
    for g, win in enumerate(POOL_WINDOWS):
        lo_off = -(win // 2)
        hi_off = win - win // 2 - 1
        cols = slice(g * POOL_GROUP, (g + 1) * POOL_GROUP)
        total = pbuf[HALO + lo_off:HALO + lo_off + tm, cols]
        for off in range(lo_off + 1, hi_off + 1):
            total = total + pbuf[HALO + off:HALO + off + tm, cols]
        lo = jnp.maximum(pos + lo_off, 0)
        hi = jnp.minimum(pos + hi_off, seq_len - 1)
        cnt = (hi - lo + 1).astype(F32)
        y = total / cnt - pbuf[HALO:HALO + tm, cols]
        mixed.append(jnp.dot(y.astype(BF16), wpool_ref[g], preferred_element_type=F32))
    m = jnp.concatenate(mixed, axis=1) * ps_ref[...]

    o = jnp.dot(a_ref[...], wout_ref[0:ATTN_WIDTH, :], preferred_element_type=F32)
    o = o + jnp.dot(m.astype(BF16), wout_ref[ATTN_WIDTH:, :], preferred_element_type=F32)
    o_ref[...] = x_ref[...] + _rms(o, g_ref[...])


def _halo_specs(tm, width, n_rows):
    per_tile = tm // HALO
    last = n_rows // HALO - 1
    prev = pl.BlockSpec((HALO, width), lambda i, *_: (jnp.maximum(i * per_tile - 1, 0), 0))
    nxt = pl.BlockSpec((HALO, width), lambda i, *_: (jnp.minimum((i + 1) * per_tile, last), 0))
    return prev, nxt


def _mixout(a, p, x, w_out, w_pool, pool_scale, g, seq_len, tm):
    t = x.shape[0]
    row = lambda i: (i, 0)
    const = lambda i: (0, 0)
    prev, nxt = _halo_specs(tm, POOL_WIDTH, t)
    kern = functools.partial(_mixout_kernel, tm=tm, tiles_per_seq=seq_len // tm, seq_len=seq_len)
    return pl.pallas_call(
        kern,
        grid=(t // tm,),
        in_specs=[
            pl.BlockSpec((tm, ATTN_WIDTH), row),
            pl.BlockSpec((tm, POOL_WIDTH), row),
            prev,
            nxt,
            pl.BlockSpec((tm, D_MODEL), row),
            pl.BlockSpec(w_out.shape, const),
            pl.BlockSpec(w_pool.shape, lambda i: (0, 0, 0)),
            pl.BlockSpec((1, POOL_WIDTH), const),
            pl.BlockSpec((1, D_MODEL), const),
        ],
        out_specs=pl.BlockSpec((tm, D_MODEL), row),
        out_shape=jax.ShapeDtypeStruct((t, D_MODEL), F32),
        scratch_shapes=[pltpu.VMEM((tm + 2 * HALO, POOL_WIDTH), F32)],
        compiler_params=_params("parallel"),
        name="pool_outproj",
    )(a, p, p, p, x, w_out, w_pool, pool_scale, g)


def _gelu_tanh(x):
    return 0.5 * x * (1.0 + jnp.tanh(math.sqrt(2.0 / math.pi) * (x + 0.044715 * (x * x * x))))


def _ffn_kernel(x_ref, xprev_ref, xnext_ref, gpre_ref, wg_ref, wv_ref, cwg_ref, cwv_ref, cbg_ref, cbv_ref,
                wd_ref, gpost_ref, o_ref, h_scr, u_scr, acc_scr, *, tm, tiles_per_seq):
    c = pl.program_id(1)

    @pl.when(c == 0)
    def _():
        ti = lax.rem(pl.program_id(0), tiles_per_seq)
        keep_prev = (ti > 0).astype(F32)
        keep_next = (ti < tiles_per_seq - 1).astype(F32)
        g = gpre_ref[...]
        h_scr[0:HALO, :] = _rms(xprev_ref[...], g) * keep_prev
        h_scr[HALO:HALO + tm, :] = _rms(x_ref[...], g)
        h_scr[HALO + tm:, :] = _rms(xnext_ref[...], g) * keep_next
        acc_scr[...] = jnp.zeros(acc_scr.shape, F32)

    h = h_scr[...].astype(BF16)

    def conv(w_ref, cw_ref, cb_ref):
        u_scr[...] = jnp.dot(h, w_ref[...], preferred_element_type=F32)
        cw = cw_ref[...]
        return (u_scr[HALO - 1:HALO - 1 + tm, :] * cw[0:1]
                + u_scr[HALO:HALO + tm, :] * cw[1:2]
                + u_scr[HALO + 1:HALO + 1 + tm, :] * cw[2:3]
                + cb_ref[...])

    gate = conv(wg_ref, cwg_ref, cbg_ref)
    val = conv(wv_ref, cwv_ref, cbv_ref)
    f = (_gelu_tanh(gate) * val).astype(BF16)
    acc_scr[...] += jnp.dot(f, wd_ref[...], preferred_element_type=F32)

    @pl.when(c == pl.num_programs(1) - 1)
    def _():
        o_ref[...] = x_ref[...] + _rms(acc_scr[...], gpost_ref[...])


def _ffn(x, g_pre, w_up, conv_w, conv_b, w_down, g_post, seq_len, tm):
    t = x.shape[0]
    n_chunks = D_FF // FF_CHUNK
    row = lambda i, c: (i, 0)
    const = lambda i, c: (0, 0)
    gate_cols = lambda i, c: (0, c)
    val_cols = lambda i, c: (0, n_chunks + c)
    prev, nxt = _halo_specs(tm, D_MODEL, t)
    kern = functools.partial(_ffn_kernel, tm=tm, tiles_per_seq=seq_len // tm)
    return pl.pallas_call(
        kern,
        grid=(t // tm, n_chunks),
        in_specs=[
            pl.BlockSpec((tm, D_MODEL), row),
            prev,
            nxt,
            pl.BlockSpec((1, D_MODEL), const),
            pl.BlockSpec((D_MODEL, FF_CHUNK), gate_cols),
            pl.BlockSpec((D_MODEL, FF_CHUNK), val_cols),
            pl.BlockSpec((3, FF_CHUNK), gate_cols),
            pl.BlockSpec((3, FF_CHUNK), val_cols),
            pl.BlockSpec((1, FF_CHUNK), gate_cols),
            pl.BlockSpec((1, FF_CHUNK), val_cols),
            pl.BlockSpec((FF_CHUNK, D_MODEL), lambda i, c: (c, 0)),
            pl.BlockSpec((1, D_MODEL), const),
        ],
        out_specs=pl.BlockSpec((tm, D_MODEL), row),
        out_shape=jax.ShapeDtypeStruct((t, D_MODEL), F32),
        scratch_shapes=[
            pltpu.VMEM((tm + 2 * HALO, D_MODEL), F32),
            pltpu.VMEM((tm + 2 * HALO, FF_CHUNK), F32),
            pltpu.VMEM((tm, D_MODEL), F32),
        ],
        compiler_params=_params("parallel", "arbitrary"),
        name="conv_mlp",
    )(x, x, x, g_pre, w_up, w_up, conv_w, conv_w, conv_b, conv_b, w_down, g_post)


def _rel_bucket(rel):
    nb = NUM_BUCKETS // 2
    max_exact = nb // 2
    ret = jnp.where(rel > 0, nb, 0)
    n = jnp.abs(rel)
    nf = jnp.maximum(n, 1).astype(F32)
    large = max_exact + (jnp.log(nf / max_exact) / math.log(MAX_DISTANCE / max_exact)
                         * (nb - max_exact)).astype(jnp.int32)
    large = jnp.minimum(large, nb - 1)
    return ret + jnp.where(n < max_exact, n, large)


def _bias_expand_kernel(y_ref, o_ref, *, tile):
    span = y_ref.shape[-1]
    table = jnp.broadcast_to(y_ref[0], (tile, span))
    skewed = pltpu.roll(table, 0, 1, stride=1, stride_axis=0)
    for di, d in enumerate(range(-NEAR_REACH, NEAR_REACH + 1)):
        start = (-d * tile) % span
        o_ref[0, di] = skewed[:, start:start + tile]


def _bias_tables(rel_bias, tile):
    assert tile >= MAX_DISTANCE, "far tiles must lie beyond the last distance bucket"
    n_near = 2 * NEAR_REACH + 1
    span = 2 * (NEAR_REACH + 1) * tile
    m = jnp.arange(span, dtype=jnp.int32)
    rel = jnp.where(m < span // 2, -m, span - m)
    rel_bias = rel_bias.astype(F32) * LOG2E
    by_rel = rel_bias[_rel_bucket(rel)].T[:, None, :]
    near = pl.pallas_call(
        functools.partial(_bias_expand_kernel, tile=tile),
        grid=(N_DIFF_HEADS,),
        in_specs=[pl.BlockSpec((1, 1, span), lambda h: (h, 0, 0))],
        out_specs=pl.BlockSpec((1, n_near, tile, tile), lambda h: (h, 0, 0, 0)),
        out_shape=jax.ShapeDtypeStruct((N_DIFF_HEADS, n_near, tile, tile), F32),
        compiler_params=_params("parallel"),
        name="bias_expand",
    )(by_rel)
    far = jnp.stack([rel_bias[NUM_BUCKETS // 2 - 1], rel_bias[NUM_BUCKETS - 1]], axis=1)
    return near, far


def kernel(x_prompt, x_sample, rel_bias, ln_mix_pre, ln_mix_post, w_in, lam_q, lam_k, head_norm, w_pool,
           pool_scale, w_out, ln_ffn_pre, ln_ffn_post, w_up, conv_w, conv_b, w_down):
    depth = w_in.shape[0]
    w_in_b, w_out_b, w_pool_b = w_in.astype(BF16), w_out.astype(BF16), w_pool.astype(BF16)
    w_up_b, w_down_b = w_up.astype(BF16), w_down.astype(BF16)
    w_vt_b = jnp.swapaxes(w_in_b[:, :, 2 * ATTN_WIDTH:3 * ATTN_WIDTH], 1, 2)
    tables = {}

    def trunk(x):
        b, s, d = x.shape
        tile = min(ATTN_TILE, s)
        tm = min(ROW_TILE, s)
        assert s % tile == 0 and s % tm == 0 and tm % HALO == 0
        if tile not in tables:
            tables[tile] = _bias_tables(rel_bias, tile)
        bias_near_t, far = tables[tile]
        xf = x.reshape(b * s, d)
        for i in range(depth):
            lam_init = 0.8 - 0.6 * math.exp(-0.3 * i)
            q, k, vt, p = _inproj(xf, ln_mix_pre[i][None], w_in_b[i], w_vt_b[i], tile)
            a = _attention(q, k, vt, bias_near_t, far, lam_q[i], lam_k[i], head_norm[i][None], lam_init,
                           tile, s)
            xf = _mixout(a, p, xf, w_out_b[i], w_pool_b[i], pool_scale[i][None], ln_mix_post[i][None], s, tm)
            xf = _ffn(xf, ln_ffn_pre[i][None], w_up_b[i], conv_w[i], conv_b[i][None], w_down_b[i],
                      ln_ffn_post[i][None], s, tm)
        return xf.reshape(b, s, d)

    return (trunk(x_prompt), trunk(x_sample))
```

```python
import functools
import math

import jax
import jax.numpy as jnp
from jax import lax
from jax.experimental import pallas as pl
from jax.experimental.pallas import tpu as pltpu

D_MODEL = 1024
ATTN_WIDTH = 512
POOL_WIDTH = D_MODEL - ATTN_WIDTH
N_DIFF_HEADS = 4
DIFF_HEAD_DIM = 64
V_HEAD_DIM = 2 * DIFF_HEAD_DIM
POOL_WINDOWS = (2, 4, 8, 16)
POOL_GROUP = POOL_WIDTH // len(POOL_WINDOWS)
D_FF = 2816
NUM_BUCKETS = 32
MAX_DISTANCE = 128
RMS_EPS = 1e-6
QK_SCALE = DIFF_HEAD_DIM ** -0.5
LOG2E = math.log2(math.e)

LANES = 128
SUBLANES = 8
MXU_WIDTH = 256
ATTN_TILE = 512
ROW_TILE = 512
FF_CHUNK = D_FF // 2
HALO = SUBLANES
FAR_BLOCK = 8
NEAR_LEAD = 6
NEAR_REACH = 2
MASK_VALUE = -1e30
VMEM_LIMIT_BYTES = 48 * 1024 * 1024

F32 = jnp.float32
BF16 = jnp.bfloat16
NT_DIMS = (((1,), (1,)), ((), ()))


def _rms(x, g):
    var = jnp.mean(x * x, axis=-1, keepdims=True)
    return x * lax.rsqrt(var + RMS_EPS) * g


def _params(*semantics):
    return pltpu.CompilerParams(dimension_semantics=semantics, vmem_limit_bytes=VMEM_LIMIT_BYTES)


def _inproj_kernel(x_ref, g_ref, w_ref, wvt_ref, q_ref, k_ref, vt_ref, p_ref):
    h = _rms(x_ref[...], g_ref[...]).astype(BF16)
    a = ATTN_WIDTH
    q = jnp.dot(h, w_ref[:, 0:a], preferred_element_type=F32)
    q_ref[...] = (q * (QK_SCALE * LOG2E)).astype(BF16)
    k_ref[...] = jnp.dot(h, w_ref[:, a:2 * a], preferred_element_type=F32).astype(BF16)
    p_ref[...] = jnp.dot(h, w_ref[:, 3 * a:], preferred_element_type=F32)
    vt_ref[0] = lax.dot_general(wvt_ref[...], h, NT_DIMS, preferred_element_type=F32).astype(BF16)


def _inproj(x, g, w_in, w_vt, tm):
    t = x.shape[0]
    row = lambda i: (i, 0)
    const = lambda i: (0, 0)
    return pl.pallas_call(
        _inproj_kernel,
        grid=(t // tm,),
        in_specs=[
            pl.BlockSpec((tm, D_MODEL), row),
            pl.BlockSpec((1, D_MODEL), const),
            pl.BlockSpec(w_in.shape, const),
            pl.BlockSpec(w_vt.shape, const),
        ],
        out_specs=[
            pl.BlockSpec((tm, ATTN_WIDTH), row),
            pl.BlockSpec((tm, ATTN_WIDTH), row),
            pl.BlockSpec((1, ATTN_WIDTH, tm), lambda i: (i, 0, 0)),
            pl.BlockSpec((tm, POOL_WIDTH), row),
        ],
        out_shape=[
            jax.ShapeDtypeStruct((t, ATTN_WIDTH), BF16),
            jax.ShapeDtypeStruct((t, ATTN_WIDTH), BF16),
            jax.ShapeDtypeStruct((t // tm, ATTN_WIDTH, tm), BF16),
            jax.ShapeDtypeStruct((t, POOL_WIDTH), F32),
        ],
        compiler_params=_params("parallel"),
        name="inproj",
    )(x, g, w_in, w_vt)


def _attn_kernel(far_ref, q_ref, qn_ref, k_ref, vt_ref, bias_ref, lq_ref, lk_ref, hn_ref, o_ref,
                 qs_scr, sa_scr, sb_scr, m_scr, l_scr, acc_scr, *, tile, nk, lam_init):
    h = pl.program_id(1)
    i = pl.program_id(2)
    t = tile
    cq = min(MXU_WIDTH, t)
    n_chunks = 2 * t // cq

    cur = lax.rem(i, 2)
    nxt = 1 - cur

    def stack_maps(slot, q):
        lane = lax.broadcasted_iota(jnp.int32, (t, LANES), 1)
        zero = jnp.zeros_like(q)
        qs_scr[slot, 0:t, :] = jnp.where(lane < DIFF_HEAD_DIM, q, zero)
        qs_scr[slot, t:2 * t, :] = jnp.where(lane >= DIFF_HEAD_DIM, q, zero)

    def scores(j, slot):
        start = pl.multiple_of(j * t, t)
        return lax.dot_general(k_ref[pl.ds(start, t), :], qs_scr[slot], NT_DIMS,
                               preferred_element_type=F32)

    n_pairs = nk // 2

    def plan(qi):
        first_near = lax.div(qi + 1, 2) - 1
        n_left = jnp.maximum(first_near, 0)
        right_start = first_near + 2
        n_far = n_left + jnp.maximum(n_pairs - right_start, 0)
        return n_left, right_start, n_far, n_left, jnp.minimum(first_near + 1, n_pairs - 1)

    def first_tile(qi):
        n_left_q, right_start_q, n_far_q, near_lo_q, _ = plan(qi)
        first_far = jnp.where(n_left_q > 0, 0, right_start_q)
        return 2 * jnp.where(n_far_q > 0, first_far, near_lo_q)

    @pl.when(i == 0)
    def _():
        stack_maps(0, q_ref[...])
        sa_scr[...] = scores(first_tile(0), 0)

    stack_maps(nxt, qn_ref[...])
    m_scr[...] = jnp.full(m_scr.shape, MASK_VALUE, F32)
    l_scr[...] = jnp.zeros(l_scr.shape, F32)
    acc_scr[...] = jnp.zeros(acc_scr.shape, F32)

    def softmax_pv(j, s_ref, near, c):
        vtj = vt_ref[j]
        for ci in range(n_chunks):
            cols = slice(ci * cq, (ci + 1) * cq)
            s = s_ref[:, cols]
            if near:
                q0 = (ci * cq) % t
                s = s + bias_ref[0, j - i + NEAR_REACH, :, q0:q0 + cq]
            m_prev = m_scr[:, cols]
            m_cur = jnp.max(s, axis=0, keepdims=True)
            if c is not None:
                m_cur = m_cur + c
            m_next = jnp.maximum(m_prev, m_cur)
            p = jnp.exp2(s - (m_next if c is None else m_next - c))
            alpha = jnp.exp2(m_prev - m_next)
            l_scr[:, cols] = alpha * l_scr[:, cols] + jnp.sum(p, axis=0, keepdims=True)
            acc_scr[:, cols] = alpha * acc_scr[:, cols] + jnp.dot(vtj, p.astype(BF16),
                                                                  preferred_element_type=F32)
            m_scr[:, cols] = m_next

    def pair(m, near, c, ahead_tile, ahead_slot):
        j0 = 2 * m
        sb_scr[...] = scores(j0 + 1, cur)
        softmax_pv(j0, sa_scr, near, c)
        sa_scr[...] = scores(ahead_tile, ahead_slot)
        softmax_pv(j0 + 1, sb_scr, near, c)

    n_left, right_start, n_far, near_lo, near_hi = plan(i)
    next_first = first_tile(jnp.minimum(i + 1, nk - 1))

    def far_pair(f):
        def index(g):
            return jnp.where(g < n_left, g, g - n_left + right_start)
        ahead = jnp.where(f + 1 < n_far, index(f + 1), near_lo)
        c = jnp.where(f < n_left, far_ref[h, 0], far_ref[h, 1])
        pair(index(f), False, c, 2 * ahead, cur)

    lead = jnp.where(n_far >= NEAR_LEAD, NEAR_LEAD, 0)
    n_loop = n_far - lead
    size = 1
    while size < FAR_BLOCK:
        @pl.when(jnp.bitwise_and(n_loop, size) != 0)
        def _(size=size):
            base = jnp.bitwise_and(n_loop, size - 1)
            for u in range(size):
                far_pair(base + u)
        size *= 2

    def far_block(u, carry):
        base = jnp.bitwise_and(n_loop, FAR_BLOCK - 1) + FAR_BLOCK * u
        for v in range(FAR_BLOCK):
            far_pair(base + v)
        return carry

    lax.fori_loop(0, lax.div(n_loop, FAR_BLOCK), far_block, 0)

    def near_block(n_lead, both):
        for v in range(n_lead):
            far_pair(n_loop + v)
        if both:
            pair(near_lo, True, None, 2 * near_hi, cur)
            pair(near_hi, True, None, next_first, nxt)
        else:
            pair(near_lo, True, None, next_first, nxt)

    for n_lead in sorted({0, NEAR_LEAD}):
        for both in (True, False):
            @pl.when(jnp.logical_and(lead == n_lead, (near_hi > near_lo) == both))
            def _(n_lead=n_lead, both=both):
                near_block(n_lead, both)

    o = acc_scr[...] * (1.0 / l_scr[...])
    prod = lq_ref[...] * lk_ref[...]
    lam = (jnp.exp(jnp.sum(prod[0:1], axis=1, keepdims=True))
           - jnp.exp(jnp.sum(prod[1:2], axis=1, keepdims=True)) + lam_init)
    a = (o[:, 0:t] - lam * o[:, t:2 * t]).T
    o_ref[...] = (_rms(a, hn_ref[...]) * (1.0 - lam_init)).astype(o_ref.dtype)


def _attention(q, k, vt, bias_near_t, far, lam_q, lam_k, head_norm, lam_init, tile, seq_len):
    t_all = q.shape[0]
    b = t_all // seq_len
    nk = seq_len // tile
    assert nk % 2 == 0, "key tiles are processed in pairs"
    kern = functools.partial(_attn_kernel, tile=tile, nk=nk, lam_init=lam_init)
    return pl.pallas_call(
        kern,
        grid=(b, N_DIFF_HEADS, nk),
        in_specs=[
            pl.BlockSpec(memory_space=pltpu.SMEM),
            pl.BlockSpec((tile, V_HEAD_DIM), lambda bi, h, i: (bi * nk + i, h)),
            pl.BlockSpec((tile, V_HEAD_DIM), lambda bi, h, i: (bi * nk + jnp.minimum(i + 1, nk - 1), h)),
            pl.BlockSpec((seq_len, V_HEAD_DIM), lambda bi, h, i: (bi, h)),
            pl.BlockSpec((nk, V_HEAD_DIM, tile), lambda bi, h, i: (bi, h, 0)),
            pl.BlockSpec((1, 2 * NEAR_REACH + 1, tile, tile), lambda bi, h, i: (h, 0, 0, 0)),
            pl.BlockSpec((2, DIFF_HEAD_DIM), lambda bi, h, i: (0, 0)),
            pl.BlockSpec((2, DIFF_HEAD_DIM), lambda bi, h, i: (0, 0)),
            pl.BlockSpec((1, V_HEAD_DIM), lambda bi, h, i: (0, 0)),
        ],
        out_specs=pl.BlockSpec((tile, V_HEAD_DIM), lambda bi, h, i: (bi * nk + i, h)),
        out_shape=jax.ShapeDtypeStruct((t_all, ATTN_WIDTH), BF16),
        scratch_shapes=[
            pltpu.VMEM((2, 2 * tile, V_HEAD_DIM), BF16),
            pltpu.VMEM((tile, 2 * tile), F32),
            pltpu.VMEM((tile, 2 * tile), F32),
            pltpu.VMEM((1, 2 * tile), F32),
            pltpu.VMEM((1, 2 * tile), F32),
            pltpu.VMEM((V_HEAD_DIM, 2 * tile), F32),
        ],
        compiler_params=_params("parallel", "parallel", "arbitrary"),
        name="diff_attention",
    )(far, q, q, k, vt, bias_near_t, lam_q, lam_k, head_norm)


def _mixout_kernel(a_ref, p_ref, pprev_ref, pnext_ref, x_ref, wout_ref, wpool_ref, ps_ref, g_ref,
                   o_ref, pbuf, *, tm, tiles_per_seq, seq_len):
    ti = lax.rem(pl.program_id(0), tiles_per_seq)
    keep_prev = (ti > 0).astype(F32)
    keep_next = (ti < tiles_per_seq - 1).astype(F32)
    pbuf[0:HALO, :] = pprev_ref[...] * keep_prev
    pbuf[HALO:HALO + tm, :] = p_ref[...]
    pbuf[HALO + tm:, :] = pnext_ref[...] * keep_next

    pos = ti * tm + lax.broadcasted_iota(jnp.int32, (tm, 1), 0)
    mixed = []
    for g, win in enumerate(POOL_WINDOWS):
        lo_off = -(win // 2)
        hi_off = win - win // 2 - 1
        cols = slice(g * POOL_GROUP, (g + 1) * POOL_GROUP)
        total = pbuf[HALO + lo_off:HALO + lo_off + tm, cols]
        for off in range(lo_off + 1, hi_off + 1):
            total = total + pbuf[HALO + off:HALO + off + tm, cols]
        lo = jnp.maximum(pos + lo_off, 0)
        hi = jnp.minimum(pos + hi_off, seq_len - 1)
        inv_cnt = 1.0 / (hi - lo + 1).astype(F32)
        y = total * inv_cnt - pbuf[HALO:HALO + tm, cols]
        mixed.append(jnp.dot(y.astype(BF16), wpool_ref[g], preferred_element_type=F32))
    m = jnp.concatenate(mixed, axis=1) * ps_ref[...]

    o = jnp.dot(a_ref[...], wout_ref[0:ATTN_WIDTH, :], preferred_element_type=F32)
    o = o + jnp.dot(m.astype(BF16), wout_ref[ATTN_WIDTH:, :], preferred_element_type=F32)
    o_ref[...] = x_ref[...] + _rms(o, g_ref[...])


def _halo_specs(tm, width, n_rows):
    per_tile = tm // HALO
    last = n_rows // HALO - 1
    prev = pl.BlockSpec((HALO, width), lambda i, *_: (jnp.maximum(i * per_tile - 1, 0), 0))
    nxt = pl.BlockSpec((HALO, width), lambda i, *_: (jnp.minimum((i + 1) * per_tile, last), 0))
    return prev, nxt


def _mixout(a, p, x, w_out, w_pool, pool_scale, g, seq_len, tm):
    t = x.shape[0]
    row = lambda i: (i, 0)
    const = lambda i: (0, 0)
    prev, nxt = _halo_specs(tm, POOL_WIDTH, t)
    kern = functools.partial(_mixout_kernel, tm=tm, tiles_per_seq=seq_len // tm, seq_len=seq_len)
    return pl.pallas_call(
        kern,
        grid=(t // tm,),
        in_specs=[
            pl.BlockSpec((tm, ATTN_WIDTH), row),
            pl.BlockSpec((tm, POOL_WIDTH), row),
            prev,
            nxt,
            pl.BlockSpec((tm, D_MODEL), row),
            pl.BlockSpec(w_out.shape, const),
            pl.BlockSpec(w_pool.shape, lambda i: (0, 0, 0)),
            pl.BlockSpec((1, POOL_WIDTH), const),
            pl.BlockSpec((1, D_MODEL), const),
        ],
        out_specs=pl.BlockSpec((tm, D_MODEL), row),
        out_shape=jax.ShapeDtypeStruct((t, D_MODEL), F32),
        scratch_shapes=[pltpu.VMEM((tm + 2 * HALO, POOL_WIDTH), F32)],
        compiler_params=_params("parallel"),
        name="pool_outproj",
    )(a, p, p, p, x, w_out, w_pool, pool_scale, g)


def _gelu_tanh_times(x, v):
    a = math.sqrt(2.0 / math.pi)
    inner = x * (a + (a * 0.044715) * (x * x))
    return (0.5 * (x * v)) * (1.0 + jnp.tanh(inner))


def _ffn_kernel(x_ref, xprev_ref, xnext_ref, gpre_ref, wg_ref, wv_ref, cwg_ref, cwv_ref, cbg_ref, cbv_ref,
                wd_ref, gpost_ref, o_ref, h_scr, u_scr, acc_scr, *, tm, tiles_per_seq):
    c = pl.program_id(1)

    @pl.when(c == 0)
    def _():
        ti = lax.rem(pl.program_id(0), tiles_per_seq)
        keep_prev = (ti > 0).astype(F32)
        keep_next = (ti < tiles_per_seq - 1).astype(F32)
        g = gpre_ref[...]
        h_scr[0:HALO, :] = _rms(xprev_ref[...], g) * keep_prev
        h_scr[HALO:HALO + tm, :] = _rms(x_ref[...], g)
        h_scr[HALO + tm:, :] = _rms(xnext_ref[...], g) * keep_next
        acc_scr[...] = jnp.zeros(acc_scr.shape, F32)

    h = h_scr[...].astype(BF16)

    def conv(w_ref, cw_ref, cb_ref):
        u_scr[...] = jnp.dot(h, w_ref[...], preferred_element_type=F32)
        cw = cw_ref[...]
        return (u_scr[HALO - 1:HALO - 1 + tm, :] * cw[0:1]
                + u_scr[HALO:HALO + tm, :] * cw[1:2]
                + u_scr[HALO + 1:HALO + 1 + tm, :] * cw[2:3]
                + cb_ref[...])

    gate = conv(wg_ref, cwg_ref, cbg_ref)
    val = conv(wv_ref, cwv_ref, cbv_ref)
    f = _gelu_tanh_times(gate, val).astype(BF16)
    acc_scr[...] += jnp.dot(f, wd_ref[...], preferred_element_type=F32)

    @pl.when(c == pl.num_programs(1) - 1)
    def _():
        o_ref[...] = x_ref[...] + _rms(acc_scr[...], gpost_ref[...])


def _ffn(x, g_pre, w_up, conv_w, conv_b, w_down, g_post, seq_len, tm):
    t = x.shape[0]
    n_chunks = D_FF // FF_CHUNK
    row = lambda i, c: (i, 0)
    const = lambda i, c: (0, 0)
    gate_cols = lambda i, c: (0, c)
    val_cols = lambda i, c: (0, n_chunks + c)
    prev, nxt = _halo_specs(tm, D_MODEL, t)
    kern = functools.partial(_ffn_kernel, tm=tm, tiles_per_seq=seq_len // tm)
    return pl.pallas_call(
        kern,
        grid=(t // tm, n_chunks),
        in_specs=[
            pl.BlockSpec((tm, D_MODEL), row),
            prev,
            nxt,
            pl.BlockSpec((1, D_MODEL), const),
            pl.BlockSpec((D_MODEL, FF_CHUNK), gate_cols),
            pl.BlockSpec((D_MODEL, FF_CHUNK), val_cols),
            pl.BlockSpec((3, FF_CHUNK), gate_cols),
            pl.BlockSpec((3, FF_CHUNK), val_cols),
            pl.BlockSpec((1, FF_CHUNK), gate_cols),
            pl.BlockSpec((1, FF_CHUNK), val_cols),
            pl.BlockSpec((FF_CHUNK, D_MODEL), lambda i, c: (c, 0)),
            pl.BlockSpec((1, D_MODEL), const),
        ],
        out_specs=pl.BlockSpec((tm, D_MODEL), row),
        out_shape=jax.ShapeDtypeStruct((t, D_MODEL), F32),
        scratch_shapes=[
            pltpu.VMEM((tm + 2 * HALO, D_MODEL), F32),
            pltpu.VMEM((tm + 2 * HALO, FF_CHUNK), F32),
            pltpu.VMEM((tm, D_MODEL), F32),
        ],
        compiler_params=_params("parallel", "arbitrary"),
        name="conv_mlp",
    )(x, x, x, g_pre, w_up, w_up, conv_w, conv_w, conv_b, conv_b, w_down, g_post)


def _rel_bucket(rel):
    nb = NUM_BUCKETS // 2
    max_exact = nb // 2
    ret = jnp.where(rel > 0, nb, 0)
    n = jnp.abs(rel)
    nf = jnp.maximum(n, 1).astype(F32)
    large = max_exact + (jnp.log(nf / max_exact) / math.log(MAX_DISTANCE / max_exact)
                         * (nb - max_exact)).astype(jnp.int32)
    large = jnp.minimum(large, nb - 1)
    return ret + jnp.where(n < max_exact, n, large)


def _bias_expand_kernel(y_ref, o_ref, *, tile):
    span = y_ref.shape[-1]
    table = jnp.broadcast_to(y_ref[0], (tile, span))
    skewed = pltpu.roll(table, 0, 1, stride=1, stride_axis=0)
    for di, d in enumerate(range(-NEAR_REACH, NEAR_REACH + 1)):
        start = (-d * tile) % span
        o_ref[0, di] = skewed[:, start:start + tile]


def _bias_tables(rel_bias, tile):
    assert tile >= MAX_DISTANCE, "far tiles must lie beyond the last distance bucket"
    n_near = 2 * NEAR_REACH + 1
    span = 2 * (NEAR_REACH + 1) * tile
    m = jnp.arange(span, dtype=jnp.int32)
    rel = jnp.where(m < span // 2, -m, span - m)
    rel_bias = rel_bias.astype(F32) * LOG2E
    by_rel = rel_bias[_rel_bucket(rel)].T[:, None, :]
    near = pl.pallas_call(
        functools.partial(_bias_expand_kernel, tile=tile),
        grid=(N_DIFF_HEADS,),
        in_specs=[pl.BlockSpec((1, 1, span), lambda h: (h, 0, 0))],
        out_specs=pl.BlockSpec((1, n_near, tile, tile), lambda h: (h, 0, 0, 0)),
        out_shape=jax.ShapeDtypeStruct((N_DIFF_HEADS, n_near, tile, tile), F32),
        compiler_params=_params("parallel"),
        name="bias_expand",
    )(by_rel)
    far = jnp.stack([rel_bias[NUM_BUCKETS // 2 - 1], rel_bias[NUM_BUCKETS - 1]], axis=1)
    return near, far


def kernel(x_prompt, x_sample, rel_bias, ln_mix_pre, ln_mix_post, w_in, lam_q, lam_k, head_norm, w_pool,
           pool_scale, w_out, ln_ffn_pre, ln_ffn_post, w_up, conv_w, conv_b, w_down):
    depth = w_in.shape[0]
    w_in_b, w_out_b, w_pool_b = w_in.astype(BF16), w_out.astype(BF16), w_pool.astype(BF16)
    w_up_b, w_down_b = w_up.astype(BF16), w_down.astype(BF16)
    w_vt_b = jnp.swapaxes(w_in_b[:, :, 2 * ATTN_WIDTH:3 * ATTN_WIDTH], 1, 2)
    tables = {}

    def trunk(x):
        b, s, d = x.shape
        tile = min(ATTN_TILE, s)
        tm = min(ROW_TILE, s)
        assert s % tile == 0 and s % tm == 0 and tm % HALO == 0
        if tile not in tables:
            tables[tile] = _bias_tables(rel_bias, tile)
        bias_near_t, far = tables[tile]
        xf = x.reshape(b * s, d)
        for i in range(depth):
            lam_init = 0.8 - 0.6 * math.exp(-0.3 * i)
            q, k, vt, p = _inproj(xf, ln_mix_pre[i][None], w_in_b[i], w_vt_b[i], tile)
            a = _attention(q, k, vt, bias_near_t, far, lam_q[i], lam_k[i], head_norm[i][None], lam_init,
                           tile, s)
            xf = _mixout(a, p, xf, w_out_b[i], w_pool_b[i], pool_scale[i][None], ln_mix_post[i][None], s, tm)
            xf = _ffn(xf, ln_ffn_pre[i][None], w_up_b[i], conv_w[i], conv_b[i][None], w_down_b[i],
                      ln_ffn_post[i][None], s, tm)
        return xf.reshape(b, s, d)

    return (trunk(x_prompt), trunk(x_sample))
```

```python
import functools
import math

import jax
import jax.numpy as jnp
from jax import lax
from jax.experimental import pallas as pl
from jax.experimental.pallas import tpu as pltpu

D_MODEL = 1024
ATTN_WIDTH = 512
POOL_WIDTH = D_MODEL - ATTN_WIDTH
N_DIFF_HEADS = 4
DIFF_HEAD_DIM = 64
V_HEAD_DIM = 2 * DIFF_HEAD_DIM
POOL_WINDOWS = (2, 4, 8, 16)
POOL_GROUP = POOL_WIDTH // len(POOL_WINDOWS)
D_FF = 2816
NUM_BUCKETS = 32
MAX_DISTANCE = 128
RMS_EPS = 1e-6
QK_SCALE = DIFF_HEAD_DIM ** -0.5
LOG2E = math.log2(math.e)

LANES = 128
SUBLANES = 8
MXU_WIDTH = 256
ATTN_TILE = 512
ROW_TILE = 512
FF_CHUNK = D_FF // 2
HALO = SUBLANES
FAR_BLOCK = 8
NEAR_LEAD = 6
NEAR_REACH = 2
MASK_VALUE = -1e30
VMEM_LIMIT_BYTES = 48 * 1024 * 1024

F32 = jnp.float32
BF16 = jnp.bfloat16
NT_DIMS = (((1,), (1,)), ((), ()))


def _rms(x, g):
    var = jnp.mean(x * x, axis=-1, keepdims=True)
    return x * lax.rsqrt(var + RMS_EPS) * g


def _params(*semantics):
    return pltpu.CompilerParams(dimension_semantics=semantics, vmem_limit_bytes=VMEM_LIMIT_BYTES)


def _inproj_kernel(x_ref, g_ref, w_ref, wvt_ref, q_ref, k_ref, vt_ref, p_ref):
    h = _rms(x_ref[...], g_ref[...]).astype(BF16)
    a = ATTN_WIDTH
    q = jnp.dot(h, w_ref[:, 0:a], preferred_element_type=F32)
    q_ref[...] = (q * (QK_SCALE * LOG2E)).astype(BF16)
    k_ref[...] = jnp.dot(h, w_ref[:, a:2 * a], preferred_element_type=F32).astype(BF16)
    p_ref[...] = jnp.dot(h, w_ref[:, 3 * a:], preferred_element_type=F32)
    vt_ref[0] = lax.dot_general(wvt_ref[...], h, NT_DIMS, preferred_element_type=F32).astype(BF16)


def _inproj(x, g, w_in, w_vt, tm):
    t = x.shape[0]
    row = lambda i: (i, 0)
    const = lambda i: (0, 0)
    return pl.pallas_call(
        _inproj_kernel,
        grid=(t // tm,),
        in_specs=[
            pl.BlockSpec((tm, D_MODEL), row),
            pl.BlockSpec((1, D_MODEL), const),
            pl.BlockSpec(w_in.shape, const),
            pl.BlockSpec(w_vt.shape, const),
        ],
        out_specs=[
            pl.BlockSpec((tm, ATTN_WIDTH), row),
            pl.BlockSpec((tm, ATTN_WIDTH), row),
            pl.BlockSpec((1, ATTN_WIDTH, tm), lambda i: (i, 0, 0)),
            pl.BlockSpec((tm, POOL_WIDTH), row),
        ],
        out_shape=[
            jax.ShapeDtypeStruct((t, ATTN_WIDTH), BF16),
            jax.ShapeDtypeStruct((t, ATTN_WIDTH), BF16),
            jax.ShapeDtypeStruct((t // tm, ATTN_WIDTH, tm), BF16),
            jax.ShapeDtypeStruct((t, POOL_WIDTH), F32),
        ],
        compiler_params=_params("parallel"),
        name="inproj",
    )(x, g, w_in, w_vt)


def _attn_kernel(far_ref, q_ref, qn_ref, k_ref, vt_ref, bias_ref, lq_ref, lk_ref, hn_ref, o_ref,
                 qs_scr, sa_scr, sb_scr, m_scr, l_scr, acc_scr, *, tile, nk, lam_init):
    h = pl.program_id(1)
    i = pl.program_id(2)
    t = tile
    cq = min(MXU_WIDTH, t)
    n_chunks = 2 * t // cq

    cur = lax.rem(i, 2)
    nxt = 1 - cur

    def stack_maps(slot, q):
        lane = lax.broadcasted_iota(jnp.int32, (t, LANES), 1)
        zero = jnp.zeros_like(q)
        qs_scr[slot, 0:t, :] = jnp.where(lane < DIFF_HEAD_DIM, q, zero)
        qs_scr[slot, t:2 * t, :] = jnp.where(lane >= DIFF_HEAD_DIM, q, zero)

    def scores(j, slot):
        start = pl.multiple_of(j * t, t)
        return lax.dot_general(k_ref[pl.ds(start, t), :], qs_scr[slot], NT_DIMS,
                               preferred_element_type=F32)

    n_pairs = nk // 2

    def plan(qi):
        first_near = lax.div(qi + 1, 2) - 1
        n_left = jnp.maximum(first_near, 0)
        right_start = first_near + 2
        n_far = n_left + jnp.maximum(n_pairs - right_start, 0)
        return n_left, right_start, n_far, n_left, jnp.minimum(first_near + 1, n_pairs - 1)

    def first_tile(qi):
        n_left_q, right_start_q, n_far_q, near_lo_q, _ = plan(qi)
        first_far = jnp.where(n_left_q > 0, 0, right_start_q)
        return 2 * jnp.where(n_far_q > 0, first_far, near_lo_q)

    @pl.when(i == 0)
    def _():
        stack_maps(0, q_ref[...])
        sa_scr[...] = scores(first_tile(0), 0)

    stack_maps(nxt, qn_ref[...])
    m_scr[...] = jnp.full(m_scr.shape, MASK_VALUE, F32)
    l_scr[...] = jnp.zeros(l_scr.shape, F32)
    acc_scr[...] = jnp.zeros(acc_scr.shape, F32)

    def softmax_pv(j, s_ref, near, c):
        vtj = vt_ref[j]
        for ci in range(n_chunks):
            cols = slice(ci * cq, (ci + 1) * cq)
            s = s_ref[:, cols]
            if near:
                q0 = (ci * cq) % t
                s = s + bias_ref[0, j - i + NEAR_REACH, :, q0:q0 + cq]
            m_prev = m_scr[:, cols]
            m_cur = jnp.max(s, axis=0, keepdims=True)
            if c is not None:
                m_cur = m_cur + c
            m_next = jnp.maximum(m_prev, m_cur)
            p = jnp.exp2(s - (m_next if c is None else m_next - c))
            alpha = jnp.exp2(m_prev - m_next)
            l_scr[:, cols] = alpha * l_scr[:, cols] + jnp.sum(p, axis=0, keepdims=True)
            acc_scr[:, cols] = alpha * acc_scr[:, cols] + jnp.dot(vtj, p.astype(BF16),
                                                                  preferred_element_type=F32)
            m_scr[:, cols] = m_next

    def pair(m, near, c, ahead_tile, ahead_slot):
        j0 = 2 * m
        sb_scr[...] = scores(j0 + 1, cur)
        softmax_pv(j0, sa_scr, near, c)
        sa_scr[...] = scores(ahead_tile, ahead_slot)
        softmax_pv(j0 + 1, sb_scr, near, c)

    n_left, right_start, n_far, near_lo, near_hi = plan(i)
    next_first = first_tile(jnp.minimum(i + 1, nk - 1))

    def far_pair(f):
        def index(g):
            return jnp.where(g < n_left, g, g - n_left + right_start)
        ahead = jnp.where(f + 1 < n_far, index(f + 1), near_lo)
        c = jnp.where(f < n_left, far_ref[h, 0], far_ref[h, 1])
        pair(index(f), False, c, 2 * ahead, cur)

    lead = jnp.where(n_far >= NEAR_LEAD, NEAR_LEAD, 0)
    n_loop = n_far - lead
    size = 1
    while size < FAR_BLOCK:
        @pl.when(jnp.bitwise_and(n_loop, size) != 0)
        def _(size=size):
            base = jnp.bitwise_and(n_loop, size - 1)
            for u in range(size):
                far_pair(base + u)
        size *= 2

    def far_block(u, carry):
        base = jnp.bitwise_and(n_loop, FAR_BLOCK - 1) + FAR_BLOCK * u
        for v in range(FAR_BLOCK):
            far_pair(base + v)
        return carry

    lax.fori_loop(0, lax.div(n_loop, FAR_BLOCK), far_block, 0)

    def near_block(n_lead, both):
        for v in range(n_lead):
            far_pair(n_loop + v)
        if both:
            pair(near_lo, True, None, 2 * near_hi, cur)
            pair(near_hi, True, None, next_first, nxt)
        else:
            pair(near_lo, True, None, next_first, nxt)

    for n_lead in sorted({0, NEAR_LEAD}):
        for both in (True, False):
            @pl.when(jnp.logical_and(lead == n_lead, (near_hi > near_lo) == both))
            def _(n_lead=n_lead, both=both):
                near_block(n_lead, both)

    o = acc_scr[...] * (1.0 / l_scr[...])
    prod = lq_ref[...] * lk_ref[...]
    lam = (jnp.exp(jnp.sum(prod[0:1], axis=1, keepdims=True))
           - jnp.exp(jnp.sum(prod[1:2], axis=1, keepdims=True)) + lam_init)
    a = (o[:, 0:t] - lam * o[:, t:2 * t]).T
    o_ref[...] = (_rms(a, hn_ref[...]) * (1.0 - lam_init)).astype(o_ref.dtype)


def _attention(q, k, vt, bias_near_t, far, lam_q, lam_k, head_norm, lam_init, tile, seq_len):
    t_all = q.shape[0]
    b = t_all // seq_len
    nk = seq_len // tile
    assert nk % 2 == 0, "key tiles are processed in pairs"
    kern = functools.partial(_attn_kernel, tile=tile, nk=nk, lam_init=lam_init)
    return pl.pallas_call(
        kern,
        grid=(b, N_DIFF_HEADS, nk),
        in_specs=[
            pl.BlockSpec(memory_space=pltpu.SMEM),
            pl.BlockSpec((tile, V_HEAD_DIM), lambda bi, h, i: (bi * nk, h)),
            pl.BlockSpec((tile, V_HEAD_DIM), lambda bi, h, i: (bi * nk + jnp.minimum(i + 1, nk - 1), h)),
            pl.BlockSpec((seq_len, V_HEAD_DIM), lambda bi, h, i: (bi, h)),
            pl.BlockSpec((nk, V_HEAD_DIM, tile), lambda bi, h, i: (bi, h, 0)),
            pl.BlockSpec((1, 2 * NEAR_REACH + 1, tile, tile), lambda bi, h, i: (h, 0, 0, 0)),
            pl.BlockSpec((2, DIFF_HEAD_DIM), lambda bi, h, i: (0, 0)),
            pl.BlockSpec((2, DIFF_HEAD_DIM), lambda bi, h, i: (0, 0)),
            pl.BlockSpec((1, V_HEAD_DIM), lambda bi, h, i: (0, 0)),
        ],
        out_specs=pl.BlockSpec((tile, V_HEAD_DIM), lambda bi, h, i: (bi * nk + i, h)),
        out_shape=jax.ShapeDtypeStruct((t_all, ATTN_WIDTH), BF16),
        scratch_shapes=[
            pltpu.VMEM((2, 2 * tile, V_HEAD_DIM), BF16),
            pltpu.VMEM((tile, 2 * tile), F32),
            pltpu.VMEM((tile, 2 * tile), F32),
            pltpu.VMEM((1, 2 * tile), F32),
            pltpu.VMEM((1, 2 * tile), F32),
            pltpu.VMEM((V_HEAD_DIM, 2 * tile), F32),
        ],
        compiler_params=_params("parallel", "parallel", "arbitrary"),
        name="diff_attention",
    )(far, q, q, k, vt, bias_near_t, lam_q, lam_k, head_norm)


def _mixout_kernel(a_ref, p_ref, pprev_ref, pnext_ref, x_ref, wout_ref, wpool_ref, ps_ref, g_ref,
                   o_ref, pbuf, *, tm, tiles_per_seq, seq_len):
    ti = lax.rem(pl.program_id(0), tiles_per_seq)
    keep_prev = (ti > 0).astype(F32)
    keep_next = (ti < tiles_per_seq - 1).astype(F32)
    pbuf[0:HALO, :] = pprev_ref[...] * keep_prev
    pbuf[HALO:HALO + tm, :] = p_ref[...]
    pbuf[HALO + tm:, :] = pnext_ref[...] * keep_next

    pos = ti * tm + lax.broadcasted_iota(jnp.int32, (tm, 1), 0)
    mixed = []
    for g, win in enumerate(POOL_WINDOWS):
        lo_off = -(win // 2)
        hi_off = win - win // 2 - 1
        cols = slice(g * POOL_GROUP, (g + 1) * POOL_GROUP)
        total = pbuf[HALO + lo_off:HALO + lo_off + tm, cols]
        for off in range(lo_off + 1, hi_off + 1):
            total = total + pbuf[HALO + off:HALO + off + tm, cols]
        lo = jnp.maximum(pos + lo_off, 0)
        hi = jnp.minimum(pos + hi_off, seq_len - 1)
        cnt = (hi - lo + 1).astype(F32)
        y = total / cnt - pbuf[HALO:HALO + tm, cols]
        mixed.append(jnp.dot(y.astype(BF16), wpool_ref[g], preferred_element_type=F32))
    m = jnp.concatenate(mixed, axis=1) * ps_ref[...]

    o = jnp.dot(a_ref[...], wout_ref[0:ATTN_WIDTH, :], preferred_element_type=F32)
    o = o + jnp.dot(m.astype(BF16), wout_ref[ATTN_WIDTH:, :], preferred_element_type=F32)
    o_ref[...] = x_ref[...] + _rms(o, g_ref[...])


def _halo_specs(tm, width, n_rows):
    per_tile = tm // HALO
    last = n_rows // HALO - 1
    prev = pl.BlockSpec((HALO, width), lambda i, *_: (jnp.maximum(i * per_tile - 1, 0), 0))
    nxt = pl.BlockSpec((HALO, width), lambda i, *_: (jnp.minimum((i + 1) * per_tile, last), 0))
    return prev, nxt


def _mixout(a, p, x, w_out, w_pool, pool_scale, g, seq_len, tm):
    t = x.shape[0]
    row = lambda i: (i, 0)
    const = lambda i: (0, 0)
    prev, nxt = _halo_specs(tm, POOL_WIDTH, t)
    kern = functools.partial(_mixout_kernel, tm=tm, tiles_per_seq=seq_len // tm, seq_len=seq_len)
    return pl.pallas_call(
        kern,
        grid=(t // tm,),
        in_specs=[
            pl.BlockSpec((tm, ATTN_WIDTH), row),
            pl.BlockSpec((tm, POOL_WIDTH), row),
            prev,
            nxt,
            pl.BlockSpec((tm, D_MODEL), row),
            pl.BlockSpec(w_out.shape, const),
            pl.BlockSpec(w_pool.shape, lambda i: (0, 0, 0)),
            pl.BlockSpec((1, POOL_WIDTH), const),
            pl.BlockSpec((1, D_MODEL), const),
        ],
        out_specs=pl.BlockSpec((tm, D_MODEL), row),
        out_shape=jax.ShapeDtypeStruct((t, D_MODEL), F32),
        scratch_shapes=[pltpu.VMEM((tm + 2 * HALO, POOL_WIDTH), F32)],
        compiler_params=_params("parallel"),
        name="pool_outproj",
    )(a, p, p, p, x, w_out, w_pool, pool_scale, g)


def _gelu_tanh(x):
    return 0.5 * x * (1.0 + jnp.tanh(math.sqrt(2.0 / math.pi) * (x + 0.044715 * (x * x * x))))


def _ffn_kernel(x_ref, xprev_ref, xnext_ref, gpre_ref, wg_ref, wv_ref, cwg_ref, cwv_ref, cbg_ref, cbv_ref,
                wd_ref, gpost_ref, o_ref, h_scr, u_scr, acc_scr, *, tm, tiles_per_seq):
    c = pl.program_id(1)

    @pl.when(c == 0)
    def _():
        ti = lax.rem(pl.program_id(0), tiles_per_seq)
        keep_prev = (ti > 0).astype(F32)
        keep_next = (ti < tiles_per_seq - 1).astype(F32)
        g = gpre_ref[...]
        h_scr[0:HALO, :] = _rms(xprev_ref[...], g) * keep_prev
        h_scr[HALO:HALO + tm, :] = _rms(x_ref[...], g)
        h_scr[HALO + tm:, :] = _rms(xnext_ref[...], g) * keep_next
        acc_scr[...] = jnp.zeros(acc_scr.shape, F32)

    h = h_scr[...].astype(BF16)

    def conv(w_ref, cw_ref, cb_ref):
        u_scr[...] = jnp.dot(h, w_ref[...], preferred_element_type=F32)
        cw = cw_ref[...]
        return (u_scr[HALO - 1:HALO - 1 + tm, :] * cw[0:1]
                + u_scr[HALO:HALO + tm, :] * cw[1:2]
                + u_scr[HALO + 1:HALO + 1 + tm, :] * cw[2:3]
                + cb_ref[...])

    gate = conv(wg_ref, cwg_ref, cbg_ref)
    val = conv(wv_ref, cwv_ref, cbv_ref)
    f = (_gelu_tanh(gate) * val).astype(BF16)
    acc_scr[...] += jnp.dot(f, wd_ref[...], preferred_element_type=F32)

    @pl.when(c == pl.num_programs(1) - 1)
    def _():
        o_ref[...] = x_ref[...] + _rms(acc_scr[...], gpost_ref[...])


def _ffn(x, g_pre, w_up, conv_w, conv_b, w_down, g_post, seq_len, tm):
    t = x.shape[0]
    n_chunks = D_FF // FF_CHUNK
    row = lambda i, c: (i, 0)
    const = lambda i, c: (0, 0)
    gate_cols = lambda i, c: (0, c)
    val_cols = lambda i, c: (0, n_chunks + c)
    prev, nxt = _halo_specs(tm, D_MODEL, t)
    kern = functools.partial(_ffn_kernel, tm=tm, tiles_per_seq=seq_len // tm)
    return pl.pallas_call(
        kern,
        grid=(t // tm, n_chunks),
        in_specs=[
            pl.BlockSpec((tm, D_MODEL), row),
            prev,
            nxt,
            pl.BlockSpec((1, D_MODEL), const),
            pl.BlockSpec((D_MODEL, FF_CHUNK), gate_cols),
            pl.BlockSpec((D_MODEL, FF_CHUNK), val_cols),
            pl.BlockSpec((3, FF_CHUNK), gate_cols),
            pl.BlockSpec((3, FF_CHUNK), val_cols),
            pl.BlockSpec((1, FF_CHUNK), gate_cols),
            pl.BlockSpec((1, FF_CHUNK), val_cols),
            pl.BlockSpec((FF_CHUNK, D_MODEL), lambda i, c: (c, 0)),
            pl.BlockSpec((1, D_MODEL), const),
        ],
        out_specs=pl.BlockSpec((tm, D_MODEL), row),
        out_shape=jax.ShapeDtypeStruct((t, D_MODEL), F32),
        scratch_shapes=[
            pltpu.VMEM((tm + 2 * HALO, D_MODEL), F32),
            pltpu.VMEM((tm + 2 * HALO, FF_CHUNK), F32),
            pltpu.VMEM((tm, D_MODEL), F32),
        ],
        compiler_params=_params("parallel", "arbitrary"),
        name="conv_mlp",
    )(x, x, x, g_pre, w_up, w_up, conv_w, conv_w, conv_b, conv_b, w_down, g_post)


def _rel_bucket(rel):
    nb = NUM_BUCKETS // 2
    max_exact = nb // 2
    ret = jnp.where(rel > 0, nb, 0)
    n = jnp.abs(rel)
    nf = jnp.maximum(n, 1).astype(F32)
    large = max_exact + (jnp.log(nf / max_exact) / math.log(MAX_DISTANCE / max_exact)
                         * (nb - max_exact)).astype(jnp.int32)
    large = jnp.minimum(large, nb - 1)
    return ret + jnp.where(n < max_exact, n, large)


def _bias_expand_kernel(y_ref, o_ref, *, tile):
    span = y_ref.shape[-1]
    table = jnp.broadcast_to(y_ref[0], (tile, span))
    skewed = pltpu.roll(table, 0, 1, stride=1, stride_axis=0)
    for di, d in enumerate(range(-NEAR_REACH, NEAR_REACH + 1)):
        start = (-d * tile) % span
        o_ref[0, di] = skewed[:, start:start + tile]


def _bias_tables(rel_bias, tile):
    assert tile >= MAX_DISTANCE, "far tiles must lie beyond the last distance bucket"
    n_near = 2 * NEAR_REACH + 1
    span = 2 * (NEAR_REACH + 1) * tile
    m = jnp.arange(span, dtype=jnp.int32)
    rel = jnp.where(m < span // 2, -m, span - m)
    rel_bias = rel_bias.astype(F32) * LOG2E
    by_rel = rel_bias[_rel_bucket(rel)].T[:, None, :]
    near = pl.pallas_call(
        functools.partial(_bias_expand_kernel, tile=tile),
        grid=(N_DIFF_HEADS,),
        in_specs=[pl.BlockSpec((1, 1, span), lambda h: (h, 0, 0))],
        out_specs=pl.BlockSpec((1, n_near, tile, tile), lambda h: (h, 0, 0, 0)),
        out_shape=jax.ShapeDtypeStruct((N_DIFF_HEADS, n_near, tile, tile), F32),
        compiler_params=_params("parallel"),
        name="bias_expand",
    )(by_rel)
    far = jnp.stack([rel_bias[NUM_BUCKETS // 2 - 1], rel_bias[NUM_BUCKETS - 1]], axis=1)
    return near, far


def kernel(x_prompt, x_sample, rel_bias, ln_mix_pre, ln_mix_post, w_in, lam_q, lam_k, head_norm, w_pool,
           pool_scale, w_out, ln_ffn_pre, ln_ffn_post, w_up, conv_w, conv_b, w_down):
    depth = w_in.shape[0]
    w_in_b, w_out_b, w_pool_b = w_in.astype(BF16), w_out.astype(BF16), w_pool.astype(BF16)
    w_up_b, w_down_b = w_up.astype(BF16), w_down.astype(BF16)
    w_vt_b = jnp.swapaxes(w_in_b[:, :, 2 * ATTN_WIDTH:3 * ATTN_WIDTH], 1, 2)
    tables = {}

    def trunk(x):
        b, s, d = x.shape
        tile = min(ATTN_TILE, s)
        tm = min(ROW_TILE, s)
        assert s % tile == 0 and s % tm == 0 and tm % HALO == 0
        if tile not in tables:
            tables[tile] = _bias_tables(rel_bias, tile)
        bias_near_t, far = tables[tile]
        xf = x.reshape(b * s, d)
        for i in range(depth):
            lam_init = 0.8 - 0.6 * math.exp(-0.3 * i)
            q, k, vt, p = _inproj(xf, ln_mix_pre[i][None], w_in_b[i], w_vt_b[i], tile)
            a = _attention(q, k, vt, bias_near_t, far, lam_q[i], lam_k[i], head_norm[i][None], lam_init,
                           tile, s)
            xf = _mixout(a, p, xf, w_out_b[i], w_pool_b[i], pool_scale[i][None], ln_mix_post[i][None], s, tm)
            xf = _ffn(xf, ln_ffn_pre[i][None], w_up_b[i], conv_w[i], conv_b[i][None], w_down_b[i],
                      ln_ffn_post[i][None], s, tm)
        return xf.reshape(b, s, d)

    return (trunk(x_prompt), trunk(x_sample))
```

```python
import functools
import math

import jax
import jax.numpy as jnp
from jax import lax
from jax.experimental import pallas as pl
from jax.experimental.pallas import tpu as pltpu

D_MODEL = 1024
ATTN_WIDTH = 512
POOL_WIDTH = D_MODEL - ATTN_WIDTH
N_DIFF_HEADS = 4
DIFF_HEAD_DIM = 64
V_HEAD_DIM = 2 * DIFF_HEAD_DIM
POOL_WINDOWS = (2, 4, 8, 16)
POOL_GROUP = POOL_WIDTH // len(POOL_WINDOWS)
D_FF = 2816
NUM_BUCKETS = 32
MAX_DISTANCE = 128
RMS_EPS = 1e-6
QK_SCALE = DIFF_HEAD_DIM ** -0.5
LOG2E = math.log2(math.e)

LANES = 128
SUBLANES = 8
MXU_WIDTH = 256
ATTN_TILE = 512
ROW_TILE = 512
FF_CHUNK = D_FF // 2
HALO = SUBLANES
FAR_BLOCK = 8
NEAR_LEAD = 6
NEAR_REACH = 2
MASK_VALUE = -1e30
VMEM_LIMIT_BYTES = 48 * 1024 * 1024

F32 = jnp.float32
BF16 = jnp.bfloat16
NT_DIMS = (((1,), (1,)), ((), ()))


def _rms(x, g):
    var = jnp.mean(x * x, axis=-1, keepdims=True)
    return x * lax.rsqrt(var + RMS_EPS) * g


def _params(*semantics):
    return pltpu.CompilerParams(dimension_semantics=semantics, vmem_limit_bytes=VMEM_LIMIT_BYTES)


def _inproj_kernel(x_ref, g_ref, w_ref, wvt_ref, q_ref, k_ref, vt_ref, p_ref):
    h = _rms(x_ref[...], g_ref[...]).astype(BF16)
    a = ATTN_WIDTH
    q = jnp.dot(h, w_ref[:, 0:a], preferred_element_type=F32)
    q_ref[...] = (q * (QK_SCALE * LOG2E)).astype(BF16)
    k_ref[...] = jnp.dot(h, w_ref[:, a:2 * a], preferred_element_type=F32).astype(BF16)
    p_ref[...] = jnp.dot(h, w_ref[:, 3 * a:], preferred_element_type=F32)
    vt_ref[0] = lax.dot_general(wvt_ref[...], h, NT_DIMS, preferred_element_type=F32).astype(BF16)


def _inproj(x, g, w_in, w_vt, tm):
    t = x.shape[0]
    row = lambda i: (i, 0)
    const = lambda i: (0, 0)
    return pl.pallas_call(
        _inproj_kernel,
        grid=(t // tm,),
        in_specs=[
            pl.BlockSpec((tm, D_MODEL), row),
            pl.BlockSpec((1, D_MODEL), const),
            pl.BlockSpec(w_in.shape, const),
            pl.BlockSpec(w_vt.shape, const),
        ],
        out_specs=[
            pl.BlockSpec((tm, ATTN_WIDTH), row),
            pl.BlockSpec((tm, ATTN_WIDTH), row),
            pl.BlockSpec((1, ATTN_WIDTH, tm), lambda i: (i, 0, 0)),
            pl.BlockSpec((tm, POOL_WIDTH), row),
        ],
        out_shape=[
            jax.ShapeDtypeStruct((t, ATTN_WIDTH), BF16),
            jax.ShapeDtypeStruct((t, ATTN_WIDTH), BF16),
            jax.ShapeDtypeStruct((t // tm, ATTN_WIDTH, tm), BF16),
            jax.ShapeDtypeStruct((t, POOL_WIDTH), F32),
        ],
        compiler_params=_params("parallel"),
        name="inproj",
    )(x, g, w_in, w_vt)


def _attn_kernel(far_ref, q_ref, qn_ref, k_ref, vt_ref, bias_ref, lq_ref, lk_ref, hn_ref, o_ref,
                 qs_scr, sa_scr, sb_scr, m_scr, l_scr, acc_scr, *, tile, nk, lam_init):
    h = pl.program_id(1)
    i = pl.program_id(2)
    t = tile
    cq = min(MXU_WIDTH, t)
    n_chunks = 2 * t // cq

    cur = lax.rem(i, 2)
    nxt = 1 - cur

    def stack_maps(slot, q):
        lane = lax.broadcasted_iota(jnp.int32, (t, LANES), 1)
        zero = jnp.zeros_like(q)
        qs_scr[slot, 0:t, :] = jnp.where(lane < DIFF_HEAD_DIM, q, zero)
        qs_scr[slot, t:2 * t, :] = jnp.where(lane >= DIFF_HEAD_DIM, q, zero)

    def scores(j, slot):
        start = pl.multiple_of(j * t, t)
        return lax.dot_general(k_ref[pl.ds(start, t), :], qs_scr[slot], NT_DIMS,
                               preferred_element_type=F32)

    n_pairs = nk // 2

    def plan(qi):
        first_near = lax.div(qi + 1, 2) - 1
        n_left = jnp.maximum(first_near, 0)
        right_start = first_near + 2
        n_far = n_left + jnp.maximum(n_pairs - right_start, 0)
        return n_left, right_start, n_far, n_left, jnp.minimum(first_near + 1, n_pairs - 1)

    def first_tile(qi):
        n_left_q, right_start_q, n_far_q, near_lo_q, _ = plan(qi)
        first_far = jnp.where(n_left_q > 0, 0, right_start_q)
        return 2 * jnp.where(n_far_q > 0, first_far, near_lo_q)

    @pl.when(i == 0)
    def _():
        stack_maps(0, q_ref[...])
        sa_scr[...] = scores(first_tile(0), 0)

    stack_maps(nxt, qn_ref[...])
    m_scr[...] = jnp.full(m_scr.shape, MASK_VALUE, F32)
    l_scr[...] = jnp.zeros(l_scr.shape, F32)
    acc_scr[...] = jnp.zeros(acc_scr.shape, F32)

    def softmax_pv(j, s_ref, near, c):
        vtj = vt_ref[j]
        for ci in range(n_chunks):
            cols = slice(ci * cq, (ci + 1) * cq)
            s = s_ref[:, cols]
            if near:
                q0 = (ci * cq) % t
                s = s + bias_ref[0, j - i + NEAR_REACH, :, q0:q0 + cq]
            m_prev = m_scr[:, cols]
            m_cur = jnp.max(s, axis=0, keepdims=True)
            if c is not None:
                m_cur = m_cur + c
            m_next = jnp.maximum(m_prev, m_cur)
            p = jnp.exp2(s - (m_next if c is None else m_next - c))
            alpha = jnp.exp2(m_prev - m_next)
            l_scr[:, cols] = alpha * l_scr[:, cols] + jnp.sum(p, axis=0, keepdims=True)
            acc_scr[:, cols] = alpha * acc_scr[:, cols] + jnp.dot(vtj, p.astype(BF16),
                                                                  preferred_element_type=F32)
            m_scr[:, cols] = m_next

    def pair(m, near, c, ahead_tile, ahead_slot):
        j0 = 2 * m
        sb_scr[...] = scores(j0 + 1, cur)
        softmax_pv(j0, sa_scr, near, c)
        sa_scr[...] = scores(ahead_tile, ahead_slot)
        softmax_pv(j0 + 1, sb_scr, near, c)

    n_left, right_start, n_far, near_lo, near_hi = plan(i)
    next_first = first_tile(jnp.minimum(i + 1, nk - 1))

    def far_pair(f):
        def index(g):
            return jnp.where(g < n_left, g, g - n_left + right_start)
        ahead = jnp.where(f + 1 < n_far, index(f + 1), near_lo)
        c = jnp.where(f < n_left, far_ref[h, 0], far_ref[h, 1])
        pair(index(f), False, c, 2 * ahead, cur)

    lead = jnp.where(n_far >= NEAR_LEAD, NEAR_LEAD, 0)
    n_loop = n_far - lead
    size = 1
    while size < FAR_BLOCK:
        @pl.when(jnp.bitwise_and(n_loop, size) != 0)
        def _(size=size):
            base = jnp.bitwise_and(n_loop, size - 1)
            for u in range(size):
                far_pair(base + u)
        size *= 2

    def far_block(u, carry):
        base = jnp.bitwise_and(n_loop, FAR_BLOCK - 1) + FAR_BLOCK * u
        for v in range(FAR_BLOCK):
            far_pair(base + v)
        return carry

    lax.fori_loop(0, lax.div(n_loop, FAR_BLOCK), far_block, 0)

    def near_block(n_lead, both):
        for v in range(n_lead):
            far_pair(n_loop + v)
        if both:
            pair(near_lo, True, None, 2 * near_hi, cur)
            pair(near_hi, True, None, next_first, nxt)
        else:
            pair(near_lo, True, None, next_first, nxt)

    for n_lead in sorted({0, NEAR_LEAD}):
        for both in (True, False):
            @pl.when(jnp.logical_and(lead == n_lead, (near_hi > near_lo) == both))
            def _(n_lead=n_lead, both=both):
                near_block(n_lead, both)

    o = acc_scr[...] * (1.0 / l_scr[...])
    prod = lq_ref[...] * lk_ref[...]
    lam = (jnp.exp(jnp.sum(prod[0:1], axis=1, keepdims=True))
           - jnp.exp(jnp.sum(prod[1:2], axis=1, keepdims=True)) + lam_init)
    a = (o[:, 0:t] - lam * o[:, t:2 * t]).T
    o_ref[...] = (_rms(a, hn_ref[...]) * (1.0 - lam_init)).astype(o_ref.dtype)


def _attention(q, k, vt, bias_near_t, far, lam_q, lam_k, head_norm, lam_init, tile, seq_len):
    t_all = q.shape[0]
    b = t_all // seq_len
    nk = seq_len // tile
    assert nk % 2 == 0, "key tiles are processed in pairs"
    kern = functools.partial(_attn_kernel, tile=tile, nk=nk, lam_init=lam_init)
    return pl.pallas_call(
        kern,
        grid=(b, N_DIFF_HEADS, nk),
        in_specs=[
            pl.BlockSpec(memory_space=pltpu.SMEM),
            pl.BlockSpec((tile, V_HEAD_DIM), lambda bi, h, i: (bi * nk + i, h)),
            pl.BlockSpec((tile, V_HEAD_DIM), lambda bi, h, i: (bi * nk + jnp.minimum(i + 1, nk - 1), h)),
            pl.BlockSpec((seq_len, V_HEAD_DIM), lambda bi, h, i: (bi, h)),
            pl.BlockSpec((nk, V_HEAD_DIM, tile), lambda bi, h, i: (bi, h, 0)),
            pl.BlockSpec((1, 2 * NEAR_REACH + 1, tile, tile), lambda bi, h, i: (h, 0, 0, 0)),
            pl.BlockSpec((2, DIFF_HEAD_DIM), lambda bi, h, i: (0, 0)),
            pl.BlockSpec((2, DIFF_HEAD_DIM), lambda bi, h, i: (0, 0)),
            pl.BlockSpec((1, V_HEAD_DIM), lambda bi, h, i: (0, 0)),
        ],
        out_specs=pl.BlockSpec((tile, V_HEAD_DIM), lambda bi, h, i: (bi * nk + i, h)),
        out_shape=jax.ShapeDtypeStruct((t_all, ATTN_WIDTH), BF16),
        scratch_shapes=[
            pltpu.VMEM((2, 2 * tile, V_HEAD_DIM), BF16),
            pltpu.VMEM((tile, 2 * tile), F32),
            pltpu.VMEM((tile, 2 * tile), F32),
            pltpu.VMEM((1, 2 * tile), F32),
            pltpu.VMEM((1, 2 * tile), F32),
            pltpu.VMEM((V_HEAD_DIM, 2 * tile), F32),
        ],
        compiler_params=_params("parallel", "parallel", "arbitrary"),
        name="diff_attention",
    )(far, q, q, k, vt, bias_near_t, lam_q, lam_k, head_norm)


def _mixout_kernel(a_ref, p_ref, pprev_ref, pnext_ref, x_ref, wout_ref, wpool_ref, ps_ref, g_ref,
                   o_ref, pbuf, *, tm, tiles_per_seq, seq_len):
    ti = lax.rem(pl.program_id(0), tiles_per_seq)
    keep_prev = (ti > 0).astype(F32)
    keep_next = (ti < tiles_per_seq - 1).astype(F32)
    pbuf[0:HALO, :] = pprev_ref[...] * keep_prev
    pbuf[HALO:HALO + tm, :] = p_ref[...]
    pbuf[HALO + tm:, :] = pnext_ref[...] * keep_next

    pos = ti * tm + lax.broadcasted_iota(jnp.int32, (tm, 1), 0)
    mixed = []
    for g, win in enumerate(POOL_WINDOWS):
        lo_off = -(win // 2)
        hi_off = win - win // 2 - 1
        cols = slice(g * POOL_GROUP, (g + 1) * POOL_GROUP)
        total = pbuf[HALO + lo_off:HALO + lo_off + tm, cols]
        for off in range(lo_off + 1, hi_off + 1):
            total = total + pbuf[HALO + off:HALO + off + tm, cols]
        lo = jnp.maximum(pos + lo_off, 0)
        hi = jnp.minimum(pos + hi_off, seq_len - 1)
        cnt = (hi - lo + 1).astype(F32)
        y = total / cnt - pbuf[HALO:HALO + tm, cols]
        mixed.append(jnp.dot(y.astype(BF16), wpool_ref[g], preferred_element_type=F32))
    m = jnp.concatenate(mixed, axis=1) * ps_ref[...]

    o = jnp.dot(a_ref[...], wout_ref[0:ATTN_WIDTH, :], preferred_element_type=F32)
    o = o + jnp.dot(m.astype(BF16), wout_ref[ATTN_WIDTH:, :], preferred_element_type=F32)
    o_ref[...] = x_ref[...] + _rms(o, g_ref[...])


def _halo_specs(tm, width, n_rows):
    per_tile = tm // HALO
    last = n_rows // HALO - 1
    prev = pl.BlockSpec((HALO, width), lambda i, *_: (jnp.maximum(i * per_tile - 1, 0), 0))
    nxt = pl.BlockSpec((HALO, width), lambda i, *_: (jnp.minimum((i + 1) * per_tile, last), 0))
    return prev, nxt


def _mixout(a, p, x, w_out, w_pool, pool_scale, g, seq_len, tm):
    t = x.shape[0]
    row = lambda i: (i, 0)
    const = lambda i: (0, 0)
    prev, nxt = _halo_specs(tm, POOL_WIDTH, t)
    kern = functools.partial(_mixout_kernel, tm=tm, tiles_per_seq=seq_len // tm, seq_len=seq_len)
    return pl.pallas_call(
        kern,
        grid=(t // tm,),
        in_specs=[
            pl.BlockSpec((tm, ATTN_WIDTH), row),
            pl.BlockSpec((tm, POOL_WIDTH), row),
            prev,
            nxt,
            pl.BlockSpec((tm, D_MODEL), row),
            pl.BlockSpec(w_out.shape, const),
            pl.BlockSpec(w_pool.shape, lambda i: (0, 0, 0)),
            pl.BlockSpec((1, POOL_WIDTH), const),
            pl.BlockSpec((1, D_MODEL), const),
        ],
        out_specs=pl.BlockSpec((tm, D_MODEL), row),
        out_shape=jax.ShapeDtypeStruct((t, D_MODEL), F32),
        scratch_shapes=[pltpu.VMEM((tm + 2 * HALO, POOL_WIDTH), F32)],
        compiler_params=_params("parallel"),
        name="pool_outproj",
    )(a, p, p, p, x, w_out, w_pool, pool_scale, g)


def _gelu_tanh(x):
    return 0.5 * x * (1.0 + jnp.tanh(math.sqrt(2.0 / math.pi) * (x + 0.044715 * (x * x * x))))


def _ffn_kernel(x_ref, xprev_ref, xnext_ref, gpre_ref, wg_ref, wv_ref, cwg_ref, cwv_ref, cbg_ref, cbv_ref,
                wd_ref, gpost_ref, o_ref, h_scr, u_scr, acc_scr, *, tm, tiles_per_seq):
    c = pl.program_id(1)

    @pl.when(c == 0)
    def _():
        ti = lax.rem(pl.program_id(0), tiles_per_seq)
        keep_prev = (ti > 0).astype(F32)
        keep_next = (ti < tiles_per_seq - 1).astype(F32)
        g = gpre_ref[...]
        h_scr[0:HALO, :] = _rms(xprev_ref[...], g) * keep_prev
        h_scr[HALO:HALO + tm, :] = _rms(x_ref[...], g)
        h_scr[HALO + tm:, :] = _rms(xnext_ref[...], g) * keep_next
        acc_scr[...] = jnp.zeros(acc_scr.shape, F32)

    h = h_scr[...].astype(BF16)

    def conv(w_ref, cw_ref, cb_ref, u_ref):
        u_ref[...] = jnp.dot(h, w_ref[...], preferred_element_type=F32)
        cw = cw_ref[...]
        return (u_ref[HALO - 1:HALO - 1 + tm, :] * cw[0:1]
                + u_ref[HALO:HALO + tm, :] * cw[1:2]
                + u_ref[HALO + 1:HALO + 1 + tm, :] * cw[2:3]
                + cb_ref[...])

    gate = conv(wg_ref, cwg_ref, cbg_ref, u_scr.at[0])
    val = conv(wv_ref, cwv_ref, cbv_ref, u_scr.at[1])
    f = (_gelu_tanh(gate) * val).astype(BF16)
    acc_scr[...] += jnp.dot(f, wd_ref[...], preferred_element_type=F32)

    @pl.when(c == pl.num_programs(1) - 1)
    def _():
        o_ref[...] = x_ref[...] + _rms(acc_scr[...], gpost_ref[...])


def _ffn(x, g_pre, w_up, conv_w, conv_b, w_down, g_post, seq_len, tm):
    t = x.shape[0]
    n_chunks = D_FF // FF_CHUNK
    row = lambda i, c: (i, 0)
    const = lambda i, c: (0, 0)
    gate_cols = lambda i, c: (0, c)
    val_cols = lambda i, c: (0, n_chunks + c)
    prev, nxt = _halo_specs(tm, D_MODEL, t)
    kern = functools.partial(_ffn_kernel, tm=tm, tiles_per_seq=seq_len // tm)
    return pl.pallas_call(
        kern,
        grid=(t // tm, n_chunks),
        in_specs=[
            pl.BlockSpec((tm, D_MODEL), row),
            prev,
            nxt,
            pl.BlockSpec((1, D_MODEL), const),
            pl.BlockSpec((D_MODEL, FF_CHUNK), gate_cols),
            pl.BlockSpec((D_MODEL, FF_CHUNK), val_cols),
            pl.BlockSpec((3, FF_CHUNK), gate_cols),
            pl.BlockSpec((3, FF_CHUNK), val_cols),
            pl.BlockSpec((1, FF_CHUNK), gate_cols),
            pl.BlockSpec((1, FF_CHUNK), val_cols),
            pl.BlockSpec((FF_CHUNK, D_MODEL), lambda i, c: (c, 0)),
            pl.BlockSpec((1, D_MODEL), const),
        ],
        out_specs=pl.BlockSpec((tm, D_MODEL), row),
        out_shape=jax.ShapeDtypeStruct((t, D_MODEL), F32),
        scratch_shapes=[
            pltpu.VMEM((tm + 2 * HALO, D_MODEL), F32),
            pltpu.VMEM((2, tm + 2 * HALO, FF_CHUNK), F32),
            pltpu.VMEM((tm, D_MODEL), F32),
        ],
        compiler_params=_params("parallel", "arbitrary"),
        name="conv_mlp",
    )(x, x, x, g_pre, w_up, w_up, conv_w, conv_w, conv_b, conv_b, w_down, g_post)


def _rel_bucket(rel):
    nb = NUM_BUCKETS // 2
    max_exact = nb // 2
    ret = jnp.where(rel > 0, nb, 0)
    n = jnp.abs(rel)
    nf = jnp.maximum(n, 1).astype(F32)
    large = max_exact + (jnp.log(nf / max_exact) / math.log(MAX_DISTANCE / max_exact)
                         * (nb - max_exact)).astype(jnp.int32)
    large = jnp.minimum(large, nb - 1)
    return ret + jnp.where(n < max_exact, n, large)


def _bias_expand_kernel(y_ref, o_ref, *, tile):
    span = y_ref.shape[-1]
    table = jnp.broadcast_to(y_ref[0], (tile, span))
    skewed = pltpu.roll(table, 0, 1, stride=1, stride_axis=0)
    for di, d in enumerate(range(-NEAR_REACH, NEAR_REACH + 1)):
        start = (-d * tile) % span
        o_ref[0, di] = skewed[:, start:start + tile]


def _bias_tables(rel_bias, tile):
    assert tile >= MAX_DISTANCE, "far tiles must lie beyond the last distance bucket"
    n_near = 2 * NEAR_REACH + 1
    span = 2 * (NEAR_REACH + 1) * tile
    m = jnp.arange(span, dtype=jnp.int32)
    rel = jnp.where(m < span // 2, -m, span - m)
    rel_bias = rel_bias.astype(F32) * LOG2E
    by_rel = rel_bias[_rel_bucket(rel)].T[:, None, :]
    near = pl.pallas_call(
        functools.partial(_bias_expand_kernel, tile=tile),
        grid=(N_DIFF_HEADS,),
        in_specs=[pl.BlockSpec((1, 1, span), lambda h: (h, 0, 0))],
        out_specs=pl.BlockSpec((1, n_near, tile, tile), lambda h: (h, 0, 0, 0)),
        out_shape=jax.ShapeDtypeStruct((N_DIFF_HEADS, n_near, tile, tile), F32),
        compiler_params=_params("parallel"),
        name="bias_expand",
    )(by_rel)
    far = jnp.stack([rel_bias[NUM_BUCKETS // 2 - 1], rel_bias[NUM_BUCKETS - 1]], axis=1)
    return near, far


def kernel(x_prompt, x_sample, rel_bias, ln_mix_pre, ln_mix_post, w_in, lam_q, lam_k, head_norm, w_pool,
           pool_scale, w_out, ln_ffn_pre, ln_ffn_post, w_up, conv_w, conv_b, w_down):
    depth = w_in.shape[0]
    w_in_b, w_out_b, w_pool_b = w_in.astype(BF16), w_out.astype(BF16), w_pool.astype(BF16)
    w_up_b, w_down_b = w_up.astype(BF16), w_down.astype(BF16)
    w_vt_b = jnp.swapaxes(w_in_b[:, :, 2 * ATTN_WIDTH:3 * ATTN_WIDTH], 1, 2)
    tables = {}

    def trunk(x):
        b, s, d = x.shape
        tile = min(ATTN_TILE, s)
        tm = min(ROW_TILE, s)
        assert s % tile == 0 and s % tm == 0 and tm % HALO == 0
        if tile not in tables:
            tables[tile] = _bias_tables(rel_bias, tile)
        bias_near_t, far = tables[tile]
        xf = x.reshape(b * s, d)
        for i in range(depth):
            lam_init = 0.8 - 0.6 * math.exp(-0.3 * i)
            q, k, vt, p = _inproj(xf, ln_mix_pre[i][None], w_in_b[i], w_vt_b[i], tile)
            a = _attention(q, k, vt, bias_near_t, far, lam_q[i], lam_k[i], head_norm[i][None], lam_init,
                           tile, s)
            xf = _mixout(a, p, xf, w_out_b[i], w_pool_b[i], pool_scale[i][None], ln_mix_post[i][None], s, tm)
            xf = _ffn(xf, ln_ffn_pre[i][None], w_up_b[i], conv_w[i], conv_b[i][None], w_down_b[i],
                      ln_ffn_post[i][None], s, tm)
        return xf.reshape(b, s, d)

    return (trunk(x_prompt), trunk(x_sample))
```
